```python
import math, functools
import jax, jax.numpy as jnp
from jax import lax
import numpy as np

D_MODEL = 1024
BATCH = 2
SEQ = 8192
DEPTH = 4
DEC_BATCH = 32
DEC_SEQ = 4
PAST_LEN = 8192
PAGE_SIZE = 128

N_BRANCH = 4
BRANCH_WIDTH = D_MODEL // N_BRANCH
HEAD_DIM = 64
N_HEADS = BRANCH_WIDTH // HEAD_DIM
Q_BLOCK = 128
NORM_EPS = 1e-6
NEG = -1e30
NSA_BLOCK = 64
NSA_TOPK = 16
NSA_WINDOW = 512
NSA_FORCED = 2.0 * N_HEADS
SSM_GROUPS = 2
SSM_STATE = 128
SSM_CONV = 4
SSM_CHUNK = 128
SSM_CONV_DIM = BRANCH_WIDTH + 2 * SSM_GROUPS * SSM_STATE
RWKV_W_LORA = 64
RWKV_A_LORA = 64
RWKV_G_LORA = 128
RWKV_IN = 3 * BRANCH_WIDTH + RWKV_W_LORA + RWKV_A_LORA + RWKV_G_LORA
RWKV_LN_EPS = 64e-5
D_FF = 2816
FFN_CONV = 3
IN_SIZES = (BRANCH_WIDTH, 6 * HEAD_DIM, 3 * N_HEADS, BRANCH_WIDTH, SSM_CONV_DIM, N_HEADS, RWKV_IN, 3 * BRANCH_WIDTH, N_BRANCH * D_MODEL)
D_IN = BRANCH_WIDTH + 6 * HEAD_DIM + 3 * N_HEADS + BRANCH_WIDTH + SSM_CONV_DIM + N_HEADS + RWKV_IN + 3 * BRANCH_WIDTH + N_BRANCH * D_MODEL

kernel_name = 'hybrid_nsa_ssd_rwkv7_stickbreak_step'


def rmsnorm(x, g):
    xf = x.astype(jnp.float32)
    return xf * lax.rsqrt(jnp.mean(xf * xf, axis=-1, keepdims=True) + NORM_EPS) * g


def split_sizes(u, sizes):
    return jnp.split(u, [int(i) for i in np.cumsum(sizes)[:-1]], axis=-1)


def causal_dwconv(u, buf, w, b):
    K = w.shape[0]
    T = u.shape[1]
    full = jnp.concatenate([buf.astype(u.dtype), u], axis=1)
    out = sum(full[:, k:k + T] * w[k] for k in range(K)) + b
    return out, full[:, T:]


def alibi_slopes():
    return 2.0 ** (-8.0 * (jnp.arange(N_HEADS, dtype=jnp.float32) + 1.0) / N_HEADS)


def sweep_query_blocks(fn, n_blocks):
    out = lax.map(fn, jnp.arange(n_blocks))
    out = jnp.moveaxis(out, 0, 1)
    return out.reshape((out.shape[0], -1) + out.shape[3:])


def gather_pages(pool, page_table):
    g = pool[page_table]
    return g.reshape((g.shape[0], -1) + g.shape[3:])


def to_blocks(u):
    b, T, d = u.shape
    nbl = -(-T // NSA_BLOCK)
    u = jnp.pad(u, ((0, 0), (0, nbl * NSA_BLOCK - T), (0, 0)))
    return u.reshape(b, nbl, NSA_BLOCK, d)


def nsa_compress(k_raw, v_raw, cmp_pe, cmp_w, k_norm):
    b, T, d = k_raw.shape
    nb = T // NSA_BLOCK

    def comp(u, pe, w):
        blocks = u[:, :nb * NSA_BLOCK].reshape(b, nb, NSA_BLOCK, d) + pe
        return jnp.einsum('bnm,me->bne', blocks.reshape(b, nb, NSA_BLOCK * d), w)

    return rmsnorm(comp(k_raw, cmp_pe[0], cmp_w[0]), k_norm), comp(v_raw, cmp_pe[1], cmp_w[1])


def nsa_attend(q, q_pos, gates, cmp_k, cmp_v, sel_k, sel_v, win_k, win_v, win_pos):
    b, tq, h, d = q.shape
    scale = d ** -0.5
    slopes = alibi_slopes()
    qf = q.astype(jnp.float32)
    nb = cmp_k.shape[1]
    dist_c = q_pos[:, None] - ((jnp.arange(nb) + 1) * NSA_BLOCK - 1)[None, :]
    valid_c = dist_c >= 0
    s_c = jnp.einsum('bqhd,bnd->bhqn', qf, cmp_k) * scale - slopes[:, None, None] * dist_c
    s_c = jnp.where(valid_c, s_c, NEG)
    p_c = jax.nn.softmax(s_c, axis=-1) * valid_c
    o_c = jnp.einsum('bhqn,bnd->bqhd', p_c, cmp_v)
    nbl = sel_k.shape[1]
    blk = jnp.arange(nbl)[None, :]
    cur = (q_pos // NSA_BLOCK)[:, None]
    imp = jnp.pad(p_c.sum(axis=1), ((0, 0), (0, 0), (0, nbl - nb)))
    forced = (blk == 0) | (blk == cur) | (blk == cur - 1)
    score = jnp.where(blk > cur, NEG, jnp.where(forced, NSA_FORCED, imp))
    top_s, top_i = lax.top_k(score, min(NSA_TOPK, nbl))
    bidx = jnp.arange(b)[:, None, None]
    ks = sel_k[bidx, top_i]
    vs = sel_v[bidx, top_i]
    kpos = top_i[..., None] * NSA_BLOCK + jnp.arange(NSA_BLOCK)
    dist_s = q_pos[None, :, None, None] - kpos
    valid_s = (dist_s >= 0) & (top_s > 0.5 * NEG)[..., None]
    s_s = jnp.einsum('bqhd,bqkld->bhqkl', qf, ks) * scale - slopes[None, :, None, None, None] * dist_s[:, None]
    s_s = jnp.where(valid_s[:, None], s_s, NEG).reshape(b, h, tq, -1)
    o_s = jnp.einsum('bhqm,bqmd->bqhd', jax.nn.softmax(s_s, axis=-1), vs.reshape(b, tq, -1, d))
    dist_w = q_pos[:, None] - win_pos[None, :]
    valid_w = (dist_w >= 0) & (dist_w < NSA_WINDOW) & (win_pos >= 0)[None, :]
    s_w = jnp.einsum('bqhd,bkd->bhqk', qf, win_k) * scale - slopes[:, None, None] * dist_w
    s_w = jnp.where(valid_w, s_w, NEG)
    o_w = jnp.einsum('bhqk,bkd->bqhd', jax.nn.softmax(s_w, axis=-1), win_v)
    g = jax.nn.sigmoid(gates.astype(jnp.float32))
    return g[..., 0:1] * o_c + g[..., 1:2] * o_s + g[..., 2:3] * o_w


def nsa_prompt(q, gates, new_k, new_v, new_win, cmp_pe, cmp_w, k_norm):
    b, T = q.shape[:2]
    kc, vc = nsa_compress(new_k[:, :, 0], new_v[:, :, 0], cmp_pe, cmp_w, k_norm)
    ks, vs = to_blocks(new_k[:, :, 1]), to_blocks(new_v[:, :, 1])
    pad = ((0, 0), (NSA_WINDOW, 0), (0, 0))
    kw, vw = jnp.pad(new_win[:, :, 0], pad), jnp.pad(new_win[:, :, 1], pad)
    span = NSA_WINDOW + Q_BLOCK

    def one(i):
        start = i * Q_BLOCK
        sl = lambda t, n: lax.dynamic_slice_in_dim(t, start, n, axis=1)
        return nsa_attend(sl(q, Q_BLOCK), start + jnp.arange(Q_BLOCK), sl(gates, Q_BLOCK), kc, vc, ks, vs,
                          sl(kw, span), sl(vw, span), start - NSA_WINDOW + jnp.arange(span))

    out = sweep_query_blocks(one, T // Q_BLOCK)
    return out, new_win[:, -min(NSA_WINDOW, T):]


def nsa_sample(past_k, past_v, win_buf, q, gates, new_k, new_v, new_win, cmp_pe, cmp_w, k_norm):
    T = q.shape[1]
    past = past_k.shape[1]
    all_k = jnp.concatenate([past_k, new_k], axis=1)
    all_v = jnp.concatenate([past_v, new_v], axis=1)
    kc, vc = nsa_compress(all_k[:, :, 0], all_v[:, :, 0], cmp_pe, cmp_w, k_norm)
    ks, vs = to_blocks(all_k[:, :, 1]), to_blocks(all_v[:, :, 1])
    n_buf = win_buf.shape[1]
    win = jnp.concatenate([win_buf, new_win], axis=1)
    out = nsa_attend(q, past + jnp.arange(T), gates, kc, vc, ks, vs, win[:, :, 0], win[:, :, 1],
                     past - n_buf + jnp.arange(n_buf + T))
    return out, win[:, -n_buf:]


def stick_breaking_attend(q, q_pos, k, v, k_pos):
    z = jnp.einsum('bqhd,bkhd->bhqk', q.astype(jnp.float32), k.astype(jnp.float32)) * HEAD_DIM ** -0.5
    mask = k_pos[None, :] < q_pos[:, None]
    log_1mb = jnp.where(mask, jax.nn.log_sigmoid(-z), 0.0)
    after = lax.cumsum(log_1mb, axis=3, reverse=True) - log_1mb
    a = jnp.where(mask, jnp.exp(jax.nn.log_sigmoid(z) + after), 0.0)
    return jnp.einsum('bhqk,bkhd->bqhd', a, v.astype(jnp.float32))


def sb_prompt(q, k, v):
    T = q.shape[1]
    k_pos = jnp.arange(T)

    def one(i):
        qb = lax.dynamic_slice_in_dim(q, i * Q_BLOCK, Q_BLOCK, axis=1)
        return stick_breaking_attend(qb, i * Q_BLOCK + jnp.arange(Q_BLOCK), k, v, k_pos)

    return sweep_query_blocks(one, T // Q_BLOCK)


def sb_sample(past_k, past_v, q, k, v):
    T = q.shape[1]
    past = past_k.shape[1]
    k_all = jnp.concatenate([past_k, k], axis=1)
    v_all = jnp.concatenate([past_v, v], axis=1)
    return stick_breaking_attend(q, past + jnp.arange(T), k_all, v_all, jnp.arange(past + T))


def ssd_chunked(x, dt, A, B, C, h0):
    b, T, h, p = x.shape
    n = B.shape[-1]
    rep = h // B.shape[2]
    Q = SSM_CHUNK if T % SSM_CHUNK == 0 else T
    nc = T // Q
    xf = x.astype(jnp.float32).reshape(b, nc, Q, h, p)
    Bh = jnp.repeat(B.astype(jnp.float32), rep, axis=2).reshape(b, nc, Q, h, n)
    Ch = jnp.repeat(C.astype(jnp.float32), rep, axis=2).reshape(b, nc, Q, h, n)
    dtc = dt.reshape(b, nc, Q, h)
    cum = jnp.cumsum(dtc * A, axis=2)
    seg = cum[:, :, :, None, :] - cum[:, :, None, :, :]
    causal = jnp.tril(jnp.ones((Q, Q), bool))[None, None, :, :, None]
    L = jnp.exp(jnp.where(causal, seg, -jnp.inf))
    scores = jnp.einsum('bclhn,bcshn->bclsh', Ch, Bh) * L * dtc[:, :, None, :, :]
    y_intra = jnp.einsum('bclsh,bcshp->bclhp', scores, xf)
    decay_to_end = jnp.exp(cum[:, :, -1:, :] - cum)
    chunk_states = jnp.einsum('bcshn,bcsh,bcshp->bchpn', Bh, decay_to_end * dtc, xf)
    chunk_decay = jnp.exp(cum[:, :, -1, :])

    def step(hc, inp):
        s_c, d_c = inp
        return d_c[:, :, None, None] * hc + s_c, hc

    h_final, h_starts = lax.scan(step, h0.astype(jnp.float32),
                                 (jnp.moveaxis(chunk_states, 1, 0), jnp.moveaxis(chunk_decay, 1, 0)))
    h_starts = jnp.moveaxis(h_starts, 0, 1)
    y_inter = jnp.einsum('bclhn,bchpn,bclh->bclhp', Ch, h_starts, jnp.exp(cum))
    return (y_intra + y_inter).reshape(b, T, h, p), h_final


def mamba_mix(z, xbc, dt_raw, conv_buf, h0, conv_w, conv_b, dt_bias, a_log, d_skip, norm_g):
    b, T, _ = xbc.shape
    xbc_c, new_buf = causal_dwconv(xbc, conv_buf, conv_w, conv_b)
    xs, Bm, Cm = split_sizes(jax.nn.silu(xbc_c), (BRANCH_WIDTH, SSM_GROUPS * SSM_STATE, SSM_GROUPS * SSM_STATE))
    xs = xs.reshape(b, T, N_HEADS, HEAD_DIM)
    Bm = Bm.reshape(b, T, SSM_GROUPS, SSM_STATE)
    Cm = Cm.reshape(b, T, SSM_GROUPS, SSM_STATE)
    dt = jax.nn.softplus(dt_raw.astype(jnp.float32) + dt_bias)
    A = -jnp.exp(a_log.astype(jnp.float32))
    y, h_final = ssd_chunked(xs, dt, A, Bm, Cm, h0)
    y = (y + d_skip[:, None] * xs).reshape(b, T, BRANCH_WIDTH) * jax.nn.silu(z)
    y = rmsnorm(y.reshape(b, T, SSM_GROUPS, -1), 1.0).reshape(b, T, BRANCH_WIDTH) * norm_g
    return y, new_buf, h_final


def rwkv7_mix(u, shift_buf, s0, mu, w0, w2, a0, a2, g2, k_k, k_a, r_k, ln_w, ln_b):
    b, T, _ = u.shape
    uf = u.astype(jnp.float32)
    prev = jnp.concatenate([shift_buf.astype(jnp.float32), uf[:, :-1]], axis=1)
    us = uf + (prev - uf) * mu
    r, k, v, wd, ad, gd = split_sizes(us, (BRANCH_WIDTH, BRANCH_WIDTH, BRANCH_WIDTH, RWKV_W_LORA, RWKV_A_LORA, RWKV_G_LORA))
    w_raw = -jax.nn.softplus(-(w0 + jnp.tanh(wd) @ w2)) - 0.5
    decay = jnp.exp(-jnp.exp(w_raw))
    a = jax.nn.sigmoid(a0 + ad @ a2)
    g = jax.nn.sigmoid(gd) @ g2
    heads = lambda t: t.reshape(b, T, N_HEADS, HEAD_DIM)
    r, k, v, a, decay = heads(r), heads(k), heads(v), heads(a), heads(decay)
    kk = k * k_k.reshape(N_HEADS, HEAD_DIM)
    kk = kk / jnp.maximum(jnp.sqrt(jnp.sum(kk * kk, axis=-1, keepdims=True)), 1e-12)
    k = k * (1.0 + (a - 1.0) * k_a.reshape(N_HEADS, HEAD_DIM))

    def step(S, inp):
        r_t, k_t, v_t, d_t, kk_t, kka_t = inp
        S = (S * d_t[:, :, None, :]
             + jnp.einsum('bhvk,bhk->bhv', S, -kk_t)[..., None] * kka_t[:, :, None, :]
             + v_t[..., None] * k_t[:, :, None, :])
        return S, jnp.einsum('bhvk,bhk->bhv', S, r_t)

    seqs = tuple(jnp.moveaxis(t, 1, 0) for t in (r, k, v, decay, kk, kk * a))
    S_final, y = lax.scan(step, s0.astype(jnp.float32), seqs)
    y = jnp.moveaxis(y, 0, 1)
    mean = jnp.mean(y, axis=-1, keepdims=True)
    var = jnp.mean(jnp.square(y - mean), axis=-1, keepdims=True)
    y = ((y - mean) * lax.rsqrt(var + RWKV_LN_EPS)).reshape(b, T, BRANCH_WIDTH) * ln_w + ln_b
    bonus = jnp.sum(r * k * r_k, axis=-1, keepdims=True) * v
    y = (y + bonus.reshape(b, T, BRANCH_WIDTH)) * g
    return y, uf[:, -1:], S_final


def conv_ffn(x, buf, w_up, conv_w, conv_b, w_down):
    h, new_buf = causal_dwconv(x @ w_up, buf, conv_w, conv_b)
    g, u = jnp.split(h, 2, axis=-1)
    return (jax.nn.silu(g) * u) @ w_down, new_buf


def layer_forward(x, nsa_fn, sb_fn, ssm_buf, ssm_h, rwkv_buf, rwkv_s, ffn_buf, W):
    b, T, _ = x.shape
    xn = rmsnorm(x, W['norm1'])
    nsa_q, nsa_kv, nsa_gate, ssm_z, ssm_xbc, ssm_dt, rwkv_u, sb_qkv, merge = split_sizes(xn @ W['w_in'], IN_SIZES)
    heads = lambda t: t.reshape(b, T, N_HEADS, HEAD_DIM)
    q = rmsnorm(heads(nsa_q), W['nsa_q_norm'])
    k_cmp, v_cmp, k_slc, v_slc, k_win, v_win = jnp.split(nsa_kv, 6, axis=-1)
    new_k = jnp.stack([k_cmp.astype(jnp.float32), rmsnorm(k_slc, W['nsa_k_norm'])], axis=2)
    new_v = jnp.stack([v_cmp, v_slc], axis=2)
    new_win = jnp.stack([rmsnorm(k_win, W['nsa_k_norm']), v_win.astype(jnp.float32)], axis=2)
    o_a, win_state = nsa_fn(q, nsa_gate.reshape(b, T, N_HEADS, 3), new_k, new_v, new_win,
                            W['nsa_cmp_pe'], W['nsa_cmp_w'], W['nsa_k_norm'])
    o_b, ssm_buf, ssm_h = mamba_mix(ssm_z, ssm_xbc, ssm_dt, ssm_buf, ssm_h, W['ssm_conv_w'], W['ssm_conv_b'],
                                    W['ssm_dt_bias'], W['ssm_a_log'], W['ssm_d'], W['ssm_norm'])
    o_c, rwkv_buf, rwkv_s = rwkv7_mix(rwkv_u, rwkv_buf, rwkv_s, W['rwkv_mu'], W['rwkv_w0'], W['rwkv_w2'],
                                      W['rwkv_a0'], W['rwkv_a2'], W['rwkv_g2'], W['rwkv_k_k'], W['rwkv_k_a'],
                                      W['rwkv_r_k'], W['rwkv_ln_w'], W['rwkv_ln_b'])
    sq, sk, sv = [heads(t) for t in jnp.split(sb_qkv, 3, axis=-1)]
    o_d = sb_fn(sq, sk, sv)
    branches = jnp.stack([o_a.reshape(b, T, BRANCH_WIDTH), o_b, o_c, o_d.reshape(b, T, BRANCH_WIDTH)], axis=2)
    proj = jnp.einsum('btnc,ncd->btnd', branches, W['w_branch'])
    gate = jax.nn.sigmoid(merge.reshape(b, T, N_BRANCH, D_MODEL))
    x = x + jnp.sum(gate * proj, axis=2) @ W['w_out']
    f, ffn_buf = conv_ffn(rmsnorm(x, W['norm2']), ffn_buf, W['ffn_up'], W['ffn_conv_w'], W['ffn_conv_b'], W['ffn_down'])
    x = x + f
    return x, (new_k, new_v, sk, sv, win_state, ssm_buf, ssm_h, rwkv_buf, rwkv_s, ffn_buf)


def setup_inputs(seed: int = 0) -> dict:
    key = jax.random.key(seed)
    keys = iter(jax.random.split(key, 64))

    def nrm(shape, scale=1.0):
        return jax.random.normal(next(keys), shape, jnp.float32) * scale

    def gain(shape):
        return 1.0 + nrm(shape, 0.01)

    n_pages = PAST_LEN // PAGE_SIZE
    n_pool = (DEC_BATCH * n_pages * 5) // 4
    w_buf = min(NSA_WINDOW, PAST_LEN)
    perm = jax.random.permutation(next(keys), n_pool)
    page_table = perm[:DEC_BATCH * n_pages].reshape(DEC_BATCH, n_pages).astype(jnp.int32)
    dt0 = jnp.exp(jax.random.uniform(next(keys), (DEPTH, N_HEADS), jnp.float32, math.log(1e-3), math.log(1e-1)))
    return {
        'x_prompt': nrm((BATCH, SEQ, D_MODEL)),
        'x_sample': nrm((DEC_BATCH, DEC_SEQ, D_MODEL)),
        'cache_nsa_k': nrm((DEPTH, n_pool, PAGE_SIZE, 2, HEAD_DIM)),
        'cache_nsa_v': nrm((DEPTH, n_pool, PAGE_SIZE, 2, HEAD_DIM)),
        'cache_sb_k': nrm((DEPTH, n_pool, PAGE_SIZE, N_HEADS, HEAD_DIM)),
        'cache_sb_v': nrm((DEPTH, n_pool, PAGE_SIZE, N_HEADS, HEAD_DIM)),
        'state_win_kv': nrm((DEPTH, DEC_BATCH, w_buf, 2, HEAD_DIM)),
        'state_ssm_conv': nrm((DEPTH, DEC_BATCH, SSM_CONV - 1, SSM_CONV_DIM)),
        'state_ssm': nrm((DEPTH, DEC_BATCH, N_HEADS, HEAD_DIM, SSM_STATE), 0.1),
        'state_rwkv_shift': nrm((DEPTH, DEC_BATCH, 1, RWKV_IN)),
        'state_rwkv': nrm((DEPTH, DEC_BATCH, N_HEADS, HEAD_DIM, HEAD_DIM), 0.1),
        'state_ffn_conv': nrm((DEPTH, DEC_BATCH, FFN_CONV - 1, 2 * D_FF)),
        'page_table': page_table,
        'norm1': gain((DEPTH, D_MODEL)),
        'norm2': gain((DEPTH, D_MODEL)),
        'w_in': nrm((DEPTH, D_MODEL, D_IN), D_MODEL ** -0.5),
        'nsa_q_norm': gain((DEPTH, HEAD_DIM)),
        'nsa_k_norm': gain((DEPTH, HEAD_DIM)),
        'nsa_cmp_pe': nrm((DEPTH, 2, NSA_BLOCK, HEAD_DIM), 0.1),
        'nsa_cmp_w': nrm((DEPTH, 2, NSA_BLOCK * HEAD_DIM, HEAD_DIM), (NSA_BLOCK * HEAD_DIM) ** -0.5),
        'ssm_conv_w': nrm((DEPTH, SSM_CONV, SSM_CONV_DIM), SSM_CONV ** -0.5),
        'ssm_conv_b': nrm((DEPTH, SSM_CONV_DIM), 0.01),
        'ssm_dt_bias': dt0 + jnp.log(-jnp.expm1(-dt0)),
        'ssm_a_log': jnp.log(jax.random.uniform(next(keys), (DEPTH, N_HEADS), jnp.float32, 1.0, 16.0)),
        'ssm_d': 1.0 + nrm((DEPTH, N_HEADS), 0.1),
        'ssm_norm': gain((DEPTH, BRANCH_WIDTH)),
        'rwkv_mu': jax.random.uniform(next(keys), (DEPTH, RWKV_IN), jnp.float32),
        'rwkv_w0': nrm((DEPTH, BRANCH_WIDTH), 0.5),
        'rwkv_w2': nrm((DEPTH, RWKV_W_LORA, BRANCH_WIDTH), 0.1),
        'rwkv_a0': nrm((DEPTH, BRANCH_WIDTH), 0.1),
        'rwkv_a2': nrm((DEPTH, RWKV_A_LORA, BRANCH_WIDTH), 0.1),
        'rwkv_g2': nrm((DEPTH, RWKV_G_LORA, BRANCH_WIDTH), RWKV_G_LORA ** -0.5),
        'rwkv_k_k': 0.85 + nrm((DEPTH, BRANCH_WIDTH), 0.05),
        'rwkv_k_a': 1.0 + nrm((DEPTH, BRANCH_WIDTH), 0.05),
        'rwkv_r_k': nrm((DEPTH, N_HEADS, HEAD_DIM), 0.1),
        'rwkv_ln_w': gain((DEPTH, BRANCH_WIDTH)),
        'rwkv_ln_b': nrm((DEPTH, BRANCH_WIDTH), 0.01),
        'w_branch': nrm((DEPTH, N_BRANCH, BRANCH_WIDTH, D_MODEL), BRANCH_WIDTH ** -0.5),
        'w_out': nrm((DEPTH, D_MODEL, D_MODEL), D_MODEL ** -0.5),
        'ffn_up': nrm((DEPTH, D_MODEL, 2 * D_FF), D_MODEL ** -0.5),
        'ffn_conv_w': nrm((DEPTH, FFN_CONV, 2 * D_FF), FFN_CONV ** -0.5),
        'ffn_conv_b': nrm((DEPTH, 2 * D_FF), 0.01),
        'ffn_down': nrm((DEPTH, D_FF, D_MODEL), D_FF ** -0.5),
    }


def reference(x_prompt, x_sample, cache_nsa_k, cache_nsa_v, cache_sb_k, cache_sb_v, state_win_kv,
              state_ssm_conv, state_ssm, state_rwkv_shift, state_rwkv, state_ffn_conv, page_table,
              norm1, norm2, w_in, nsa_q_norm, nsa_k_norm, nsa_cmp_pe, nsa_cmp_w,
              ssm_conv_w, ssm_conv_b, ssm_dt_bias, ssm_a_log, ssm_d, ssm_norm,
              rwkv_mu, rwkv_w0, rwkv_w2, rwkv_a0, rwkv_a2, rwkv_g2, rwkv_k_k, rwkv_k_a, rwkv_r_k,
              rwkv_ln_w, rwkv_ln_b, w_branch, w_out, ffn_up, ffn_conv_w, ffn_conv_b, ffn_down):
    f32 = jnp.float32
    bp = x_prompt.shape[0]
    yp, ys = x_prompt, x_sample
    states_p, states_s = [], []
    for l in range(DEPTH):
        W = dict(norm1=norm1[l], norm2=norm2[l], w_in=w_in[l], nsa_q_norm=nsa_q_norm[l], nsa_k_norm=nsa_k_norm[l],
                 nsa_cmp_pe=nsa_cmp_pe[l], nsa_cmp_w=nsa_cmp_w[l], ssm_conv_w=ssm_conv_w[l], ssm_conv_b=ssm_conv_b[l],
                 ssm_dt_bias=ssm_dt_bias[l], ssm_a_log=ssm_a_log[l], ssm_d=ssm_d[l], ssm_norm=ssm_norm[l],
                 rwkv_mu=rwkv_mu[l], rwkv_w0=rwkv_w0[l], rwkv_w2=rwkv_w2[l], rwkv_a0=rwkv_a0[l], rwkv_a2=rwkv_a2[l],
                 rwkv_g2=rwkv_g2[l], rwkv_k_k=rwkv_k_k[l], rwkv_k_a=rwkv_k_a[l], rwkv_r_k=rwkv_r_k[l],
                 rwkv_ln_w=rwkv_ln_w[l], rwkv_ln_b=rwkv_ln_b[l], w_branch=w_branch[l], w_out=w_out[l],
                 ffn_up=ffn_up[l], ffn_conv_w=ffn_conv_w[l], ffn_conv_b=ffn_conv_b[l], ffn_down=ffn_down[l])
        yp, st = layer_forward(yp, nsa_prompt, sb_prompt,
                               jnp.zeros((bp, SSM_CONV - 1, SSM_CONV_DIM), f32),
                               jnp.zeros((bp, N_HEADS, HEAD_DIM, SSM_STATE), f32),
                               jnp.zeros((bp, 1, RWKV_IN), f32),
                               jnp.zeros((bp, N_HEADS, HEAD_DIM, HEAD_DIM), f32),
                               jnp.zeros((bp, FFN_CONV - 1, 2 * D_FF), f32), W)
        states_p.append(st)
        nsa_fn = functools.partial(nsa_sample, gather_pages(cache_nsa_k[l], page_table),
                                   gather_pages(cache_nsa_v[l], page_table), state_win_kv[l])
        sb_fn = functools.partial(sb_sample, gather_pages(cache_sb_k[l], page_table),
                                  gather_pages(cache_sb_v[l], page_table))
        ys, st = layer_forward(ys, nsa_fn, sb_fn, state_ssm_conv[l], state_ssm[l], state_rwkv_shift[l],
                               state_rwkv[l], state_ffn_conv[l], W)
        states_s.append(st)
    (nsa_k_p, nsa_v_p, sb_k_p, sb_v_p, win_p, ssm_conv_p, ssm_p, rwkv_shift_p, rwkv_p, ffn_conv_p) = [jnp.stack(s) for s in zip(*states_p)]
    (nsa_k_s, nsa_v_s, sb_k_s, sb_v_s, win_s, ssm_conv_s, ssm_s, rwkv_shift_s, rwkv_s, ffn_conv_s) = [jnp.stack(s) for s in zip(*states_s)]
    return (yp, ys, nsa_k_p, nsa_k_s, nsa_v_p, nsa_v_s, sb_k_p, sb_k_s, sb_v_p, sb_v_s, win_p, win_s,
            ssm_conv_p, ssm_conv_s, ssm_p, ssm_s, rwkv_shift_p, rwkv_shift_s, rwkv_p, rwkv_s, ffn_conv_p, ffn_conv_s)
```

```python
import functools
import math

import numpy as np
import jax
import jax.numpy as jnp
from jax import lax
from jax.experimental import pallas as pl
from jax.experimental.pallas import tpu as pltpu

F32 = jnp.float32
BF16 = jnp.bfloat16
HIGHEST = lax.Precision.HIGHEST

D_MODEL = 1024
N_BRANCH = 4
BRANCH_WIDTH = D_MODEL // N_BRANCH
HEAD_DIM = 64
N_HEADS = BRANCH_WIDTH // HEAD_DIM
Q_BLOCK = 128
PAGE_SIZE = 128
NORM_EPS = 1e-6
NEG = -1e30
NSA_BLOCK = 64
NSA_TOPK = 16
NSA_WINDOW = 512
NSA_FORCED = 2.0 * N_HEADS
SSM_GROUPS = 2
SSM_STATE = 128
SSM_CONV = 4
SSM_CHUNK = 128
SSM_CONV_DIM = BRANCH_WIDTH + 2 * SSM_GROUPS * SSM_STATE
RWKV_W_LORA = 64
RWKV_A_LORA = 64
RWKV_G_LORA = 128
RWKV_IN = 3 * BRANCH_WIDTH + RWKV_W_LORA + RWKV_A_LORA + RWKV_G_LORA
RWKV_LN_EPS = 64e-5
D_FF = 2816
FFN_CONV = 3

LANES = 128
VMEM_LIMIT = 56 * 1024 * 1024

SEG_MERGE = 0
SEG_RWKV = 4096
SEG_NSA_Q = 5120
SEG_SSM_XBC = 5376
SEG_SB = 6144
SEG_NSA_KV = 6912
SEG_NSA_GATE = 7296
SEG_SSM_Z = 7424
SEG_SSM_DT = 7680
D_IN_PAD = 8192


def _in_perm():
    sizes = (BRANCH_WIDTH, 6 * HEAD_DIM, 3 * N_HEADS, BRANCH_WIDTH, SSM_CONV_DIM, N_HEADS, RWKV_IN,
             3 * BRANCH_WIDTH, N_BRANCH * D_MODEL)
    off = np.concatenate([[0], np.cumsum(sizes)])
    o_q, o_kv, o_gate, o_z, o_xbc, o_dt, o_rwkv, o_sb, o_merge = off[:-1]
    perm = -np.ones((D_IN_PAD,), np.int64)
    perm[SEG_MERGE:SEG_MERGE + 4096] = o_merge + np.arange(4096)
    perm[SEG_RWKV:SEG_RWKV + RWKV_IN] = o_rwkv + np.arange(RWKV_IN)
    perm[SEG_NSA_Q:SEG_NSA_Q + 256] = o_q + np.arange(256)
    perm[SEG_SSM_XBC:SEG_SSM_XBC + 768] = o_xbc + np.arange(768)
    perm[SEG_SB:SEG_SB + 768] = o_sb + np.arange(768)
    kv_order = (0, 2, 1, 3, 4, 5)
    for j, src in enumerate(kv_order):
        perm[SEG_NSA_KV + 64 * j:SEG_NSA_KV + 64 * (j + 1)] = o_kv + 64 * src + np.arange(64)
    perm[SEG_NSA_GATE:SEG_NSA_GATE + 12] = o_gate + np.arange(12)
    perm[SEG_SSM_Z:SEG_SSM_Z + 256] = o_z + np.arange(256)
    perm[SEG_SSM_DT:SEG_SSM_DT + 4] = o_dt + np.arange(4)
    return perm


def _cparams(*sem):
    return pltpu.CompilerParams(dimension_semantics=tuple(sem), vmem_limit_bytes=VMEM_LIMIT)


def _const_spec(shape):
    nd = len(shape)
    return pl.BlockSpec(shape, lambda *_: (0,) * nd)


def _iota(shape, dim):
    return lax.broadcasted_iota(jnp.int32, shape, dim)


def _dot(a, b):
    return jnp.dot(a, b, preferred_element_type=F32)


def _dot_hi(a, b):
    return jnp.dot(a, b, preferred_element_type=F32, precision=HIGHEST)


def _dot_nt(a, b):
    return lax.dot_general(a, b, (((1,), (1,)), ((), ())), preferred_element_type=F32)


def _sigmoid(x):
    return 1.0 / (1.0 + jnp.exp(-x))


def _silu(x):
    return x * _sigmoid(x)


def _softplus(x):
    return jnp.maximum(x, 0.0) + jnp.log(1.0 + jnp.exp(-jnp.abs(x)))


def _inproj_kernel(x_ref, g_ref, w_ref, o_ref, xn_ref):
    @pl.when(pl.program_id(1) == 0)
    def _():
        x = x_ref[...]
        ms = jnp.mean(x * x, axis=-1, keepdims=True)
        xn_ref[...] = (x * lax.rsqrt(ms + NORM_EPS) * g_ref[...]).astype(BF16)

    o_ref[...] = _dot(xn_ref[...], w_ref[...])


def _inproj(x, g, w, tm, tn=1024):
    n = x.shape[0]
    return pl.pallas_call(
        _inproj_kernel,
        grid=(n // tm, D_IN_PAD // tn),
        in_specs=[pl.BlockSpec((tm, D_MODEL), lambda i, j: (i, 0)),
                  pl.BlockSpec((1, D_MODEL), lambda i, j: (0, 0)),
                  pl.BlockSpec((D_MODEL, tn), lambda i, j: (0, j))],
        out_specs=pl.BlockSpec((tm, tn), lambda i, j: (i, j)),
        out_shape=jax.ShapeDtypeStruct((n, D_IN_PAD), F32),
        scratch_shapes=[pltpu.VMEM((tm, D_MODEL), BF16)],
        compiler_params=_cparams("parallel", "arbitrary"),
        name="inproj",
    )(x, g, w)


def _merge_kernel(x_ref, oa_ref, ob_ref, oc_ref, od_ref, gate_ref, wb_ref, wo_ref, out_ref):
    acc = None
    for n, o_ref in enumerate((oa_ref, ob_ref, oc_ref, od_ref)):
        proj = _dot(o_ref[...].astype(BF16), wb_ref[n])
        term = _sigmoid(gate_ref[:, n * D_MODEL:(n + 1) * D_MODEL]) * proj
        acc = term if acc is None else acc + term
    out_ref[...] = x_ref[...] + _dot(acc.astype(BF16), wo_ref[...])


def _merge(x, branches, u, wb, wo, tm):
    n = x.shape[0]
    row = lambda i: (i, 0)
    bspec = pl.BlockSpec((tm, BRANCH_WIDTH), row)
    return pl.pallas_call(
        _merge_kernel,
        grid=(n // tm,),
        in_specs=[pl.BlockSpec((tm, D_MODEL), row), bspec, bspec, bspec, bspec,
                  pl.BlockSpec((tm, N_BRANCH * D_MODEL), lambda i: (i, SEG_MERGE // (N_BRANCH * D_MODEL))),
                  _const_spec((N_BRANCH, BRANCH_WIDTH, D_MODEL)),
                  _const_spec((D_MODEL, D_MODEL))],
        out_specs=pl.BlockSpec((tm, D_MODEL), row),
        out_shape=jax.ShapeDtypeStruct((n, D_MODEL), F32),
        compiler_params=_cparams("parallel"),
        name="merge",
    )(x, *branches, u, wb, wo)


FFN_COLS = 256
FFN_HALO = 8


def _ffn_kernel(x_ref, xp_ref, p1_ref, p2_ref, g_ref, wup_ref, cw_ref, cb_ref, wd_ref,
                out_ref, hs_ref, xn_ref, acc_ref, *, seq_tiles, period, keep):
    tm = x_ref.shape[0]
    prompt = seq_tiles > 0
    halo = FFN_HALO if prompt else 0

    def norm(x):
        ms = jnp.mean(x * x, axis=-1, keepdims=True)
        return (x * lax.rsqrt(ms + NORM_EPS) * g_ref[...]).astype(BF16)

    x = x_ref[...]
    xn_ref[halo:halo + tm, :] = norm(x)
    if prompt:
        first = (pl.program_id(0) % seq_tiles) == 0
        xn_ref[0:halo, :] = norm(xp_ref[...])
        hist_ok = jnp.where(first, 0.0, 1.0)
    else:
        step = _iota((tm, 1), 0) % period
    acc_ref[...] = x

    for c in range(D_FF // FFN_COLS):
        gs = slice(c * FFN_COLS, (c + 1) * FFN_COLS)
        us = slice(D_FF + c * FFN_COLS, D_FF + (c + 1) * FFN_COLS)
        xn = xn_ref[...]
        conv = []
        for part, cs in enumerate((gs, us)):
            h = _dot(xn, wup_ref[:, cs])
            if prompt:
                rows = _iota((tm + halo, 1), 0)
                h = jnp.where(rows < halo, h * hist_ok, h)
            h1 = pltpu.roll(h, 1, axis=0)
            h2 = pltpu.roll(h, 2, axis=0)
            if not prompt:
                h1 = jnp.where(step >= 1, h1, p1_ref[:, cs])
                h2 = jnp.where(step >= 2, h2, p2_ref[:, cs])
            y = cw_ref[2:3, cs] * h + cw_ref[1:2, cs] * h1 + cw_ref[0:1, cs] * h2 + cb_ref[:, cs]
            conv.append(y[halo:, :])
            hs_ref[0, :, cs] = h[halo + tm - keep:, :]
        act = (_silu(conv[0]) * conv[1]).astype(BF16)
        acc_ref[...] += _dot(act, wd_ref[gs, :])
    out_ref[...] = acc_ref[...]


def _ffn(x, prev1, prev2, g, wup, cw, cb, wd, *, tm, seq_tiles, period, keep):
    n = x.shape[0]
    nt = n // tm
    prompt = seq_tiles > 0
    halo = FFN_HALO if prompt else 0
    hb = tm // FFN_HALO
    if prompt:
        xp_spec = pl.BlockSpec((FFN_HALO, D_MODEL), lambda i: (jnp.maximum(i * hb - 1, 0), 0))
        xp = x
        p_spec = _const_spec(prev1.shape)
    else:
        xp_spec = _const_spec((FFN_HALO, D_MODEL))
        xp = x
        p_spec = pl.BlockSpec((tm, 2 * D_FF), lambda i: (i, 0))
    single = dict(pipeline_mode=pl.Buffered(1))
    kern = functools.partial(_ffn_kernel, seq_tiles=seq_tiles, period=period, keep=keep)
    return pl.pallas_call(
        kern,
        grid=(nt,),
        in_specs=[pl.BlockSpec((tm, D_MODEL), lambda i: (i, 0)), xp_spec, p_spec, p_spec,
                  _const_spec((1, D_MODEL)),
                  pl.BlockSpec((D_MODEL, 2 * D_FF), lambda i: (0, 0), **single),
                  _const_spec((FFN_CONV, 2 * D_FF)), _const_spec((1, 2 * D_FF)),
                  pl.BlockSpec((D_FF, D_MODEL), lambda i: (0, 0), **single)],
        out_specs=[pl.BlockSpec((tm, D_MODEL), lambda i: (i, 0)),
                   pl.BlockSpec((1, keep, 2 * D_FF), lambda i: (i, 0, 0))],
        out_shape=[jax.ShapeDtypeStruct((n, D_MODEL), F32),
                   jax.ShapeDtypeStruct((nt, keep, 2 * D_FF), F32)],
        scratch_shapes=[pltpu.VMEM((tm + halo, D_MODEL), BF16), pltpu.VMEM((tm, D_MODEL), F32)],
        compiler_params=_cparams("parallel"),
        name="ffn",
    )(x, xp, prev1, prev2, g, wup, cw, cb, wd)


SB_KT = 256
SB_PAGES = 8


def _head_block_mask(rows_per_head, n_rows):
    r = _iota((n_rows, BRANCH_WIDTH), 0) // rows_per_head
    c = _iota((n_rows, BRANCH_WIDTH), 1) // HEAD_DIM
    return r == c


def _suffix_matrix(n):
    return jnp.where(_iota((n, n), 0) > _iota((n, n), 1), 1.0, 0.0).astype(BF16)


def _sb_block(qbd, k, v, mask, carry, tri):
    z = _dot_nt(qbd, k)
    sp = _softplus(z)
    l1 = -sp if mask is None else jnp.where(mask, -sp, 0.0)
    hi = l1.astype(BF16)
    lo = (l1 - hi.astype(F32)).astype(BF16)
    after = _dot(hi, tri) + _dot(lo, tri) + carry
    a = jnp.exp(z + l1 + after)
    if mask is not None:
        a = jnp.where(mask, a, 0.0)
    return _dot(a.astype(BF16), v), carry + jnp.sum(l1, axis=1, keepdims=True)


def _fold_heads(acc, rows_per_head):
    masked = jnp.where(_head_block_mask(rows_per_head, acc.shape[0]), acc, 0.0)
    out = masked[0:rows_per_head]
    for h in range(1, N_HEADS):
        out = out + masked[h * rows_per_head:(h + 1) * rows_per_head]
    return out


def _sb_prompt_kernel(q_ref, k_ref, v_ref, o_ref, acc_ref, carry_ref):
    i = pl.program_id(1)
    rows = N_HEADS * Q_BLOCK
    q = q_ref[...] * (HEAD_DIM ** -0.5)
    qbd = jnp.where(_head_block_mask(Q_BLOCK, rows), jnp.concatenate([q] * N_HEADS, axis=0), 0.0).astype(BF16)
    tri = _suffix_matrix(SB_KT)
    acc_ref[...] = jnp.zeros_like(acc_ref)
    carry_ref[...] = jnp.zeros_like(carry_ref)
    qpos = i * Q_BLOCK + _iota((rows, 1), 0) % Q_BLOCK
    n_chunks = (i * Q_BLOCK) // SB_KT + 1

    def body(jj, _):
        j = n_chunks - 1 - jj
        start = pl.multiple_of(j * SB_KT, SB_KT)
        kpos = start + _iota((1, SB_KT), 1)
        pv, carry = _sb_block(qbd, k_ref[pl.ds(start, SB_KT), :], v_ref[pl.ds(start, SB_KT), :],
                              kpos < qpos, carry_ref[...], tri)
        acc_ref[...] += pv
        carry_ref[...] = carry
        return 0

    lax.fori_loop(0, n_chunks, body, 0)
    o_ref[...] = _fold_heads(acc_ref[...], Q_BLOCK)


def _sb_prompt(u3, kb, vb):
    b, t, _ = u3.shape
    rows = N_HEADS * Q_BLOCK
    return pl.pallas_call(
        _sb_prompt_kernel,
        grid=(b, t // Q_BLOCK),
        in_specs=[pl.BlockSpec((None, Q_BLOCK, BRANCH_WIDTH), lambda bi, i: (bi, i, SEG_SB // BRANCH_WIDTH)),
                  pl.BlockSpec((None, t, BRANCH_WIDTH), lambda bi, i: (bi, 0, 0)),
                  pl.BlockSpec((None, t, BRANCH_WIDTH), lambda bi, i: (bi, 0, 0))],
        out_specs=pl.BlockSpec((None, Q_BLOCK, BRANCH_WIDTH), lambda bi, i: (bi, i, 0)),
        out_shape=jax.ShapeDtypeStruct((b, t, BRANCH_WIDTH), F32),
        scratch_shapes=[pltpu.VMEM((rows, BRANCH_WIDTH), F32), pltpu.VMEM((rows, 1), F32)],
        compiler_params=_cparams("parallel", "parallel"),
        name="sb_prompt",
    )(u3, kb, vb)


SAMPLE_ROWS = 8


def _sb_sample_kernel(pt_ref, q_ref, kn_ref, vn_ref, *refs, n_steps):
    k_refs = refs[:SB_PAGES]
    v_refs = refs[SB_PAGES:2 * SB_PAGES]
    o_ref, acc_ref, carry_ref = refs[2 * SB_PAGES:]
    s = pl.program_id(1)
    rows = N_HEADS * SAMPLE_ROWS
    qbd = (q_ref[...] * HEAD_DIM ** -0.5).astype(BF16)
    tri = _suffix_matrix(PAGE_SIZE)

    @pl.when(s == 0)
    def _():
        step = _iota((rows, 1), 0) % SAMPLE_ROWS
        col = _iota((1, PAGE_SIZE), 1)
        pv, carry = _sb_block(qbd, kn_ref[...].astype(BF16), vn_ref[...].astype(BF16), col < step,
                              jnp.zeros((rows, 1), F32), tri)
        acc_ref[...] = pv
        carry_ref[...] = carry

    for r in range(SB_PAGES):
        pv, carry = _sb_block(qbd, k_refs[r][...].astype(BF16), v_refs[r][...].astype(BF16), None,
                              carry_ref[...], tri)
        acc_ref[...] += pv
        carry_ref[...] = carry

    @pl.when(s == n_steps - 1)
    def _():
        o_ref[...] = _fold_heads(acc_ref[...], SAMPLE_ROWS)


def _sb_sample(page_table, qbd, kn, vn, cache_k, cache_v, layer, n_pool):
    b, n_pages = page_table.shape
    n_steps = n_pages // SB_PAGES
    rows = N_HEADS * SAMPLE_ROWS
    base = layer * n_pool

    def page_spec(r):
        return pl.BlockSpec((None, PAGE_SIZE, BRANCH_WIDTH),
                            lambda bi, s, pt: (base + pt[bi, n_pages - 1 - (s * SB_PAGES + r)], 0, 0))

    per_b = lambda shape: pl.BlockSpec((None,) + shape, lambda bi, s, pt: (bi, 0, 0))
    grid_spec = pltpu.PrefetchScalarGridSpec(
        num_scalar_prefetch=1,
        grid=(b, n_steps),
        in_specs=[per_b((rows, BRANCH_WIDTH)), per_b((PAGE_SIZE, BRANCH_WIDTH)), per_b((PAGE_SIZE, BRANCH_WIDTH))]
                 + [page_spec(r) for r in range(SB_PAGES)] * 2,
        out_specs=per_b((SAMPLE_ROWS, BRANCH_WIDTH)),
        scratch_shapes=[pltpu.VMEM((rows, BRANCH_WIDTH), F32), pltpu.VMEM((rows, 1), F32)],
    )
    return pl.pallas_call(
        functools.partial(_sb_sample_kernel, n_steps=n_steps),
        grid_spec=grid_spec,
        out_shape=jax.ShapeDtypeStruct((b, SAMPLE_ROWS, BRANCH_WIDTH), F32),
        compiler_params=_cparams("parallel", "arbitrary"),
        name="sb_sample",
    )(page_table, qbd, kn, vn, *([cache_k] * SB_PAGES), *([cache_v] * SB_PAGES))


NSA_KT = 512
ALIBI_SLOPES = tuple(2.0 ** (-8.0 * (h + 1.0) / N_HEADS) for h in range(N_HEADS))


def _row_slopes(rq):
    h = _iota((N_HEADS * rq, 1), 0) // rq
    out = jnp.full((N_HEADS * rq, 1), ALIBI_SLOPES[0], F32)
    for i in range(1, N_HEADS):
        out = jnp.where(h == i, ALIBI_SLOPES[i], out)
    return out


def _rep_right_matrix():
    r = _iota((LANES, BRANCH_WIDTH), 0)
    c = _iota((LANES, BRANCH_WIDTH), 1)
    return jnp.where(r == HEAD_DIM + c % HEAD_DIM, 1.0, 0.0)


def _nsa_prep_kernel(q_ref, kv_ref, qg_ref, kg_ref, qs_out, newk_out, newv_out, win_out, newkb_out, winb_out,
                     vslc_out, vwin_out):
    q = q_ref[...]
    qn = q * lax.rsqrt(_dot_hi(q * q, _head_sum_matrix(1.0 / HEAD_DIM)) + NORM_EPS) * qg_ref[...]
    qs_out[...] = (qn * HEAD_DIM ** -0.5).astype(BF16)
    kv = kv_ref[...]
    w = kv.shape[1]
    same = _iota((w, w), 0) // HEAD_DIM == _iota((w, w), 1) // HEAD_DIM
    ms = _dot_hi(kv * kv, jnp.where(same, 1.0 / HEAD_DIM, 0.0))
    grp = _iota((1, w), 1) // HEAD_DIM
    normed = jnp.where((grp == 1) | (grp == 4), kv * lax.rsqrt(ms + NORM_EPS) * kg_ref[...], kv)
    newk = normed[:, 0:LANES]
    newv = normed[:, LANES:2 * LANES]
    win = normed[:, 2 * LANES:3 * LANES]
    newk_out[...] = newk
    newv_out[...] = newv
    win_out[...] = win
    newkb_out[...] = newk.astype(BF16)
    winb_out[...] = win.astype(BF16)
    rep = _rep_right_matrix().astype(BF16)
    vslc_out[...] = _dot(newv.astype(BF16), rep).astype(BF16)
    vwin_out[...] = _dot(win.astype(BF16), rep).astype(BF16)


def _nsa_prep(u, q_gain, k_gain, tm):
    n = u.shape[0]
    out = lambda w: pl.BlockSpec((tm, w), lambda i: (i, 0))
    shp = lambda w, dt: jax.ShapeDtypeStruct((n, w), dt)
    return pl.pallas_call(
        _nsa_prep_kernel,
        grid=(n // tm,),
        in_specs=[pl.BlockSpec((tm, BRANCH_WIDTH), lambda i: (i, SEG_NSA_Q // BRANCH_WIDTH)),
                  pl.BlockSpec((tm, 3 * LANES), lambda i: (i, SEG_NSA_KV // (3 * LANES))),
                  _const_spec(q_gain.shape), _const_spec(k_gain.shape)],
        out_specs=[out(BRANCH_WIDTH), out(LANES), out(LANES), out(LANES), out(LANES), out(LANES),
                   out(BRANCH_WIDTH), out(BRANCH_WIDTH)],
        out_shape=[shp(BRANCH_WIDTH, BF16), shp(LANES, F32), shp(LANES, F32), shp(LANES, F32), shp(LANES, BF16),
                   shp(LANES, BF16), shp(BRANCH_WIDTH, BF16), shp(BRANCH_WIDTH, BF16)],
        compiler_params=_cparams("parallel"),
        name="nsa_prep",
    )(u, u, q_gain, k_gain)


def _nsa_compress(k_ref, v_ref, pek_ref, pev_ref, wk_ref, wv_ref, kn_ref, nb):
    def body(m, carry):
        ak, av = carry
        xk = k_ref[pl.ds(m, nb, stride=NSA_BLOCK), :] + pek_ref[pl.ds(m, 1), :]
        xv = v_ref[pl.ds(m, nb, stride=NSA_BLOCK), :] + pev_ref[pl.ds(m, 1), :]
        return ak + _dot(xk.astype(BF16), wk_ref[m]), av + _dot(xv.astype(BF16), wv_ref[m])

    ak, av = lax.fori_loop(0, NSA_BLOCK, body,
                           (jnp.zeros((nb, LANES), F32), jnp.zeros((nb, BRANCH_WIDTH), F32)))
    ms = jnp.sum(ak * ak, axis=1, keepdims=True) * (1.0 / HEAD_DIM)
    kc = ak * lax.rsqrt(ms + NORM_EPS) * kn_ref[...]
    return kc.astype(BF16), av.astype(BF16)


def _nsa_compress_kernel(k_ref, v_ref, pek_ref, pev_ref, wk_ref, wv_ref, kn_ref, kc_out, vc_out):
    kc, vc = _nsa_compress(k_ref, v_ref, pek_ref, pev_ref, wk_ref, wv_ref, kn_ref, kc_out.shape[0])
    kc_out[...] = kc
    vc_out[...] = vc


def _nsa_compress_prompt(newk, newv, p):
    b, t, _ = newk.shape
    nb = t // NSA_BLOCK
    per_b = lambda rows, w: pl.BlockSpec((None, rows, w), lambda bi: (bi, 0, 0))
    consts = [p["pe_k"], p["pe_v"], p["w_k"], p["w_v"], p["k_gain_pad"]]
    return pl.pallas_call(
        _nsa_compress_kernel,
        grid=(b,),
        in_specs=[per_b(t, LANES), per_b(t, LANES)] + [_const_spec(a.shape) for a in consts],
        out_specs=[per_b(nb, LANES), per_b(nb, BRANCH_WIDTH)],
        out_shape=[jax.ShapeDtypeStruct((b, nb, LANES), BF16), jax.ShapeDtypeStruct((b, nb, BRANCH_WIDTH), BF16)],
        compiler_params=_cparams("parallel"),
        name="nsa_compress",
    )(newk, newv, *consts)


def _nsa_cmp_branch(ql, kc, vc, qpos, q0, slopes):
    nb = kc.shape[0]
    bend = (_iota((1, nb), 1) + 1) * NSA_BLOCK - 1
    valid = bend <= qpos
    s = jnp.where(valid, _dot_nt(ql, kc) + slopes * (bend - q0).astype(F32), NEG)
    e = jnp.exp(s - jnp.max(s, axis=1, keepdims=True))
    p = jnp.where(valid, e / jnp.sum(e, axis=1, keepdims=True), 0.0)
    return _dot(p.astype(BF16), vc), p


def _nsa_select(imp, qpos_q, nbl):
    blk = _iota((1, nbl), 1)
    blk_f = blk.astype(F32)
    cur = qpos_q // NSA_BLOCK
    forced = (blk == 0) | (blk == cur) | (blk == cur - 1)
    work = jnp.where(blk > cur, NEG, jnp.where(forced, NSA_FORCED, imp))
    sel = jnp.zeros(work.shape, F32)
    for _ in range(min(NSA_TOPK, nbl)):
        mx = jnp.max(work, axis=1, keepdims=True)
        idx = jnp.min(jnp.where(work == mx, blk_f, float(nbl)), axis=1, keepdims=True)
        hit = blk_f == idx
        sel = jnp.where(hit & (mx > 0.5 * NEG), 1.0, sel)
        work = jnp.where(hit, -3e38, work)
    return sel


def _block_expand_matrix(nbl, start, kt):
    return jnp.where(_iota((nbl, kt), 0) == (start + _iota((nbl, kt), 1)) // NSA_BLOCK, 1.0, 0.0).astype(BF16)


def _gate_expand(gl, j):
    r = _iota((LANES, BRANCH_WIDTH), 0)
    c = _iota((LANES, BRANCH_WIDTH), 1)
    return _sigmoid(_dot_hi(gl, jnp.where(r == (c // HEAD_DIM) * 3 + j, 1.0, 0.0)))


def _softmax_rows(s):
    e = jnp.exp(s - jnp.max(s, axis=1, keepdims=True))
    return e / jnp.sum(e, axis=1, keepdims=True)


def _place_queries(qs, rq):
    r = _iota((BRANCH_WIDTH, LANES), 0)
    c = _iota((BRANCH_WIDTH, LANES), 1)
    ql, qr = [], []
    for h in range(N_HEADS):
        ql.append(_dot(qs, jnp.where((r == h * HEAD_DIM + c) & (c < HEAD_DIM), 1.0, 0.0).astype(BF16)))
        qr.append(_dot(qs, jnp.where((r == h * HEAD_DIM + c - HEAD_DIM) & (c >= HEAD_DIM), 1.0, 0.0).astype(BF16)))
    return jnp.concatenate(ql, axis=0).astype(BF16), jnp.concatenate(qr, axis=0).astype(BF16)


def _nsa_prompt_kernel(qs_ref, gate_ref, kc_ref, vc_ref, kb_ref, vs_ref, wb_ref, vw_ref, o_ref,
                       m_ref, l_ref, acc_ref):
    i = pl.program_id(1)
    rq = Q_BLOCK
    rows = N_HEADS * rq
    q0 = i * rq
    nbl = kc_ref.shape[0]
    ql, qr = _place_queries(qs_ref[...], rq)
    slopes = _row_slopes(rq)
    qpos = q0 + _iota((rows, 1), 0) % rq
    qpos_q = q0 + _iota((rq, 1), 0)

    o_c, p_c = _nsa_cmp_branch(ql, kc_ref[...], vc_ref[...], qpos, q0, slopes)
    imp = p_c[0:rq]
    for h in range(1, N_HEADS):
        imp = imp + p_c[h * rq:(h + 1) * rq]
    sel = _nsa_select(imp, qpos_q, nbl).astype(BF16)

    m_ref[...] = jnp.full(m_ref.shape, NEG, F32)
    l_ref[...] = jnp.zeros_like(l_ref)
    acc_ref[...] = jnp.zeros_like(acc_ref)
    n_chunks = (q0 + rq + NSA_KT - 1) // NSA_KT

    def body(j, _):
        start = pl.multiple_of(j * NSA_KT, NSA_KT)
        s = _dot_nt(qr, kb_ref[pl.ds(start, NSA_KT), :])
        v = vs_ref[pl.ds(start, NSA_KT), :]
        kpos = start + _iota((1, NSA_KT), 1)
        ok = (_dot(sel, _block_expand_matrix(nbl, start, NSA_KT)) > 0.5) & (kpos <= qpos_q)
        rel = (kpos - q0).astype(F32)
        for h in range(N_HEADS):
            rs = slice(h * rq, (h + 1) * rq)
            s_h = jnp.where(ok, s[rs] + ALIBI_SLOPES[h] * rel, NEG)
            m_old = m_ref[rs]
            m_new = jnp.maximum(m_old, jnp.max(s_h, axis=1, keepdims=True))
            alpha = jnp.exp(m_old - m_new)
            p = jnp.where(ok, jnp.exp(s_h - m_new), 0.0)
            l_ref[rs] = alpha * l_ref[rs] + jnp.sum(p, axis=1, keepdims=True)
            acc_ref[rs] = alpha * acc_ref[rs] + _dot(p.astype(BF16), v)
            m_ref[rs] = m_new
        return 0

    lax.fori_loop(0, n_chunks, body, 0)
    o_s = acc_ref[...] / l_ref[...]

    span = NSA_WINDOW + rq
    wstart = pl.multiple_of(q0, rq)
    kpos_w = q0 - NSA_WINDOW + _iota((1, span), 1)
    dist = qpos - kpos_w
    valid_w = (dist >= 0) & (dist < NSA_WINDOW) & (kpos_w >= 0)
    s_w = _dot_nt(ql, wb_ref[pl.ds(wstart, span), :]) + slopes * (kpos_w - q0).astype(F32)
    o_w = _dot(_softmax_rows(jnp.where(valid_w, s_w, NEG)).astype(BF16), vw_ref[pl.ds(wstart, span), :])

    gl = gate_ref[...]
    o_ref[...] = (_gate_expand(gl, 0) * _fold_heads(o_c, rq) + _gate_expand(gl, 1) * _fold_heads(o_s, rq)
                  + _gate_expand(gl, 2) * _fold_heads(o_w, rq))


def _nsa_prompt(qs, u3, kc, vc, newk_b, vslc, win_b_pad, vwin_pad):
    b, t, _ = qs.shape
    nb = kc.shape[1]
    rows = N_HEADS * Q_BLOCK
    per_b = lambda r, w: pl.BlockSpec((None, r, w), lambda bi, i: (bi, 0, 0))
    return pl.pallas_call(
        _nsa_prompt_kernel,
        grid=(b, t // Q_BLOCK),
        in_specs=[pl.BlockSpec((None, Q_BLOCK, BRANCH_WIDTH), lambda bi, i: (bi, i, 0)),
                  pl.BlockSpec((None, Q_BLOCK, LANES), lambda bi, i: (bi, i, SEG_NSA_GATE // LANES)),
                  per_b(nb, LANES), per_b(nb, BRANCH_WIDTH), per_b(t, LANES), per_b(t, BRANCH_WIDTH),
                  per_b(t + NSA_WINDOW, LANES), per_b(t + NSA_WINDOW, BRANCH_WIDTH)],
        out_specs=pl.BlockSpec((None, Q_BLOCK, BRANCH_WIDTH), lambda bi, i: (bi, i, 0)),
        out_shape=jax.ShapeDtypeStruct((b, t, BRANCH_WIDTH), F32),
        scratch_shapes=[pltpu.VMEM((rows, 1), F32), pltpu.VMEM((rows, 1), F32), pltpu.VMEM((rows, BRANCH_WIDTH), F32)],
        compiler_params=_cparams("parallel", "parallel"),
        name="nsa_prompt",
    )(qs, u3, kc, vc, newk_b, vslc, win_b_pad, vwin_pad)


NSA_PAGES = 8


def _nsa_sample_kernel(pt_ref, ql_ref, qr_ref, gate_ref, kn_ref, vn_ref, wbuf_ref, wn_ref, vwn_ref,
                       pek_ref, pev_ref, wk_ref, wv_ref, kg_ref, *refs, n_steps, n_new):
    k_pages = refs[:NSA_PAGES]
    v_pages = refs[NSA_PAGES:2 * NSA_PAGES]
    o_ref, kbuf_ref, vbuf_ref = refs[2 * NSA_PAGES:]
    s = pl.program_id(1)
    for r in range(NSA_PAGES):
        row0 = pl.multiple_of((s * NSA_PAGES + r) * PAGE_SIZE, PAGE_SIZE)
        kbuf_ref[pl.ds(row0, PAGE_SIZE), :] = k_pages[r][...]
        vbuf_ref[pl.ds(row0, PAGE_SIZE), :] = v_pages[r][...]

    @pl.when(s == n_steps - 1)
    def _():
        rq = SAMPLE_ROWS
        rows = N_HEADS * rq
        past = kbuf_ref.shape[0]
        nb = past // NSA_BLOCK
        n_buf = wbuf_ref.shape[0]
        ql = ql_ref[...]
        qr = qr_ref[...]
        slopes = _row_slopes(rq)
        step = _iota((rows, 1), 0) % rq
        qpos = past + step
        step_q = _iota((rq, 1), 0)
        rep = _rep_right_matrix()
        tile = lambda a: jnp.concatenate([a] * N_HEADS, axis=0)

        kc, vc = _nsa_compress(kbuf_ref, vbuf_ref, pek_ref, pev_ref, wk_ref, wv_ref, kg_ref, nb)
        o_c, p_c = _nsa_cmp_branch(ql, kc, vc, qpos, past, slopes)
        imp = p_c[0:rq]
        for h in range(1, N_HEADS):
            imp = imp + p_c[h * rq:(h + 1) * rq]
        width = -(-(nb + 1) // LANES) * LANES
        imp = jnp.concatenate([imp, jnp.zeros((rq, width - nb), F32)], axis=1)
        sel = _nsa_select(imp, past + step_q, width)

        col = _iota((1, PAGE_SIZE), 1)
        ok_p = tile(_dot(sel[:, 0:nb].astype(BF16), _block_expand_matrix(nb, 0, past)) > 0.5)
        s_p = _dot_nt(qr, kbuf_ref[...].astype(BF16)) + slopes * (_iota((1, past), 1) - past).astype(F32)
        s_p = jnp.where(ok_p, s_p, NEG)
        ok_n = tile(sel[:, nb:nb + 1] > 0.5) & (col <= step) & (col < n_new)
        s_n = jnp.where(ok_n, _dot_nt(qr, kn_ref[...]) + slopes * col.astype(F32), NEG)
        m = jnp.maximum(jnp.max(s_p, axis=1, keepdims=True), jnp.max(s_n, axis=1, keepdims=True))
        p_p = jnp.where(ok_p, jnp.exp(s_p - m), 0.0)
        p_n = jnp.where(ok_n, jnp.exp(s_n - m), 0.0)
        den = jnp.sum(p_p, axis=1, keepdims=True) + jnp.sum(p_n, axis=1, keepdims=True)
        o_s = (_dot_hi(_dot(p_p.astype(BF16), vbuf_ref[...].astype(BF16)), rep)
               + _dot(p_n.astype(BF16), vn_ref[...])) / den

        wb = wbuf_ref[...].astype(BF16)
        cw = _iota((1, n_buf), 1)
        kpos_w = past - n_buf + cw
        dist = qpos - kpos_w
        ok_w = (dist >= 0) & (dist < NSA_WINDOW) & (kpos_w >= 0)
        s_w = jnp.where(ok_w, _dot_nt(ql, wb) + slopes * (cw - n_buf).astype(F32), NEG)
        ok_wn = (col <= step) & (col < n_new)
        s_wn = jnp.where(ok_wn, _dot_nt(ql, wn_ref[...]) + slopes * col.astype(F32), NEG)
        m = jnp.maximum(jnp.max(s_w, axis=1, keepdims=True), jnp.max(s_wn, axis=1, keepdims=True))
        p_w = jnp.where(ok_w, jnp.exp(s_w - m), 0.0)
        p_wn = jnp.where(ok_wn, jnp.exp(s_wn - m), 0.0)
        den = jnp.sum(p_w, axis=1, keepdims=True) + jnp.sum(p_wn, axis=1, keepdims=True)
        o_w = (_dot_hi(_dot(p_w.astype(BF16), wb), rep) + _dot(p_wn.astype(BF16), vwn_ref[...])) / den

        gl = gate_ref[...]
        o_ref[...] = (_gate_expand(gl, 0) * _fold_heads(o_c, rq) + _gate_expand(gl, 1) * _fold_heads(o_s, rq)
                      + _gate_expand(gl, 2) * _fold_heads(o_w, rq))


def _nsa_sample(page_table, ql, qr, gate, kn, vn, wbuf, wn, vwn, p, cache_k, cache_v, layer, n_pool, n_new):
    b, n_pages = page_table.shape
    n_steps = n_pages // NSA_PAGES
    past = n_pages * PAGE_SIZE
    base = layer * n_pool

    def page_spec(r):
        return pl.BlockSpec((None, PAGE_SIZE, LANES), lambda bi, s, pt: (base + pt[bi, s * NSA_PAGES + r], 0, 0))

    per_b = lambda a: pl.BlockSpec((None,) + a.shape[1:], lambda bi, s, pt: (bi, 0, 0))
    const = lambda a: pl.BlockSpec(a.shape, lambda bi, s, pt: (0,) * a.ndim)
    consts = [p["pe_k"], p["pe_v"], p["w_k"], p["w_v"], p["k_gain_pad"]]
    seq_ops = [ql, qr, gate, kn, vn, wbuf, wn, vwn]
    grid_spec = pltpu.PrefetchScalarGridSpec(
        num_scalar_prefetch=1,
        grid=(b, n_steps),
        in_specs=[per_b(a) for a in seq_ops] + [const(a) for a in consts]
                 + [page_spec(r) for r in range(NSA_PAGES)] * 2,
        out_specs=pl.BlockSpec((None, SAMPLE_ROWS, BRANCH_WIDTH), lambda bi, s, pt: (bi, 0, 0)),
        scratch_shapes=[pltpu.VMEM((past, LANES), F32), pltpu.VMEM((past, LANES), F32)],
    )
    return pl.pallas_call(
        functools.partial(_nsa_sample_kernel, n_steps=n_steps, n_new=n_new),
        grid_spec=grid_spec,
        out_shape=jax.ShapeDtypeStruct((b, SAMPLE_ROWS, BRANCH_WIDTH), F32),
        compiler_params=_cparams("parallel", "arbitrary"),
        name="nsa_sample",
    )(page_table, *seq_ops, *consts, *([cache_k] * NSA_PAGES), *([cache_v] * NSA_PAGES))


def _nsa_sample_mix(u3, page_table, wbuf, p, cache_k, cache_v, layer, n_pool):
    b, t, _ = u3.shape
    qs, newk, newv, win, newk_b, win_b, vslc, vwin = _nsa_prep(u3.reshape(b * t, -1), p["q_gain"], p["k_gain"], b * t)
    q4 = qs.reshape(b, t, N_HEADS, HEAD_DIM).transpose(0, 2, 1, 3)
    q4 = jnp.pad(q4, ((0, 0), (0, 0), (0, SAMPLE_ROWS - t), (0, 0))).reshape(b, N_HEADS * SAMPLE_ROWS, HEAD_DIM)
    ql = jnp.concatenate([q4, jnp.zeros_like(q4)], axis=-1)
    qr = jnp.concatenate([jnp.zeros_like(q4), q4], axis=-1)
    gate = jnp.pad(u3[..., SEG_NSA_GATE:SEG_NSA_GATE + LANES], ((0, 0), (0, SAMPLE_ROWS - t), (0, 0)))
    page_rows = lambda a: jnp.pad(a.reshape(b, t, -1), ((0, 0), (0, PAGE_SIZE - t), (0, 0)))
    o = _nsa_sample(page_table, ql, qr, gate, page_rows(newk_b), page_rows(vslc), wbuf, page_rows(win_b),
                    page_rows(vwin), p, cache_k, cache_v, layer, n_pool, t)
    return o[:, :t], newk, newv, win


def _nsa_params(q_norm, k_norm, cmp_pe, cmp_w):
    lane_pad = lambda a: jnp.pad(a, ((0, 0),) * (a.ndim - 1) + ((0, LANES - HEAD_DIM),))
    wk = cmp_w[0].reshape(NSA_BLOCK, HEAD_DIM, HEAD_DIM)
    wv = cmp_w[1].reshape(NSA_BLOCK, HEAD_DIM, HEAD_DIM)
    pad_rows = lambda a: jnp.pad(a, ((0, 0), (0, LANES - HEAD_DIM), (0, 0)))
    return dict(q_gain=jnp.tile(q_norm, N_HEADS)[None, :], k_gain=jnp.tile(k_norm, 6)[None, :],
                k_gain_pad=lane_pad(k_norm[None, :]), pe_k=lane_pad(cmp_pe[0]), pe_v=lane_pad(cmp_pe[1]),
                w_k=pad_rows(lane_pad(wk)).astype(BF16), w_v=pad_rows(jnp.tile(wv, (1, 1, N_HEADS))).astype(BF16))


SSM_HIST = 8
GROUP_LANES = BRANCH_WIDTH // SSM_GROUPS


def _ssd_kernel(xbc_ref, z_ref, dt_ref, dtt_ref, hist0_ref, h0_ref, cw_ref, cb_ref, bias_ref, biast_ref,
                a_ref, at_ref, dskip_ref, ng_ref, y_ref, hout_ref, hist_ref, state_ref, *, n_valid):
    c = pl.program_id(1)
    q = xbc_ref.shape[0]

    @pl.when(c == 0)
    def _():
        hist_ref[0:SSM_HIST, :] = hist0_ref[...]
        state_ref[...] = h0_ref[...]

    hist_ref[SSM_HIST:SSM_HIST + q, :] = xbc_ref[...]
    full = hist_ref[...]
    conv = cw_ref[3:4, :] * full + cb_ref[...]
    for k in range(1, SSM_CONV):
        conv = conv + cw_ref[3 - k:4 - k, :] * pltpu.roll(full, k, axis=0)
    hist_ref[0:SSM_HIST, :] = full[q:q + SSM_HIST, :]
    act = _silu(conv[SSM_HIST:, :])
    xs = act[:, 0:BRANCH_WIDTH]
    xs_b = xs.astype(BF16)
    bm = act[:, BRANCH_WIDTH:BRANCH_WIDTH + SSM_GROUPS * SSM_STATE].astype(BF16)
    cm = act[:, BRANCH_WIDTH + SSM_GROUPS * SSM_STATE:].astype(BF16)

    dt = jnp.where(_iota((q, LANES), 0) < n_valid, _softplus(dt_ref[...] + bias_ref[...]), 0.0)
    dta = dt * -jnp.exp(a_ref[...])
    dtt = jnp.where(_iota((SSM_HIST, q), 1) < n_valid, _softplus(dtt_ref[...] + biast_ref[...]), 0.0)
    dtat = dtt * -jnp.exp(at_ref[...])
    tril = jnp.where(_iota((q, q), 0) >= _iota((q, q), 1), 1.0, 0.0)
    triu = jnp.where(_iota((q, q), 0) <= _iota((q, q), 1), 1.0, 0.0)
    expand = jnp.where(_iota((LANES, BRANCH_WIDTH), 0) == _iota((LANES, BRANCH_WIDTH), 1) // HEAD_DIM, 1.0, 0.0)
    dt_e = _dot_hi(dt, expand)
    cum_e = _dot_hi(tril, _dot_hi(dta, expand))
    cum_t = _dot_hi(dtat, triu)
    causal = _iota((q, q), 0) >= _iota((q, q), 1)
    lane_head = _iota((1, BRANCH_WIDTH), 1) // HEAD_DIM

    y = dskip_ref[...] * xs
    for g in range(SSM_GROUPS):
        sl = slice(g * SSM_STATE, (g + 1) * SSM_STATE)
        gram = _dot_nt(cm[:, sl], bm[:, sl])
        for h in range(g * (N_HEADS // SSM_GROUPS), (g + 1) * (N_HEADS // SSM_GROUPS)):
            pick = jnp.where(_iota((LANES, q), 0) == h, 1.0, 0.0)
            cum_l = _dot_hi(tril, _dot_hi(dta, pick))
            seg = cum_l - cum_t[h:h + 1, :]
            decay = jnp.where(causal, jnp.exp(jnp.minimum(seg, 0.0)), 0.0)
            scores = (gram * decay * dtt[h:h + 1, :]).astype(BF16)
            y = y + jnp.where(lane_head == h, _dot(scores, xs_b), 0.0)
    state = state_ref[...]
    inter = jnp.concatenate(
        [_dot(cm[:, g * SSM_STATE:(g + 1) * SSM_STATE],
              state[:, g * GROUP_LANES:(g + 1) * GROUP_LANES].astype(BF16)) for g in range(SSM_GROUPS)], axis=1)
    y = y + jnp.exp(cum_e) * inter

    cum_last = cum_e[q - 1:q, :]
    xw = (xs * jnp.exp(cum_last - cum_e) * dt_e).astype(BF16)
    contrib = jnp.concatenate(
        [lax.dot_general(bm[:, g * SSM_STATE:(g + 1) * SSM_STATE], xw[:, g * GROUP_LANES:(g + 1) * GROUP_LANES],
                         (((0,), (0,)), ((), ())), preferred_element_type=F32) for g in range(SSM_GROUPS)], axis=1)
    state_ref[...] = state * jnp.exp(cum_last) + contrib

    y = y * _silu(z_ref[...])
    parts = []
    for g in range(SSM_GROUPS):
        yg = y[:, g * GROUP_LANES:(g + 1) * GROUP_LANES]
        parts.append(yg * lax.rsqrt(jnp.mean(yg * yg, axis=-1, keepdims=True) + NORM_EPS))
    y_ref[...] = jnp.concatenate(parts, axis=1) * ng_ref[...]

    @pl.when(c == pl.num_programs(1) - 1)
    def _():
        hout_ref[...] = state_ref[...]


def _ssd(u3, dtt, hist0, h0, p, *, q, n_valid):
    b, t, _ = u3.shape
    per_b = lambda shape: pl.BlockSpec((None,) + shape, lambda bi, c: (bi, 0, 0))
    col = lambda width, seg: pl.BlockSpec((None, q, width), lambda bi, c: (bi, c, seg // width))
    consts = [p["conv_w"], p["conv_b"], p["dt_bias"], p["dt_bias_t"], p["a"], p["a_t"], p["d_skip"], p["norm_g"]]
    return pl.pallas_call(
        functools.partial(_ssd_kernel, n_valid=n_valid),
        grid=(b, t // q),
        in_specs=[col(SSM_CONV_DIM, SEG_SSM_XBC), col(BRANCH_WIDTH, SEG_SSM_Z), col(LANES, SEG_SSM_DT),
                  pl.BlockSpec((None, SSM_HIST, q), lambda bi, c: (bi, 0, c)),
                  per_b((SSM_HIST, SSM_CONV_DIM)), per_b((SSM_STATE, BRANCH_WIDTH))]
                 + [_const_spec(a.shape) for a in consts],
        out_specs=[pl.BlockSpec((None, q, BRANCH_WIDTH), lambda bi, c: (bi, c, 0)),
                   per_b((SSM_STATE, BRANCH_WIDTH))],
        out_shape=[jax.ShapeDtypeStruct((b, t, BRANCH_WIDTH), F32),
                   jax.ShapeDtypeStruct((b, SSM_STATE, BRANCH_WIDTH), F32)],
        scratch_shapes=[pltpu.VMEM((SSM_HIST + q, SSM_CONV_DIM), F32), pltpu.VMEM((SSM_STATE, BRANCH_WIDTH), F32)],
        compiler_params=_cparams("parallel", "arbitrary"),
        name="ssd",
    )(u3, u3, u3, dtt, hist0, h0, *consts)


def _head_sum_matrix(scale):
    same = _iota((BRANCH_WIDTH, BRANCH_WIDTH), 0) // HEAD_DIM == _iota((BRANCH_WIDTH, BRANCH_WIDTH), 1) // HEAD_DIM
    return jnp.where(same, scale, 0.0)


def _rwkv_prep_kernel(u_ref, up_ref, p1_ref, mu_ref, w0_ref, w2_ref, a0_ref, a2_ref, g2_ref, kk_ref, ka_ref, rk_ref,
                      r_out, k_out, v_out, d_out, nkk_out, kka_out, g_out, bonus_out, *, seq_tiles, period):
    tm = u_ref.shape[0]
    u = u_ref[...]
    rolled = pltpu.roll(u, 1, axis=0)
    if seq_tiles > 0:
        first = (pl.program_id(0) % seq_tiles) == 0
        carry_in = up_ref[FFN_HALO - 1:FFN_HALO, :] * jnp.where(first, 0.0, 1.0)
        prev = jnp.where(_iota((tm, 1), 0) == 0, carry_in, rolled)
    else:
        prev = jnp.where(_iota((tm, 1), 0) % period >= 1, rolled, p1_ref[...])
    us = u + (prev - u) * mu_ref[...]
    r = us[:, 0:BRANCH_WIDTH]
    k = us[:, BRANCH_WIDTH:2 * BRANCH_WIDTH]
    v = us[:, 2 * BRANCH_WIDTH:3 * BRANCH_WIDTH]
    wa = us[:, 3 * BRANCH_WIDTH:3 * BRANCH_WIDTH + LANES]
    gd = us[:, 3 * BRANCH_WIDTH + LANES:]
    is_w = _iota((1, LANES), 1) < RWKV_W_LORA
    w_lora = _dot(jnp.where(is_w, jnp.tanh(wa), 0.0).astype(BF16), w2_ref[...])
    a_lora = _dot(jnp.where(is_w, 0.0, wa).astype(BF16), a2_ref[...])
    w_raw = -_softplus(-(w0_ref[...] + w_lora)) - 0.5
    decay = jnp.exp(-jnp.exp(w_raw))
    a = _sigmoid(a0_ref[...] + a_lora)
    g = _dot(_sigmoid(gd).astype(BF16), g2_ref[...])
    head_sum = _head_sum_matrix(1.0)
    kk = k * kk_ref[...]
    kk = kk / jnp.maximum(jnp.sqrt(_dot_hi(kk * kk, head_sum)), 1e-12)
    k_mod = k * (1.0 + (a - 1.0) * ka_ref[...])
    r_out[...] = r
    k_out[...] = k_mod
    v_out[...] = v
    d_out[...] = decay
    nkk_out[...] = -kk
    kka_out[...] = kk * a
    g_out[...] = g
    bonus_out[...] = _dot_hi(r * k_mod * rk_ref[...], head_sum) * v


def _rwkv_prep(u, p1, p, *, tm, seq_tiles, period):
    n = u.shape[0]
    hb = tm // FFN_HALO
    seg = SEG_RWKV // RWKV_IN
    if seq_tiles > 0:
        up_spec = pl.BlockSpec((FFN_HALO, RWKV_IN), lambda i: (jnp.maximum(i * hb - 1, 0), seg))
        p_spec = _const_spec(p1.shape)
    else:
        up_spec = pl.BlockSpec((FFN_HALO, RWKV_IN), lambda i: (0, seg))
        p_spec = pl.BlockSpec((tm, RWKV_IN), lambda i: (i, 0))
    consts = [p["mu"], p["w0"], p["w2"], p["a0"], p["a2"], p["g2"], p["k_k"], p["k_a"], p["r_k"]]
    out = pl.BlockSpec((tm, BRANCH_WIDTH), lambda i: (i, 0))
    return pl.pallas_call(
        functools.partial(_rwkv_prep_kernel, seq_tiles=seq_tiles, period=period),
        grid=(n // tm,),
        in_specs=[pl.BlockSpec((tm, RWKV_IN), lambda i: (i, seg)), up_spec, p_spec]
                 + [_const_spec(a.shape) for a in consts],
        out_specs=[out] * 8,
        out_shape=[jax.ShapeDtypeStruct((n, BRANCH_WIDTH), F32)] * 8,
        compiler_params=_cparams("parallel"),
        name="rwkv_prep",
    )(u, u, p1, *consts)


RWKV_CHAINS = 8
RWKV_TB = 128


def _rwkv_scan_kernel(r_ref, k_ref, d_ref, nkk_ref, kka_ref, vt_ref, s0_ref, yt_ref, sout_ref, s_ref, *, n_steps):
    @pl.when(pl.program_id(1) == 0)
    def _():
        s_ref[...] = s0_ref[...]

    yt_ref[...] = jnp.zeros_like(yt_ref)
    lane = _iota((HEAD_DIM, RWKV_TB), 1)

    def step(t, _):
        for c in range(RWKV_CHAINS):
            row = lambda ref: ref[c, pl.ds(t, 1), :]
            s = s_ref[c]
            sa = jnp.sum(s * row(nkk_ref), axis=1, keepdims=True)
            v_col = jnp.sum(jnp.where(lane == t, vt_ref[c], 0.0), axis=1, keepdims=True)
            s = s * row(d_ref) + sa * row(kka_ref) + v_col * row(k_ref)
            s_ref[c] = s
            y_col = jnp.sum(s * row(r_ref), axis=1, keepdims=True)
            yt_ref[c] = jnp.where(lane == t, y_col, yt_ref[c])
        return 0

    lax.fori_loop(0, n_steps, step, 0)

    @pl.when(pl.program_id(1) == pl.num_programs(1) - 1)
    def _():
        sout_ref[...] = s_ref[...]


def _rwkv_scan(r, k, d, nkk, kka, vt, s0, *, rows, n_steps):
    chains, t_rows, _ = r.shape
    t_lanes = vt.shape[2]
    row_spec = pl.BlockSpec((RWKV_CHAINS, rows, HEAD_DIM), lambda ci, tb: (ci, tb, 0))
    lane_spec = pl.BlockSpec((RWKV_CHAINS, HEAD_DIM, RWKV_TB), lambda ci, tb: (ci, 0, tb))
    state_spec = pl.BlockSpec((RWKV_CHAINS, HEAD_DIM, HEAD_DIM), lambda ci, tb: (ci, 0, 0))
    return pl.pallas_call(
        functools.partial(_rwkv_scan_kernel, n_steps=n_steps),
        grid=(chains // RWKV_CHAINS, t_lanes // RWKV_TB),
        in_specs=[row_spec] * 5 + [lane_spec, state_spec],
        out_specs=[lane_spec, state_spec],
        out_shape=[jax.ShapeDtypeStruct((chains, HEAD_DIM, t_lanes), F32),
                   jax.ShapeDtypeStruct((chains, HEAD_DIM, HEAD_DIM), F32)],
        scratch_shapes=[pltpu.VMEM((RWKV_CHAINS, HEAD_DIM, HEAD_DIM), F32)],
        compiler_params=_cparams("parallel", "arbitrary"),
        name="rwkv_scan",
    )(r, k, d, nkk, kka, vt, s0)


def _rwkv_post_kernel(y_ref, bonus_ref, g_ref, lnw_ref, lnb_ref, o_ref):
    y = y_ref[...]
    head_mean = _head_sum_matrix(1.0 / HEAD_DIM)
    cen = y - _dot_hi(y, head_mean)
    var = _dot_hi(cen * cen, head_mean)
    yn = cen * lax.rsqrt(var + RWKV_LN_EPS) * lnw_ref[...] + lnb_ref[...]
    o_ref[...] = (yn + bonus_ref[...]) * g_ref[...]


def _rwkv_post(y, bonus, g, ln_w, ln_b, tm):
    n = y.shape[0]
    blk = pl.BlockSpec((tm, BRANCH_WIDTH), lambda i: (i, 0))
    return pl.pallas_call(
        _rwkv_post_kernel,
        grid=(n // tm,),
        in_specs=[blk, blk, blk, _const_spec(ln_w.shape), _const_spec(ln_b.shape)],
        out_specs=blk,
        out_shape=jax.ShapeDtypeStruct((n, BRANCH_WIDTH), F32),
        compiler_params=_cparams("parallel"),
        name="rwkv_post",
    )(y, bonus, g, ln_w, ln_b)


def _rwkv_mix(u2, p1, s0, p, ln_w, ln_b, *, b, t, tm, seq_tiles, period):
    r, k, v, d, nkk, kka, g, bonus = _rwkv_prep(u2, p1, p, tm=tm, seq_tiles=seq_tiles, period=period)
    t_rows = max(t, SAMPLE_ROWS)
    t_lanes = -(-t // RWKV_TB) * RWKV_TB

    def chains(a):
        a = a.reshape(b, t, N_HEADS, HEAD_DIM).transpose(0, 2, 1, 3).reshape(b * N_HEADS, t, HEAD_DIM)
        return a

    rows = [jnp.pad(chains(a), ((0, 0), (0, t_rows - t), (0, 0))) for a in (r, k, d, nkk, kka)]
    vt = jnp.pad(chains(v).transpose(0, 2, 1), ((0, 0), (0, 0), (0, t_lanes - t)))
    yt, s_fin = _rwkv_scan(*rows, vt, s0, rows=min(t_rows, RWKV_TB), n_steps=min(t, RWKV_TB))
    y = yt[:, :, :t].transpose(0, 2, 1).reshape(b, N_HEADS, t, HEAD_DIM).transpose(0, 2, 1, 3)
    o = _rwkv_post(y.reshape(b * t, BRANCH_WIDTH), bonus, g, ln_w, ln_b, tm)
    return o, s_fin


def _rwkv_params(mu, w0, w2, a0, a2, g2, k_k, k_a, r_k):
    row = lambda v: v.reshape(1, -1)
    zeros = jnp.zeros((RWKV_W_LORA, BRANCH_WIDTH), F32)
    return dict(mu=row(mu), w0=row(w0), w2=jnp.concatenate([w2, zeros]).astype(BF16), a0=row(a0),
                a2=jnp.concatenate([zeros, a2]).astype(BF16), g2=g2.astype(BF16), k_k=row(k_k), k_a=row(k_a),
                r_k=row(r_k))


def _ssd_params(conv_w, conv_b, dt_bias, a_log, d_skip, norm_g):
    a = a_log.astype(F32)
    pad_row = lambda v: jnp.pad(v, (0, LANES - N_HEADS))[None, :]
    pad_col = lambda v: jnp.pad(v, (0, SSM_HIST - N_HEADS))[:, None]
    return dict(conv_w=conv_w, conv_b=conv_b[None, :], dt_bias=pad_row(dt_bias), dt_bias_t=pad_col(dt_bias),
                a=pad_row(a), a_t=pad_col(a), d_skip=jnp.repeat(d_skip, HEAD_DIM)[None, :], norm_g=norm_g[None, :])


PROMPT_TM = 512
INPROJ_TM = 1024


def _sb_sample_mix(u3, page_table, cache_k, cache_v, layer, n_pool):
    b, t, _ = u3.shape
    q = u3[..., SEG_SB:SEG_SB + BRANCH_WIDTH]
    k = u3[..., SEG_SB + BRANCH_WIDTH:SEG_SB + 2 * BRANCH_WIDTH]
    v = u3[..., SEG_SB + 2 * BRANCH_WIDTH:SEG_SB + 3 * BRANCH_WIDTH]
    qt = jnp.tile(jnp.pad(q, ((0, 0), (0, SAMPLE_ROWS - t), (0, 0))), (1, N_HEADS, 1))
    rows = N_HEADS * SAMPLE_ROWS
    own = (np.arange(rows)[:, None] // SAMPLE_ROWS) == (np.arange(BRANCH_WIDTH)[None, :] // HEAD_DIM)
    qbd = jnp.where(own[None], qt, 0.0)
    page_rows = lambda a: jnp.pad(a, ((0, 0), (0, PAGE_SIZE - t), (0, 0)))
    o = _sb_sample(page_table, qbd, page_rows(k), page_rows(v), cache_k, cache_v, layer, n_pool)
    return o[:, :t]


def kernel(x_prompt, x_sample, cache_nsa_k, cache_nsa_v, cache_sb_k, cache_sb_v, state_win_kv, state_ssm_conv, state_ssm, state_rwkv_shift, state_rwkv, state_ffn_conv, page_table, norm1, norm2, w_in, nsa_q_norm, nsa_k_norm, nsa_cmp_pe, nsa_cmp_w, ssm_conv_w, ssm_conv_b, ssm_dt_bias, ssm_a_log, ssm_d, ssm_norm, rwkv_mu, rwkv_w0, rwkv_w2, rwkv_a0, rwkv_a2, rwkv_g2, rwkv_k_k, rwkv_k_a, rwkv_r_k, rwkv_ln_w, rwkv_ln_b, w_branch, w_out, ffn_up, ffn_conv_w, ffn_conv_b, ffn_down):
    bp, tp, _ = x_prompt.shape
    bs, ts, _ = x_sample.shape
    depth, n_pool = cache_nsa_k.shape[:2]
    n_pages = page_table.shape[1]
    past = n_pages * PAGE_SIZE
    n_buf = state_win_kv.shape[2]
    n_p, n_s = bp * tp, bs * ts
    assert tp % INPROJ_TM == 0 and tp >= NSA_WINDOW and past % NSA_BLOCK == 0 and ts < SAMPLE_ROWS
    assert n_pages % SB_PAGES == 0 and n_pages % NSA_PAGES == 0 and n_s % FFN_HALO == 0

    perm = _in_perm()
    w_in_p = jnp.where(perm >= 0, jnp.take(w_in, np.maximum(perm, 0), axis=2), 0.0).astype(BF16)
    nsa_ck = cache_nsa_k.reshape(depth * n_pool, PAGE_SIZE, LANES)
    nsa_cv = cache_nsa_v.reshape(depth * n_pool, PAGE_SIZE, LANES)
    sb_ck = cache_sb_k.reshape(depth * n_pool, PAGE_SIZE, BRANCH_WIDTH)
    sb_cv = cache_sb_v.reshape(depth * n_pool, PAGE_SIZE, BRANCH_WIDTH)
    dummy = jnp.zeros((FFN_HALO, LANES), F32)
    seg = lambda u, start, width: u[..., start:start + width]

    xp = x_prompt.reshape(n_p, D_MODEL)
    xs = x_sample.reshape(n_s, D_MODEL)
    outs = [[] for _ in range(20)]
    for l in range(depth):
        nsa_p = _nsa_params(nsa_q_norm[l], nsa_k_norm[l], nsa_cmp_pe[l], nsa_cmp_w[l])
        ssd_p = _ssd_params(ssm_conv_w[l], ssm_conv_b[l], ssm_dt_bias[l], ssm_a_log[l], ssm_d[l], ssm_norm[l])
        rwkv_p = _rwkv_params(rwkv_mu[l], rwkv_w0[l], rwkv_w2[l], rwkv_a0[l], rwkv_a2[l], rwkv_g2[l],
                              rwkv_k_k[l], rwkv_k_a[l], rwkv_r_k[l].reshape(-1))
        ln_w, ln_b = rwkv_ln_w[l][None, :], rwkv_ln_b[l][None, :]
        wb, wo = w_branch[l].astype(BF16), w_out[l].astype(BF16)
        wup, wdn = ffn_up[l].astype(BF16), ffn_down[l].astype(BF16)

        up = _inproj(xp, norm1[l][None, :], w_in_p[l], INPROJ_TM)
        u3 = up.reshape(bp, tp, D_IN_PAD)
        r3 = lambda a: a.reshape(bp, tp, -1)
        qs, newk, newv, win, newk_b, win_b, vslc, vwin = _nsa_prep(up, nsa_p["q_gain"], nsa_p["k_gain"], PROMPT_TM)
        kc, vc = _nsa_compress_prompt(r3(newk), r3(newv), nsa_p)
        front = lambda a: jnp.pad(r3(a), ((0, 0), (NSA_WINDOW, 0), (0, 0)))
        o_a = _nsa_prompt(r3(qs), u3, kc, vc, r3(newk_b), r3(vslc), front(win_b), front(vwin))
        dtt = jnp.pad(jnp.swapaxes(seg(u3, SEG_SSM_DT, N_HEADS), 1, 2), ((0, 0), (0, SSM_HIST - N_HEADS), (0, 0)))
        o_b, ssm_fin = _ssd(u3, dtt, jnp.zeros((bp, SSM_HIST, SSM_CONV_DIM), F32),
                            jnp.zeros((bp, SSM_STATE, BRANCH_WIDTH), F32), ssd_p, q=SSM_CHUNK, n_valid=SSM_CHUNK)
        o_c, rwkv_fin = _rwkv_mix(up, dummy, jnp.zeros((bp * N_HEADS, HEAD_DIM, HEAD_DIM), F32), rwkv_p, ln_w, ln_b,
                                  b=bp, t=tp, tm=PROMPT_TM, seq_tiles=tp // PROMPT_TM, period=0)
        sb_k = seg(u3, SEG_SB + BRANCH_WIDTH, BRANCH_WIDTH)
        sb_v = seg(u3, SEG_SB + 2 * BRANCH_WIDTH, BRANCH_WIDTH)
        o_d = _sb_prompt(u3, sb_k.astype(BF16), sb_v.astype(BF16))
        flat = lambda a: a.reshape(n_p, BRANCH_WIDTH)
        xp = _merge(xp, [flat(o_a), flat(o_b), o_c, flat(o_d)], up, wb, wo, PROMPT_TM)
        xp, hs = _ffn(xp, dummy, dummy, norm2[l][None, :], wup, ffn_conv_w[l], ffn_conv_b[l][None, :], wdn,
                      tm=PROMPT_TM, seq_tiles=tp // PROMPT_TM, period=0, keep=FFN_HALO)
        prompt_states = (
            r3(newk).reshape(bp, tp, 2, HEAD_DIM), r3(newv).reshape(bp, tp, 2, HEAD_DIM),
            sb_k.reshape(bp, tp, N_HEADS, HEAD_DIM), sb_v.reshape(bp, tp, N_HEADS, HEAD_DIM),
            r3(win)[:, tp - min(NSA_WINDOW, tp):].reshape(bp, -1, 2, HEAD_DIM),
            seg(u3, SEG_SSM_XBC, SSM_CONV_DIM)[:, tp - (SSM_CONV - 1):],
            ssm_fin.reshape(bp, SSM_STATE, N_HEADS, HEAD_DIM).transpose(0, 2, 3, 1),
            seg(u3, SEG_RWKV, RWKV_IN)[:, tp - 1:],
            rwkv_fin.reshape(bp, N_HEADS, HEAD_DIM, HEAD_DIM),
            hs.reshape(bp, tp // PROMPT_TM, FFN_HALO, 2 * D_FF)[:, -1, FFN_HALO - (FFN_CONV - 1):])

        us = _inproj(xs, norm1[l][None, :], w_in_p[l], n_s)
        u3 = us.reshape(bs, ts, D_IN_PAD)
        o_a, newk, newv, win = _nsa_sample_mix(u3, page_table, state_win_kv[l].reshape(bs, n_buf, LANES), nsa_p,
                                               nsa_ck, nsa_cv, l, n_pool)
        t_pad = 2 * SAMPLE_ROWS
        u3_pad = jnp.pad(u3, ((0, 0), (0, t_pad - ts), (0, 0)))
        dtt = jnp.pad(jnp.swapaxes(seg(u3_pad, SEG_SSM_DT, N_HEADS), 1, 2), ((0, 0), (0, SSM_HIST - N_HEADS), (0, 0)))
        hist0 = jnp.pad(state_ssm_conv[l], ((0, 0), (SSM_HIST - (SSM_CONV - 1), 0), (0, 0)))
        h0 = state_ssm[l].transpose(0, 3, 1, 2).reshape(bs, SSM_STATE, BRANCH_WIDTH)
        o_b, ssm_fin = _ssd(u3_pad, dtt, hist0, h0, ssd_p, q=t_pad, n_valid=ts)
        shift_rows = jnp.pad(state_rwkv_shift[l], ((0, 0), (0, ts - 1), (0, 0))).reshape(n_s, RWKV_IN)
        o_c, rwkv_fin = _rwkv_mix(us, shift_rows, state_rwkv[l].reshape(bs * N_HEADS, HEAD_DIM, HEAD_DIM), rwkv_p,
                                  ln_w, ln_b, b=bs, t=ts, tm=n_s, seq_tiles=0, period=ts)
        o_d = _sb_sample_mix(u3, page_table, sb_ck, sb_cv, l, n_pool)
        flat = lambda a: a.reshape(n_s, BRANCH_WIDTH)
        xs = _merge(xs, [flat(o_a), flat(o_b[:, :ts]), o_c, flat(o_d)], us, wb, wo, n_s)
        conv_state = state_ffn_conv[l]
        prev1 = jnp.pad(conv_state[:, 1:2], ((0, 0), (0, ts - 1), (0, 0))).reshape(n_s, 2 * D_FF)
        prev2 = jnp.pad(conv_state, ((0, 0), (0, ts - 2), (0, 0))).reshape(n_s, 2 * D_FF)
        xs, hs = _ffn(xs, prev1, prev2, norm2[l][None, :], wup, ffn_conv_w[l], ffn_conv_b[l][None, :], wdn,
                      tm=n_s, seq_tiles=0, period=ts, keep=n_s)
        tail = lambda old, new, n: jnp.concatenate([old, new], axis=1)[:, -n:]
        sample_states = (
            newk.reshape(bs, ts, 2, HEAD_DIM), newv.reshape(bs, ts, 2, HEAD_DIM),
            seg(u3, SEG_SB + BRANCH_WIDTH, BRANCH_WIDTH).reshape(bs, ts, N_HEADS, HEAD_DIM),
            seg(u3, SEG_SB + 2 * BRANCH_WIDTH, BRANCH_WIDTH).reshape(bs, ts, N_HEADS, HEAD_DIM),
            tail(state_win_kv[l], win.reshape(bs, ts, 2, HEAD_DIM), n_buf),
            tail(state_ssm_conv[l], seg(u3, SEG_SSM_XBC, SSM_CONV_DIM), SSM_CONV - 1),
            ssm_fin.reshape(bs, SSM_STATE, N_HEADS, HEAD_DIM).transpose(0, 2, 3, 1),
            seg(u3, SEG_RWKV, RWKV_IN)[:, ts - 1:],
            rwkv_fin.reshape(bs, N_HEADS, HEAD_DIM, HEAD_DIM),
            tail(conv_state, hs.reshape(bs, ts, 2 * D_FF), FFN_CONV - 1))
        for j in range(10):
            outs[2 * j].append(prompt_states[j])
            outs[2 * j + 1].append(sample_states[j])

    return (xp.reshape(bp, tp, D_MODEL), xs.reshape(bs, ts, D_MODEL)) + tuple(jnp.stack(o) for o in outs)
```

```python
import functools
import math

import numpy as np
import jax
import jax.numpy as jnp
from jax import lax
from jax.experimental import pallas as pl
from jax.experimental.pallas import tpu as pltpu

F32 = jnp.float32
BF16 = jnp.bfloat16
HIGHEST = lax.Precision.HIGHEST

D_MODEL = 1024
N_BRANCH = 4
BRANCH_WIDTH = D_MODEL // N_BRANCH
HEAD_DIM = 64
N_HEADS = BRANCH_WIDTH // HEAD_DIM
Q_BLOCK = 128
PAGE_SIZE = 128
NORM_EPS = 1e-6
NEG = -1e30
NSA_BLOCK = 64
NSA_TOPK = 16
NSA_WINDOW = 512
NSA_FORCED = 2.0 * N_HEADS
SSM_GROUPS = 2
SSM_STATE = 128
SSM_CONV = 4
SSM_CHUNK = 128
SSM_CONV_DIM = BRANCH_WIDTH + 2 * SSM_GROUPS * SSM_STATE
RWKV_W_LORA = 64
RWKV_A_LORA = 64
RWKV_G_LORA = 128
RWKV_IN = 3 * BRANCH_WIDTH + RWKV_W_LORA + RWKV_A_LORA + RWKV_G_LORA
RWKV_LN_EPS = 64e-5
D_FF = 2816
FFN_CONV = 3

LANES = 128
VMEM_LIMIT = 56 * 1024 * 1024

SEG_MERGE = 0
SEG_RWKV = 4096
SEG_NSA_Q = 5120
SEG_SSM_XBC = 5376
SEG_SB = 6144
SEG_NSA_KV = 6912
SEG_NSA_GATE = 7296
SEG_SSM_Z = 7424
SEG_SSM_DT = 7680
D_IN_PAD = 8192


def _in_perm():
    sizes = (BRANCH_WIDTH, 6 * HEAD_DIM, 3 * N_HEADS, BRANCH_WIDTH, SSM_CONV_DIM, N_HEADS, RWKV_IN,
             3 * BRANCH_WIDTH, N_BRANCH * D_MODEL)
    off = np.concatenate([[0], np.cumsum(sizes)])
    o_q, o_kv, o_gate, o_z, o_xbc, o_dt, o_rwkv, o_sb, o_merge = off[:-1]
    perm = -np.ones((D_IN_PAD,), np.int64)
    perm[SEG_MERGE:SEG_MERGE + 4096] = o_merge + np.arange(4096)
    perm[SEG_RWKV:SEG_RWKV + RWKV_IN] = o_rwkv + np.arange(RWKV_IN)
    perm[SEG_NSA_Q:SEG_NSA_Q + 256] = o_q + np.arange(256)
    perm[SEG_SSM_XBC:SEG_SSM_XBC + 768] = o_xbc + np.arange(768)
    perm[SEG_SB:SEG_SB + 768] = o_sb + np.arange(768)
    kv_order = (0, 2, 1, 3, 4, 5)
    for j, src in enumerate(kv_order):
        perm[SEG_NSA_KV + 64 * j:SEG_NSA_KV + 64 * (j + 1)] = o_kv + 64 * src + np.arange(64)
    perm[SEG_NSA_GATE:SEG_NSA_GATE + 12] = o_gate + np.arange(12)
    perm[SEG_SSM_Z:SEG_SSM_Z + 256] = o_z + np.arange(256)
    perm[SEG_SSM_DT:SEG_SSM_DT + 4] = o_dt + np.arange(4)
    return perm


def _cparams(*sem):
    return pltpu.CompilerParams(dimension_semantics=tuple(sem), vmem_limit_bytes=VMEM_LIMIT)


def _const_spec(shape):
    nd = len(shape)
    return pl.BlockSpec(shape, lambda *_: (0,) * nd)


def _iota(shape, dim):
    return lax.broadcasted_iota(jnp.int32, shape, dim)


def _dot(a, b):
    return jnp.dot(a, b, preferred_element_type=F32)


def _dot_hi(a, b):
    return jnp.dot(a, b, preferred_element_type=F32, precision=HIGHEST)


def _dot_nt(a, b):
    return lax.dot_general(a, b, (((1,), (1,)), ((), ())), preferred_element_type=F32)


def _sigmoid(x):
    return 1.0 / (1.0 + jnp.exp(-x))


def _silu(x):
    return x * _sigmoid(x)


def _softplus(x):
    return jnp.maximum(x, 0.0) + jnp.log(1.0 + jnp.exp(-jnp.abs(x)))


def _inproj_kernel(x_ref, g_ref, w_ref, o_ref, xn_ref):
    @pl.when(pl.program_id(1) == 0)
    def _():
        x = x_ref[...]
        ms = jnp.mean(x * x, axis=-1, keepdims=True)
        xn_ref[...] = (x * lax.rsqrt(ms + NORM_EPS) * g_ref[...]).astype(BF16)

    o_ref[...] = _dot(xn_ref[...], w_ref[...])


def _inproj(x, g, w, tm, tn=1024):
    n = x.shape[0]
    return pl.pallas_call(
        _inproj_kernel,
        grid=(n // tm, D_IN_PAD // tn),
        in_specs=[pl.BlockSpec((tm, D_MODEL), lambda i, j: (i, 0)),
                  pl.BlockSpec((1, D_MODEL), lambda i, j: (0, 0)),
                  pl.BlockSpec((D_MODEL, tn), lambda i, j: (0, j))],
        out_specs=pl.BlockSpec((tm, tn), lambda i, j: (i, j)),
        out_shape=jax.ShapeDtypeStruct((n, D_IN_PAD), F32),
        scratch_shapes=[pltpu.VMEM((tm, D_MODEL), BF16)],
        compiler_params=_cparams("parallel", "arbitrary"),
        name="inproj",
    )(x, g, w)


def _merge_kernel(x_ref, oa_ref, ob_ref, oc_ref, od_ref, gate_ref, wb_ref, wo_ref, out_ref):
    acc = None
    for n, o_ref in enumerate((oa_ref, ob_ref, oc_ref, od_ref)):
        proj = _dot(o_ref[...].astype(BF16), wb_ref[n])
        term = _sigmoid(gate_ref[:, n * D_MODEL:(n + 1) * D_MODEL]) * proj
        acc = term if acc is None else acc + term
    out_ref[...] = x_ref[...] + _dot(acc.astype(BF16), wo_ref[...])


def _merge(x, branches, u, wb, wo, tm):
    n = x.shape[0]
    row = lambda i: (i, 0)
    bspec = pl.BlockSpec((tm, BRANCH_WIDTH), row)
    return pl.pallas_call(
        _merge_kernel,
        grid=(n // tm,),
        in_specs=[pl.BlockSpec((tm, D_MODEL), row), bspec, bspec, bspec, bspec,
                  pl.BlockSpec((tm, N_BRANCH * D_MODEL), lambda i: (i, SEG_MERGE // (N_BRANCH * D_MODEL))),
                  _const_spec((N_BRANCH, BRANCH_WIDTH, D_MODEL)),
                  _const_spec((D_MODEL, D_MODEL))],
        out_specs=pl.BlockSpec((tm, D_MODEL), row),
        out_shape=jax.ShapeDtypeStruct((n, D_MODEL), F32),
        compiler_params=_cparams("parallel"),
        name="merge",
    )(x, *branches, u, wb, wo)


FFN_COLS = 256
FFN_HALO = 8


def _ffn_kernel(x_ref, xp_ref, p1_ref, p2_ref, g_ref, wup_ref, cw_ref, cb_ref, wd_ref,
                out_ref, hs_ref, xn_ref, acc_ref, *, seq_tiles, period, keep):
    tm = x_ref.shape[0]
    prompt = seq_tiles > 0
    halo = FFN_HALO if prompt else 0

    def norm(x):
        ms = jnp.mean(x * x, axis=-1, keepdims=True)
        return (x * lax.rsqrt(ms + NORM_EPS) * g_ref[...]).astype(BF16)

    x = x_ref[...]
    xn_ref[halo:halo + tm, :] = norm(x)
    if prompt:
        first = (pl.program_id(0) % seq_tiles) == 0
        xn_ref[0:halo, :] = norm(xp_ref[...])
        hist_ok = jnp.where(first, 0.0, 1.0)
    else:
        step = _iota((tm, 1), 0) % period
    acc_ref[...] = x

    for c in range(D_FF // FFN_COLS):
        gs = slice(c * FFN_COLS, (c + 1) * FFN_COLS)
        us = slice(D_FF + c * FFN_COLS, D_FF + (c + 1) * FFN_COLS)
        xn = xn_ref[...]
        conv = []
        for part, cs in enumerate((gs, us)):
            h = _dot(xn, wup_ref[:, cs])
            if prompt:
                rows = _iota((tm + halo, 1), 0)
                h = jnp.where(rows < halo, h * hist_ok, h)
            h1 = pltpu.roll(h, 1, axis=0)
            h2 = pltpu.roll(h, 2, axis=0)
            if not prompt:
                h1 = jnp.where(step >= 1, h1, p1_ref[:, cs])
                h2 = jnp.where(step >= 2, h2, p2_ref[:, cs])
            y = cw_ref[2:3, cs] * h + cw_ref[1:2, cs] * h1 + cw_ref[0:1, cs] * h2 + cb_ref[:, cs]
            conv.append(y[halo:, :])
            hs_ref[0, :, cs] = h[halo + tm - keep:, :]
        act = (_silu(conv[0]) * conv[1]).astype(BF16)
        acc_ref[...] += _dot(act, wd_ref[gs, :])
    out_ref[...] = acc_ref[...]


def _ffn(x, prev1, prev2, g, wup, cw, cb, wd, *, tm, seq_tiles, period, keep):
    n = x.shape[0]
    nt = n // tm
    prompt = seq_tiles > 0
    halo = FFN_HALO if prompt else 0
    hb = tm // FFN_HALO
    if prompt:
        xp_spec = pl.BlockSpec((FFN_HALO, D_MODEL), lambda i: (jnp.maximum(i * hb - 1, 0), 0))
        xp = x
        p_spec = _const_spec(prev1.shape)
    else:
        xp_spec = _const_spec((FFN_HALO, D_MODEL))
        xp = x
        p_spec = pl.BlockSpec((tm, 2 * D_FF), lambda i: (i, 0))
    single = dict(pipeline_mode=pl.Buffered(1))
    kern = functools.partial(_ffn_kernel, seq_tiles=seq_tiles, period=period, keep=keep)
    return pl.pallas_call(
        kern,
        grid=(nt,),
        in_specs=[pl.BlockSpec((tm, D_MODEL), lambda i: (i, 0)), xp_spec, p_spec, p_spec,
                  _const_spec((1, D_MODEL)),
                  pl.BlockSpec((D_MODEL, 2 * D_FF), lambda i: (0, 0), **single),
                  _const_spec((FFN_CONV, 2 * D_FF)), _const_spec((1, 2 * D_FF)),
                  pl.BlockSpec((D_FF, D_MODEL), lambda i: (0, 0), **single)],
        out_specs=[pl.BlockSpec((tm, D_MODEL), lambda i: (i, 0)),
                   pl.BlockSpec((1, keep, 2 * D_FF), lambda i: (i, 0, 0))],
        out_shape=[jax.ShapeDtypeStruct((n, D_MODEL), F32),
                   jax.ShapeDtypeStruct((nt, keep, 2 * D_FF), F32)],
        scratch_shapes=[pltpu.VMEM((tm + halo, D_MODEL), BF16), pltpu.VMEM((tm, D_MODEL), F32)],
        compiler_params=_cparams("parallel"),
        name="ffn",
    )(x, xp, prev1, prev2, g, wup, cw, cb, wd)


SB_KT = 256
SB_PAGES = 8


def _head_block_mask(rows_per_head, n_rows):
    r = _iota((n_rows, BRANCH_WIDTH), 0) // rows_per_head
    c = _iota((n_rows, BRANCH_WIDTH), 1) // HEAD_DIM
    return r == c


def _suffix_matrix(n):
    return jnp.where(_iota((n, n), 0) > _iota((n, n), 1), 1.0, 0.0).astype(BF16)


def _sb_block(qbd, k, v, mask, carry, tri):
    z = _dot_nt(qbd, k)
    sp = _softplus(z)
    l1 = -sp if mask is None else jnp.where(mask, -sp, 0.0)
    hi = l1.astype(BF16)
    lo = (l1 - hi.astype(F32)).astype(BF16)
    after = _dot(hi, tri) + _dot(lo, tri) + carry
    a = jnp.exp(z + l1 + after)
    if mask is not None:
        a = jnp.where(mask, a, 0.0)
    return _dot(a.astype(BF16), v), carry + jnp.sum(l1, axis=1, keepdims=True)


SB_DEAD = -104.0


def _sb_alive(carry):
    return (jnp.max(carry) > SB_DEAD).astype(jnp.int32)


def _fold_heads(acc, rows_per_head):
    masked = jnp.where(_head_block_mask(rows_per_head, acc.shape[0]), acc, 0.0)
    out = masked[0:rows_per_head]
    for h in range(1, N_HEADS):
        out = out + masked[h * rows_per_head:(h + 1) * rows_per_head]
    return out


def _sb_prompt_kernel(q_ref, k_ref, v_ref, o_ref, acc_ref, carry_ref):
    i = pl.program_id(1)
    rows = N_HEADS * Q_BLOCK
    q = q_ref[...] * (HEAD_DIM ** -0.5)
    qbd = jnp.where(_head_block_mask(Q_BLOCK, rows), jnp.concatenate([q] * N_HEADS, axis=0), 0.0).astype(BF16)
    tri = _suffix_matrix(SB_KT)
    acc_ref[...] = jnp.zeros_like(acc_ref)
    carry_ref[...] = jnp.zeros_like(carry_ref)
    qpos = i * Q_BLOCK + _iota((rows, 1), 0) % Q_BLOCK
    n_chunks = (i * Q_BLOCK) // SB_KT + 1

    def body(state):
        jj, _ = state
        j = n_chunks - 1 - jj
        start = pl.multiple_of(j * SB_KT, SB_KT)
        kpos = start + _iota((1, SB_KT), 1)
        pv, carry = _sb_block(qbd, k_ref[pl.ds(start, SB_KT), :], v_ref[pl.ds(start, SB_KT), :],
                              kpos < qpos, carry_ref[...], tri)
        acc_ref[...] += pv
        carry_ref[...] = carry
        return jj + 1, _sb_alive(carry)

    lax.while_loop(lambda st: (st[0] < n_chunks) & (st[1] > 0), body, (jnp.int32(0), jnp.int32(1)))
    o_ref[...] = _fold_heads(acc_ref[...], Q_BLOCK)


def _sb_prompt(u3, kb, vb):
    b, t, _ = u3.shape
    rows = N_HEADS * Q_BLOCK
    return pl.pallas_call(
        _sb_prompt_kernel,
        grid=(b, t // Q_BLOCK),
        in_specs=[pl.BlockSpec((None, Q_BLOCK, BRANCH_WIDTH), lambda bi, i: (bi, i, SEG_SB // BRANCH_WIDTH)),
                  pl.BlockSpec((None, t, BRANCH_WIDTH), lambda bi, i: (bi, 0, 0)),
                  pl.BlockSpec((None, t, BRANCH_WIDTH), lambda bi, i: (bi, 0, 0))],
        out_specs=pl.BlockSpec((None, Q_BLOCK, BRANCH_WIDTH), lambda bi, i: (bi, i, 0)),
        out_shape=jax.ShapeDtypeStruct((b, t, BRANCH_WIDTH), F32),
        scratch_shapes=[pltpu.VMEM((rows, BRANCH_WIDTH), F32), pltpu.VMEM((rows, 1), F32)],
        compiler_params=_cparams("parallel", "parallel"),
        name="sb_prompt",
    )(u3, kb, vb)


SAMPLE_ROWS = 8


def _sb_sample_kernel(pt_ref, q_ref, kn_ref, vn_ref, *refs, n_steps):
    k_refs = refs[:SB_PAGES]
    v_refs = refs[SB_PAGES:2 * SB_PAGES]
    o_ref, acc_ref, carry_ref, alive_ref = refs[2 * SB_PAGES:]
    s = pl.program_id(1)
    rows = N_HEADS * SAMPLE_ROWS

    def queries():
        return (q_ref[...] * HEAD_DIM ** -0.5).astype(BF16), _suffix_matrix(PAGE_SIZE)

    @pl.when(s == 0)
    def _():
        qbd, tri = queries()
        step = _iota((rows, 1), 0) % SAMPLE_ROWS
        col = _iota((1, PAGE_SIZE), 1)
        pv, carry = _sb_block(qbd, kn_ref[...].astype(BF16), vn_ref[...].astype(BF16), col < step,
                              jnp.zeros((rows, 1), F32), tri)
        acc_ref[...] = pv
        carry_ref[...] = carry
        alive_ref[0] = _sb_alive(carry)

    for r in range(SB_PAGES):
        @pl.when(alive_ref[0] > 0)
        def _():
            qbd, tri = queries()
            pv, carry = _sb_block(qbd, k_refs[r][...].astype(BF16), v_refs[r][...].astype(BF16), None,
                                  carry_ref[...], tri)
            acc_ref[...] += pv
            carry_ref[...] = carry
            alive_ref[0] = _sb_alive(carry)

    @pl.when(s == n_steps - 1)
    def _():
        o_ref[...] = _fold_heads(acc_ref[...], SAMPLE_ROWS)


def _sb_sample(page_table, qbd, kn, vn, cache_k, cache_v, layer, n_pool):
    b, n_pages = page_table.shape
    n_steps = n_pages // SB_PAGES
    rows = N_HEADS * SAMPLE_ROWS
    base = layer * n_pool

    def page_spec(r):
        return pl.BlockSpec((None, PAGE_SIZE, BRANCH_WIDTH),
                            lambda bi, s, pt: (base + pt[bi, n_pages - 1 - (s * SB_PAGES + r)], 0, 0))

    per_b = lambda shape: pl.BlockSpec((None,) + shape, lambda bi, s, pt: (bi, 0, 0))
    grid_spec = pltpu.PrefetchScalarGridSpec(
        num_scalar_prefetch=1,
        grid=(b, n_steps),
        in_specs=[per_b((rows, BRANCH_WIDTH)), per_b((PAGE_SIZE, BRANCH_WIDTH)), per_b((PAGE_SIZE, BRANCH_WIDTH))]
                 + [page_spec(r) for r in range(SB_PAGES)] * 2,
        out_specs=per_b((SAMPLE_ROWS, BRANCH_WIDTH)),
        scratch_shapes=[pltpu.VMEM((rows, BRANCH_WIDTH), F32), pltpu.VMEM((rows, 1), F32),
                        pltpu.SMEM((1,), jnp.int32)],
    )
    return pl.pallas_call(
        functools.partial(_sb_sample_kernel, n_steps=n_steps),
        grid_spec=grid_spec,
        out_shape=jax.ShapeDtypeStruct((b, SAMPLE_ROWS, BRANCH_WIDTH), F32),
        compiler_params=_cparams("parallel", "arbitrary"),
        name="sb_sample",
    )(page_table, qbd, kn, vn, *([cache_k] * SB_PAGES), *([cache_v] * SB_PAGES))


NSA_KT = 512
ALIBI_SLOPES = tuple(2.0 ** (-8.0 * (h + 1.0) / N_HEADS) for h in range(N_HEADS))


def _row_slopes(rq):
    h = _iota((N_HEADS * rq, 1), 0) // rq
    out = jnp.full((N_HEADS * rq, 1), ALIBI_SLOPES[0], F32)
    for i in range(1, N_HEADS):
        out = jnp.where(h == i, ALIBI_SLOPES[i], out)
    return out


def _rep_right_matrix():
    r = _iota((LANES, BRANCH_WIDTH), 0)
    c = _iota((LANES, BRANCH_WIDTH), 1)
    return jnp.where(r == HEAD_DIM + c % HEAD_DIM, 1.0, 0.0)


def _nsa_prep_kernel(q_ref, kv_ref, qg_ref, kg_ref, qs_out, newk_out, newv_out, win_out, newkb_out, winb_out,
                     vslc_out, vwin_out):
    q = q_ref[...]
    qn = q * lax.rsqrt(_dot_hi(q * q, _head_sum_matrix(1.0 / HEAD_DIM)) + NORM_EPS) * qg_ref[...]
    qs_out[...] = _dot((qn * HEAD_DIM ** -0.5).astype(BF16), _query_place_matrix()).astype(BF16)
    kv = kv_ref[...]
    w = kv.shape[1]
    same = _iota((w, w), 0) // HEAD_DIM == _iota((w, w), 1) // HEAD_DIM
    ms = _dot_hi(kv * kv, jnp.where(same, 1.0 / HEAD_DIM, 0.0))
    grp = _iota((1, w), 1) // HEAD_DIM
    normed = jnp.where((grp == 1) | (grp == 4), kv * lax.rsqrt(ms + NORM_EPS) * kg_ref[...], kv)
    newk = normed[:, 0:LANES]
    newv = normed[:, LANES:2 * LANES]
    win = normed[:, 2 * LANES:3 * LANES]
    newk_out[...] = newk
    newv_out[...] = newv
    win_out[...] = win
    newkb_out[...] = newk.astype(BF16)
    winb_out[...] = win.astype(BF16)
    rep = _rep_right_matrix().astype(BF16)
    vslc_out[...] = _dot(newv.astype(BF16), rep).astype(BF16)
    vwin_out[...] = _dot(win.astype(BF16), rep).astype(BF16)


def _nsa_prep(u, q_gain, k_gain, tm):
    n = u.shape[0]
    out = lambda w: pl.BlockSpec((tm, w), lambda i: (i, 0))
    shp = lambda w, dt: jax.ShapeDtypeStruct((n, w), dt)
    return pl.pallas_call(
        _nsa_prep_kernel,
        grid=(n // tm,),
        in_specs=[pl.BlockSpec((tm, BRANCH_WIDTH), lambda i: (i, SEG_NSA_Q // BRANCH_WIDTH)),
                  pl.BlockSpec((tm, 3 * LANES), lambda i: (i, SEG_NSA_KV // (3 * LANES))),
                  _const_spec(q_gain.shape), _const_spec(k_gain.shape)],
        out_specs=[out(PLACED_WIDTH), out(LANES), out(LANES), out(LANES), out(LANES), out(LANES),
                   out(BRANCH_WIDTH), out(BRANCH_WIDTH)],
        out_shape=[shp(PLACED_WIDTH, BF16), shp(LANES, F32), shp(LANES, F32), shp(LANES, F32), shp(LANES, BF16),
                   shp(LANES, BF16), shp(BRANCH_WIDTH, BF16), shp(BRANCH_WIDTH, BF16)],
        compiler_params=_cparams("parallel"),
        name="nsa_prep",
    )(u, u, q_gain, k_gain)


def _nsa_compress(k_ref, v_ref, pek_ref, pev_ref, wk_ref, wv_ref, kn_ref, nb):
    def body(m, carry):
        ak, av = carry
        xk = k_ref[pl.ds(m, nb, stride=NSA_BLOCK), :] + pek_ref[pl.ds(m, 1), :]
        xv = v_ref[pl.ds(m, nb, stride=NSA_BLOCK), :] + pev_ref[pl.ds(m, 1), :]
        return ak + _dot(xk.astype(BF16), wk_ref[m]), av + _dot(xv.astype(BF16), wv_ref[m])

    ak, av = lax.fori_loop(0, NSA_BLOCK, body,
                           (jnp.zeros((nb, LANES), F32), jnp.zeros((nb, BRANCH_WIDTH), F32)))
    ms = jnp.sum(ak * ak, axis=1, keepdims=True) * (1.0 / HEAD_DIM)
    kc = ak * lax.rsqrt(ms + NORM_EPS) * kn_ref[...]
    return kc.astype(BF16), av.astype(BF16)


def _nsa_compress_kernel(k_ref, v_ref, pek_ref, pev_ref, wk_ref, wv_ref, kn_ref, kc_out, vc_out):
    kc, vc = _nsa_compress(k_ref, v_ref, pek_ref, pev_ref, wk_ref, wv_ref, kn_ref, kc_out.shape[0])
    kc_out[...] = kc
    vc_out[...] = vc


def _nsa_compress_prompt(newk, newv, p):
    b, t, _ = newk.shape
    nb = t // NSA_BLOCK
    per_b = lambda rows, w: pl.BlockSpec((None, rows, w), lambda bi: (bi, 0, 0))
    consts = [p["pe_k"], p["pe_v"], p["w_k"], p["w_v"], p["k_gain_pad"]]
    return pl.pallas_call(
        _nsa_compress_kernel,
        grid=(b,),
        in_specs=[per_b(t, LANES), per_b(t, LANES)] + [_const_spec(a.shape) for a in consts],
        out_specs=[per_b(nb, LANES), per_b(nb, BRANCH_WIDTH)],
        out_shape=[jax.ShapeDtypeStruct((b, nb, LANES), BF16), jax.ShapeDtypeStruct((b, nb, BRANCH_WIDTH), BF16)],
        compiler_params=_cparams("parallel"),
        name="nsa_compress",
    )(newk, newv, *consts)


def _nsa_cmp_branch(ql, kc, vc, qpos, q0, slopes):
    nb = kc.shape[0]
    bend = (_iota((1, nb), 1) + 1) * NSA_BLOCK - 1
    valid = bend <= qpos
    s = jnp.where(valid, _dot_nt(ql, kc) + slopes * (bend - q0).astype(F32), NEG)
    e = jnp.exp(s - jnp.max(s, axis=1, keepdims=True))
    p = jnp.where(valid, e / jnp.sum(e, axis=1, keepdims=True), 0.0)
    return _dot(p.astype(BF16), vc), p


def _nsa_select(imp, qpos_q, nbl):
    blk = _iota((1, nbl), 1)
    blk_f = blk.astype(F32)
    cur = qpos_q // NSA_BLOCK
    forced = (blk == 0) | (blk == cur) | (blk == cur - 1)
    work = jnp.where(blk > cur, NEG, jnp.where(forced, NSA_FORCED, imp))
    sel = jnp.zeros(work.shape, F32)
    for _ in range(min(NSA_TOPK, nbl)):
        mx = jnp.max(work, axis=1, keepdims=True)
        idx = jnp.min(jnp.where(work == mx, blk_f, float(nbl)), axis=1, keepdims=True)
        hit = blk_f == idx
        sel = jnp.where(hit & (mx > 0.5 * NEG), 1.0, sel)
        work = jnp.where(hit, -3e38, work)
    return sel


def _block_expand_matrix(nbl, start, kt):
    return jnp.where(_iota((nbl, kt), 0) == (start + _iota((nbl, kt), 1)) // NSA_BLOCK, 1.0, 0.0).astype(BF16)


def _gate_expand(gl, j):
    r = _iota((LANES, BRANCH_WIDTH), 0)
    c = _iota((LANES, BRANCH_WIDTH), 1)
    return _sigmoid(_dot_hi(gl, jnp.where(r == (c // HEAD_DIM) * 3 + j, 1.0, 0.0)))


def _softmax_rows(s):
    e = jnp.exp(s - jnp.max(s, axis=1, keepdims=True))
    return e / jnp.sum(e, axis=1, keepdims=True)


PLACED_WIDTH = 2 * N_HEADS * LANES


def _query_place_matrix():
    r = _iota((BRANCH_WIDTH, PLACED_WIDTH), 0)
    j = _iota((BRANCH_WIDTH, PLACED_WIDTH), 1)
    side = j // (N_HEADS * LANES)
    head = (j // LANES) % N_HEADS
    c = j % LANES
    left = (side == 0) & (c < HEAD_DIM) & (r == head * HEAD_DIM + c)
    right = (side == 1) & (c >= HEAD_DIM) & (r == head * HEAD_DIM + c - HEAD_DIM)
    return jnp.where(left | right, 1.0, 0.0).astype(BF16)


def _placed_rows(qs):
    ql = jnp.concatenate([qs[:, h * LANES:(h + 1) * LANES] for h in range(N_HEADS)], axis=0)
    qr = jnp.concatenate([qs[:, (N_HEADS + h) * LANES:(N_HEADS + h + 1) * LANES] for h in range(N_HEADS)], axis=0)
    return ql, qr


def _nsa_prompt_kernel(qs_ref, gate_ref, kc_ref, vc_ref, kb_ref, vs_ref, wb_ref, vw_ref, o_ref,
                       m_ref, l_ref, acc_ref):
    i = pl.program_id(1)
    rq = Q_BLOCK
    rows = N_HEADS * rq
    q0 = i * rq
    nbl = kc_ref.shape[0]
    ql, qr = _placed_rows(qs_ref[...])
    slopes = _row_slopes(rq)
    qpos = q0 + _iota((rows, 1), 0) % rq
    qpos_q = q0 + _iota((rq, 1), 0)

    o_c, p_c = _nsa_cmp_branch(ql, kc_ref[...], vc_ref[...], qpos, q0, slopes)
    imp = p_c[0:rq]
    for h in range(1, N_HEADS):
        imp = imp + p_c[h * rq:(h + 1) * rq]
    sel = _nsa_select(imp, qpos_q, nbl).astype(BF16)

    span = NSA_WINDOW + rq
    wstart = pl.multiple_of(q0, rq)
    kpos_w = q0 - NSA_WINDOW + _iota((1, span), 1)
    dist = qpos - kpos_w
    valid_w = (dist >= 0) & (dist < NSA_WINDOW) & (kpos_w >= 0)
    s_w = _dot_nt(ql, wb_ref[pl.ds(wstart, span), :]) + slopes * (kpos_w - q0).astype(F32)
    o_w = _dot(_softmax_rows(jnp.where(valid_w, s_w, NEG)).astype(BF16), vw_ref[pl.ds(wstart, span), :])
    gl = gate_ref[...]
    o_ref[...] = _gate_expand(gl, 0) * _fold_heads(o_c, rq) + _gate_expand(gl, 2) * _fold_heads(o_w, rq)

    m_ref[...] = jnp.full(m_ref.shape, NEG, F32)
    l_ref[...] = jnp.zeros_like(l_ref)
    acc_ref[...] = jnp.zeros_like(acc_ref)
    n_chunks = (q0 + rq + NSA_KT - 1) // NSA_KT

    def body(j, _):
        start = pl.multiple_of(j * NSA_KT, NSA_KT)
        chosen = _dot(sel, _block_expand_matrix(nbl, start, NSA_KT))

        @pl.when(jnp.max(chosen) > 0.5)
        def _():
            s = _dot_nt(qr, kb_ref[pl.ds(start, NSA_KT), :])
            v = vs_ref[pl.ds(start, NSA_KT), :]
            kpos = start + _iota((1, NSA_KT), 1)
            ok = (chosen > 0.5) & (kpos <= qpos_q)
            rel = (kpos - q0).astype(F32)
            for h in range(N_HEADS):
                rs = slice(h * rq, (h + 1) * rq)
                s_h = jnp.where(ok, s[rs] + ALIBI_SLOPES[h] * rel, NEG)
                m_old = m_ref[rs]
                m_new = jnp.maximum(m_old, jnp.max(s_h, axis=1, keepdims=True))
                alpha = jnp.exp(m_old - m_new)
                p = jnp.where(ok, jnp.exp(s_h - m_new), 0.0)
                l_ref[rs] = alpha * l_ref[rs] + jnp.sum(p, axis=1, keepdims=True)
                acc_ref[rs] = alpha * acc_ref[rs] + _dot(p.astype(BF16), v)
                m_ref[rs] = m_new

        return 0

    lax.fori_loop(0, n_chunks, body, 0)
    o_ref[...] += _gate_expand(gl, 1) * _fold_heads(acc_ref[...] / l_ref[...], rq)


def _nsa_prompt(qs, u3, kc, vc, newk_b, vslc, win_b_pad, vwin_pad):
    b, t, _ = qs.shape
    nb = kc.shape[1]
    rows = N_HEADS * Q_BLOCK
    per_b = lambda r, w: pl.BlockSpec((None, r, w), lambda bi, i: (bi, 0, 0))
    return pl.pallas_call(
        _nsa_prompt_kernel,
        grid=(b, t // Q_BLOCK),
        in_specs=[pl.BlockSpec((None, Q_BLOCK, PLACED_WIDTH), lambda bi, i: (bi, i, 0)),
                  pl.BlockSpec((None, Q_BLOCK, LANES), lambda bi, i: (bi, i, SEG_NSA_GATE // LANES)),
                  per_b(nb, LANES), per_b(nb, BRANCH_WIDTH), per_b(t, LANES), per_b(t, BRANCH_WIDTH),
                  per_b(t + NSA_WINDOW, LANES), per_b(t + NSA_WINDOW, BRANCH_WIDTH)],
        out_specs=pl.BlockSpec((None, Q_BLOCK, BRANCH_WIDTH), lambda bi, i: (bi, i, 0)),
        out_shape=jax.ShapeDtypeStruct((b, t, BRANCH_WIDTH), F32),
        scratch_shapes=[pltpu.VMEM((rows, 1), F32), pltpu.VMEM((rows, 1), F32), pltpu.VMEM((rows, BRANCH_WIDTH), F32)],
        compiler_params=_cparams("parallel", "parallel"),
        name="nsa_prompt",
    )(qs, u3, kc, vc, newk_b, vslc, win_b_pad, vwin_pad)


NSA_PAGES = 8


def _nsa_sample_kernel(pt_ref, ql_ref, qr_ref, gate_ref, kn_ref, vn_ref, wbuf_ref, wn_ref, vwn_ref,
                       pek_ref, pev_ref, wk_ref, wv_ref, kg_ref, *refs, n_steps, n_new):
    k_pages = refs[:NSA_PAGES]
    v_pages = refs[NSA_PAGES:2 * NSA_PAGES]
    o_ref, kbuf_ref, vbuf_ref = refs[2 * NSA_PAGES:]
    s = pl.program_id(1)
    for r in range(NSA_PAGES):
        row0 = pl.multiple_of((s * NSA_PAGES + r) * PAGE_SIZE, PAGE_SIZE)
        kbuf_ref[pl.ds(row0, PAGE_SIZE), :] = k_pages[r][...]
        vbuf_ref[pl.ds(row0, PAGE_SIZE), :] = v_pages[r][...]

    @pl.when(s == n_steps - 1)
    def _():
        rq = SAMPLE_ROWS
        rows = N_HEADS * rq
        past = kbuf_ref.shape[0]
        nb = past // NSA_BLOCK
        n_buf = wbuf_ref.shape[0]
        ql = ql_ref[...]
        qr = qr_ref[...]
        slopes = _row_slopes(rq)
        step = _iota((rows, 1), 0) % rq
        qpos = past + step
        step_q = _iota((rq, 1), 0)
        rep = _rep_right_matrix()
        tile = lambda a: jnp.concatenate([a] * N_HEADS, axis=0)

        kc, vc = _nsa_compress(kbuf_ref, vbuf_ref, pek_ref, pev_ref, wk_ref, wv_ref, kg_ref, nb)
        o_c, p_c = _nsa_cmp_branch(ql, kc, vc, qpos, past, slopes)
        imp = p_c[0:rq]
        for h in range(1, N_HEADS):
            imp = imp + p_c[h * rq:(h + 1) * rq]
        width = -(-(nb + 1) // LANES) * LANES
        imp = jnp.concatenate([imp, jnp.zeros((rq, width - nb), F32)], axis=1)
        sel = _nsa_select(imp, past + step_q, width)

        col = _iota((1, PAGE_SIZE), 1)
        ok_p = tile(_dot(sel[:, 0:nb].astype(BF16), _block_expand_matrix(nb, 0, past)) > 0.5)
        s_p = _dot_nt(qr, kbuf_ref[...].astype(BF16)) + slopes * (_iota((1, past), 1) - past).astype(F32)
        s_p = jnp.where(ok_p, s_p, NEG)
        ok_n = tile(sel[:, nb:nb + 1] > 0.5) & (col <= step) & (col < n_new)
        s_n = jnp.where(ok_n, _dot_nt(qr, kn_ref[...]) + slopes * col.astype(F32), NEG)
        m = jnp.maximum(jnp.max(s_p, axis=1, keepdims=True), jnp.max(s_n, axis=1, keepdims=True))
        p_p = jnp.where(ok_p, jnp.exp(s_p - m), 0.0)
        p_n = jnp.where(ok_n, jnp.exp(s_n - m), 0.0)
        den = jnp.sum(p_p, axis=1, keepdims=True) + jnp.sum(p_n, axis=1, keepdims=True)
        o_s = (_dot_hi(_dot(p_p.astype(BF16), vbuf_ref[...].astype(BF16)), rep)
               + _dot(p_n.astype(BF16), vn_ref[...])) / den

        wb = wbuf_ref[...].astype(BF16)
        cw = _iota((1, n_buf), 1)
        kpos_w = past - n_buf + cw
        dist = qpos - kpos_w
        ok_w = (dist >= 0) & (dist < NSA_WINDOW) & (kpos_w >= 0)
        s_w = jnp.where(ok_w, _dot_nt(ql, wb) + slopes * (cw - n_buf).astype(F32), NEG)
        ok_wn = (col <= step) & (col < n_new)
        s_wn = jnp.where(ok_wn, _dot_nt(ql, wn_ref[...]) + slopes * col.astype(F32), NEG)
        m = jnp.maximum(jnp.max(s_w, axis=1, keepdims=True), jnp.max(s_wn, axis=1, keepdims=True))
        p_w = jnp.where(ok_w, jnp.exp(s_w - m), 0.0)
        p_wn = jnp.where(ok_wn, jnp.exp(s_wn - m), 0.0)
        den = jnp.sum(p_w, axis=1, keepdims=True) + jnp.sum(p_wn, axis=1, keepdims=True)
        o_w = (_dot_hi(_dot(p_w.astype(BF16), wb), rep) + _dot(p_wn.astype(BF16), vwn_ref[...])) / den

        gl = gate_ref[...]
        o_ref[...] = (_gate_expand(gl, 0) * _fold_heads(o_c, rq) + _gate_expand(gl, 1) * _fold_heads(o_s, rq)
                      + _gate_expand(gl, 2) * _fold_heads(o_w, rq))


def _nsa_sample(page_table, ql, qr, gate, kn, vn, wbuf, wn, vwn, p, cache_k, cache_v, layer, n_pool, n_new):
    b, n_pages = page_table.shape
    n_steps = n_pages // NSA_PAGES
    past = n_pages * PAGE_SIZE
    base = layer * n_pool

    def page_spec(r):
        return pl.BlockSpec((None, PAGE_SIZE, LANES), lambda bi, s, pt: (base + pt[bi, s * NSA_PAGES + r], 0, 0))

    per_b = lambda a: pl.BlockSpec((None,) + a.shape[1:], lambda bi, s, pt: (bi, 0, 0))
    const = lambda a: pl.BlockSpec(a.shape, lambda bi, s, pt: (0,) * a.ndim)
    consts = [p["pe_k"], p["pe_v"], p["w_k"], p["w_v"], p["k_gain_pad"]]
    seq_ops = [ql, qr, gate, kn, vn, wbuf, wn, vwn]
    grid_spec = pltpu.PrefetchScalarGridSpec(
        num_scalar_prefetch=1,
        grid=(b, n_steps),
        in_specs=[per_b(a) for a in seq_ops] + [const(a) for a in consts]
                 + [page_spec(r) for r in range(NSA_PAGES)] * 2,
        out_specs=pl.BlockSpec((None, SAMPLE_ROWS, BRANCH_WIDTH), lambda bi, s, pt: (bi, 0, 0)),
        scratch_shapes=[pltpu.VMEM((past, LANES), F32), pltpu.VMEM((past, LANES), F32)],
    )
    return pl.pallas_call(
        functools.partial(_nsa_sample_kernel, n_steps=n_steps, n_new=n_new),
        grid_spec=grid_spec,
        out_shape=jax.ShapeDtypeStruct((b, SAMPLE_ROWS, BRANCH_WIDTH), F32),
        compiler_params=_cparams("parallel", "arbitrary"),
        name="nsa_sample",
    )(page_table, *seq_ops, *consts, *([cache_k] * NSA_PAGES), *([cache_v] * NSA_PAGES))


def _nsa_sample_mix(u3, page_table, wbuf, p, cache_k, cache_v, layer, n_pool):
    b, t, _ = u3.shape
    qs, newk, newv, win, newk_b, win_b, vslc, vwin = _nsa_prep(u3.reshape(b * t, -1), p["q_gain"], p["k_gain"], b * t)
    q4 = qs.reshape(b, t, 2, N_HEADS, LANES).transpose(2, 0, 3, 1, 4)
    q4 = jnp.pad(q4, ((0, 0), (0, 0), (0, 0), (0, SAMPLE_ROWS - t), (0, 0)))
    ql, qr = q4.reshape(2, b, N_HEADS * SAMPLE_ROWS, LANES)
    gate = jnp.pad(u3[..., SEG_NSA_GATE:SEG_NSA_GATE + LANES], ((0, 0), (0, SAMPLE_ROWS - t), (0, 0)))
    page_rows = lambda a: jnp.pad(a.reshape(b, t, -1), ((0, 0), (0, PAGE_SIZE - t), (0, 0)))
    o = _nsa_sample(page_table, ql, qr, gate, page_rows(newk_b), page_rows(vslc), wbuf, page_rows(win_b),
                    page_rows(vwin), p, cache_k, cache_v, layer, n_pool, t)
    return o[:, :t], newk, newv, win


def _nsa_params(q_norm, k_norm, cmp_pe, cmp_w):
    lane_pad = lambda a: jnp.pad(a, ((0, 0),) * (a.ndim - 1) + ((0, LANES - HEAD_DIM),))
    wk = cmp_w[0].reshape(NSA_BLOCK, HEAD_DIM, HEAD_DIM)
    wv = cmp_w[1].reshape(NSA_BLOCK, HEAD_DIM, HEAD_DIM)
    pad_rows = lambda a: jnp.pad(a, ((0, 0), (0, LANES - HEAD_DIM), (0, 0)))
    return dict(q_gain=jnp.tile(q_norm, N_HEADS)[None, :], k_gain=jnp.tile(k_norm, 6)[None, :],
                k_gain_pad=lane_pad(k_norm[None, :]), pe_k=lane_pad(cmp_pe[0]), pe_v=lane_pad(cmp_pe[1]),
                w_k=pad_rows(lane_pad(wk)).astype(BF16), w_v=pad_rows(jnp.tile(wv, (1, 1, N_HEADS))).astype(BF16))


SSM_HIST = 8
GROUP_LANES = BRANCH_WIDTH // SSM_GROUPS


def _ssd_kernel(xbc_ref, z_ref, dt_ref, dtt_ref, hist0_ref, h0_ref, cw_ref, cb_ref, bias_ref, biast_ref,
                a_ref, at_ref, dskip_ref, ng_ref, y_ref, hout_ref, hist_ref, state_ref, *, n_valid):
    c = pl.program_id(1)
    q = xbc_ref.shape[0]

    @pl.when(c == 0)
    def _():
        hist_ref[0:SSM_HIST, :] = hist0_ref[...]
        state_ref[...] = h0_ref[...]

    hist_ref[SSM_HIST:SSM_HIST + q, :] = xbc_ref[...]
    full = hist_ref[...]
    conv = cw_ref[3:4, :] * full + cb_ref[...]
    for k in range(1, SSM_CONV):
        conv = conv + cw_ref[3 - k:4 - k, :] * pltpu.roll(full, k, axis=0)
    hist_ref[0:SSM_HIST, :] = full[q:q + SSM_HIST, :]
    act = _silu(conv[SSM_HIST:, :])
    xs = act[:, 0:BRANCH_WIDTH]
    xs_b = xs.astype(BF16)
    bm = act[:, BRANCH_WIDTH:BRANCH_WIDTH + SSM_GROUPS * SSM_STATE].astype(BF16)
    cm = act[:, BRANCH_WIDTH + SSM_GROUPS * SSM_STATE:].astype(BF16)

    dt = jnp.where(_iota((q, LANES), 0) < n_valid, _softplus(dt_ref[...] + bias_ref[...]), 0.0)
    dta = dt * -jnp.exp(a_ref[...])
    dtt = jnp.where(_iota((SSM_HIST, q), 1) < n_valid, _softplus(dtt_ref[...] + biast_ref[...]), 0.0)
    dtat = dtt * -jnp.exp(at_ref[...])
    tril = jnp.where(_iota((q, q), 0) >= _iota((q, q), 1), 1.0, 0.0)
    triu = jnp.where(_iota((q, q), 0) <= _iota((q, q), 1), 1.0, 0.0)
    expand = jnp.where(_iota((LANES, BRANCH_WIDTH), 0) == _iota((LANES, BRANCH_WIDTH), 1) // HEAD_DIM, 1.0, 0.0)
    dt_e = _dot_hi(dt, expand)
    cum_e = _dot_hi(tril, _dot_hi(dta, expand))
    cum_t = _dot_hi(dtat, triu)
    causal = _iota((q, q), 0) >= _iota((q, q), 1)
    lane_head = _iota((1, BRANCH_WIDTH), 1) // HEAD_DIM

    y = dskip_ref[...] * xs
    for g in range(SSM_GROUPS):
        sl = slice(g * SSM_STATE, (g + 1) * SSM_STATE)
        gram = _dot_nt(cm[:, sl], bm[:, sl])
        for h in range(g * (N_HEADS // SSM_GROUPS), (g + 1) * (N_HEADS // SSM_GROUPS)):
            pick = jnp.where(_iota((LANES, q), 0) == h, 1.0, 0.0)
            cum_l = _dot_hi(tril, _dot_hi(dta, pick))
            seg = cum_l - cum_t[h:h + 1, :]
            decay = jnp.where(causal, jnp.exp(jnp.minimum(seg, 0.0)), 0.0)
            scores = (gram * decay * dtt[h:h + 1, :]).astype(BF16)
            y = y + jnp.where(lane_head == h, _dot(scores, xs_b), 0.0)
    state = state_ref[...]
    inter = jnp.concatenate(
        [_dot(cm[:, g * SSM_STATE:(g + 1) * SSM_STATE],
              state[:, g * GROUP_LANES:(g + 1) * GROUP_LANES].astype(BF16)) for g in range(SSM_GROUPS)], axis=1)
    y = y + jnp.exp(cum_e) * inter

    cum_last = cum_e[q - 1:q, :]
    xw = (xs * jnp.exp(cum_last - cum_e) * dt_e).astype(BF16)
    contrib = jnp.concatenate(
        [lax.dot_general(bm[:, g * SSM_STATE:(g + 1) * SSM_STATE], xw[:, g * GROUP_LANES:(g + 1) * GROUP_LANES],
                         (((0,), (0,)), ((), ())), preferred_element_type=F32) for g in range(SSM_GROUPS)], axis=1)
    state_ref[...] = state * jnp.exp(cum_last) + contrib

    y = y * _silu(z_ref[...])
    parts = []
    for g in range(SSM_GROUPS):
        yg = y[:, g * GROUP_LANES:(g + 1) * GROUP_LANES]
        parts.append(yg * lax.rsqrt(jnp.mean(yg * yg, axis=-1, keepdims=True) + NORM_EPS))
    y_ref[...] = jnp.concatenate(parts, axis=1) * ng_ref[...]

    @pl.when(c == pl.num_programs(1) - 1)
    def _():
        hout_ref[...] = state_ref[...]


def _ssd(u3, dtt, hist0, h0, p, *, q, n_valid):
    b, t, _ = u3.shape
    per_b = lambda shape: pl.BlockSpec((None,) + shape, lambda bi, c: (bi, 0, 0))
    col = lambda width, seg: pl.BlockSpec((None, q, width), lambda bi, c: (bi, c, seg // width))
    consts = [p["conv_w"], p["conv_b"], p["dt_bias"], p["dt_bias_t"], p["a"], p["a_t"], p["d_skip"], p["norm_g"]]
    return pl.pallas_call(
        functools.partial(_ssd_kernel, n_valid=n_valid),
        grid=(b, t // q),
        in_specs=[col(SSM_CONV_DIM, SEG_SSM_XBC), col(BRANCH_WIDTH, SEG_SSM_Z), col(LANES, SEG_SSM_DT),
                  pl.BlockSpec((None, SSM_HIST, q), lambda bi, c: (bi, 0, c)),
                  per_b((SSM_HIST, SSM_CONV_DIM)), per_b((SSM_STATE, BRANCH_WIDTH))]
                 + [_const_spec(a.shape) for a in consts],
        out_specs=[pl.BlockSpec((None, q, BRANCH_WIDTH), lambda bi, c: (bi, c, 0)),
                   per_b((SSM_STATE, BRANCH_WIDTH))],
        out_shape=[jax.ShapeDtypeStruct((b, t, BRANCH_WIDTH), F32),
                   jax.ShapeDtypeStruct((b, SSM_STATE, BRANCH_WIDTH), F32)],
        scratch_shapes=[pltpu.VMEM((SSM_HIST + q, SSM_CONV_DIM), F32), pltpu.VMEM((SSM_STATE, BRANCH_WIDTH), F32)],
        compiler_params=_cparams("parallel", "arbitrary"),
        name="ssd",
    )(u3, u3, u3, dtt, hist0, h0, *consts)


def _head_sum_matrix(scale):
    same = _iota((BRANCH_WIDTH, BRANCH_WIDTH), 0) // HEAD_DIM == _iota((BRANCH_WIDTH, BRANCH_WIDTH), 1) // HEAD_DIM
    return jnp.where(same, scale, 0.0)


def _rwkv_prep_kernel(u_ref, up_ref, p1_ref, mu_ref, w0_ref, w2_ref, a0_ref, a2_ref, g2_ref, kk_ref, ka_ref, rk_ref,
                      r_out, k_out, v_out, d_out, nkk_out, kka_out, g_out, bonus_out, *, seq_tiles, period):
    tm = u_ref.shape[0]
    u = u_ref[...]
    rolled = pltpu.roll(u, 1, axis=0)
    if seq_tiles > 0:
        first = (pl.program_id(0) % seq_tiles) == 0
        carry_in = up_ref[FFN_HALO - 1:FFN_HALO, :] * jnp.where(first, 0.0, 1.0)
        prev = jnp.where(_iota((tm, 1), 0) == 0, carry_in, rolled)
    else:
        prev = jnp.where(_iota((tm, 1), 0) % period >= 1, rolled, p1_ref[...])
    us = u + (prev - u) * mu_ref[...]
    r = us[:, 0:BRANCH_WIDTH]
    k = us[:, BRANCH_WIDTH:2 * BRANCH_WIDTH]
    v = us[:, 2 * BRANCH_WIDTH:3 * BRANCH_WIDTH]
    wa = us[:, 3 * BRANCH_WIDTH:3 * BRANCH_WIDTH + LANES]
    gd = us[:, 3 * BRANCH_WIDTH + LANES:]
    is_w = _iota((1, LANES), 1) < RWKV_W_LORA
    w_lora = _dot(jnp.where(is_w, jnp.tanh(wa), 0.0).astype(BF16), w2_ref[...])
    a_lora = _dot(jnp.where(is_w, 0.0, wa).astype(BF16), a2_ref[...])
    w_raw = -_softplus(-(w0_ref[...] + w_lora)) - 0.5
    log_decay = -jnp.exp(w_raw)
    a = _sigmoid(a0_ref[...] + a_lora)
    g = _dot(_sigmoid(gd).astype(BF16), g2_ref[...])
    head_sum = _head_sum_matrix(1.0)
    kk = k * kk_ref[...]
    kk = kk / jnp.maximum(jnp.sqrt(_dot_hi(kk * kk, head_sum)), 1e-12)
    k_mod = k * (1.0 + (a - 1.0) * ka_ref[...])
    r_out[...] = r
    k_out[...] = k_mod
    v_out[...] = v
    d_out[...] = log_decay
    nkk_out[...] = -kk
    kka_out[...] = kk * a
    g_out[...] = g
    bonus_out[...] = _dot_hi(r * k_mod * rk_ref[...], head_sum) * v


def _rwkv_prep(u, p1, p, *, tm, seq_tiles, period):
    n = u.shape[0]
    hb = tm // FFN_HALO
    seg = SEG_RWKV // RWKV_IN
    if seq_tiles > 0:
        up_spec = pl.BlockSpec((FFN_HALO, RWKV_IN), lambda i: (jnp.maximum(i * hb - 1, 0), seg))
        p_spec = _const_spec(p1.shape)
    else:
        up_spec = pl.BlockSpec((FFN_HALO, RWKV_IN), lambda i: (0, seg))
        p_spec = pl.BlockSpec((tm, RWKV_IN), lambda i: (i, 0))
    consts = [p["mu"], p["w0"], p["w2"], p["a0"], p["a2"], p["g2"], p["k_k"], p["k_a"], p["r_k"]]
    out = pl.BlockSpec((tm, BRANCH_WIDTH), lambda i: (i, 0))
    return pl.pallas_call(
        functools.partial(_rwkv_prep_kernel, seq_tiles=seq_tiles, period=period),
        grid=(n // tm,),
        in_specs=[pl.BlockSpec((tm, RWKV_IN), lambda i: (i, seg)), up_spec, p_spec]
                 + [_const_spec(a.shape) for a in consts],
        out_specs=[out] * 8,
        out_shape=[jax.ShapeDtypeStruct((n, BRANCH_WIDTH), F32)] * 8,
        compiler_params=_cparams("parallel"),
        name="rwkv_prep",
    )(u, u, p1, *consts)


RWKV_CHAINS = 8
RWKV_TB = 128


def _rwkv_scan_kernel(r_ref, k_ref, d_ref, nkk_ref, kka_ref, vt_ref, s0_ref, yt_ref, sout_ref, s_ref, *, n_steps):
    @pl.when(pl.program_id(1) == 0)
    def _():
        s_ref[...] = s0_ref[...]

    yt_ref[...] = jnp.zeros_like(yt_ref)
    lane = _iota((HEAD_DIM, RWKV_TB), 1)

    def step(t, _):
        for c in range(RWKV_CHAINS):
            row = lambda ref: ref[c, pl.ds(t, 1), :]
            s = s_ref[c]
            sa = jnp.sum(s * row(nkk_ref), axis=1, keepdims=True)
            v_col = jnp.sum(jnp.where(lane == t, vt_ref[c], 0.0), axis=1, keepdims=True)
            s = s * jnp.exp(row(d_ref)) + sa * row(kka_ref) + v_col * row(k_ref)
            s_ref[c] = s
            y_col = jnp.sum(s * row(r_ref), axis=1, keepdims=True)
            yt_ref[c] = jnp.where(lane == t, y_col, yt_ref[c])
        return 0

    lax.fori_loop(0, n_steps, step, 0)

    @pl.when(pl.program_id(1) == pl.num_programs(1) - 1)
    def _():
        sout_ref[...] = s_ref[...]


def _rwkv_scan(r, k, d, nkk, kka, vt, s0, *, rows, n_steps):
    chains, t_rows, _ = r.shape
    t_lanes = vt.shape[2]
    row_spec = pl.BlockSpec((RWKV_CHAINS, rows, HEAD_DIM), lambda ci, tb: (ci, tb, 0))
    lane_spec = pl.BlockSpec((RWKV_CHAINS, HEAD_DIM, RWKV_TB), lambda ci, tb: (ci, 0, tb))
    state_spec = pl.BlockSpec((RWKV_CHAINS, HEAD_DIM, HEAD_DIM), lambda ci, tb: (ci, 0, 0))
    return pl.pallas_call(
        functools.partial(_rwkv_scan_kernel, n_steps=n_steps),
        grid=(chains // RWKV_CHAINS, t_lanes // RWKV_TB),
        in_specs=[row_spec] * 5 + [lane_spec, state_spec],
        out_specs=[lane_spec, state_spec],
        out_shape=[jax.ShapeDtypeStruct((chains, HEAD_DIM, t_lanes), F32),
                   jax.ShapeDtypeStruct((chains, HEAD_DIM, HEAD_DIM), F32)],
        scratch_shapes=[pltpu.VMEM((RWKV_CHAINS, HEAD_DIM, HEAD_DIM), F32)],
        compiler_params=_cparams("parallel", "arbitrary"),
        name="rwkv_scan",
    )(r, k, d, nkk, kka, vt, s0)


RWKV_CHUNK = 64


def _rwkv_chunk_kernel(r_ref, k_ref, v_ref, ld_ref, a_ref, b_ref, y_ref, sout_ref, s_ref):
    c = RWKV_CHUNK
    rows = N_HEADS * c

    @pl.when(pl.program_id(1) == 0)
    def _():
        s_ref[...] = jnp.zeros_like(s_ref)

    ld = ld_ref[...]
    cum = _dot_hi(jnp.where(_iota((c, c), 0) >= _iota((c, c), 1), 1.0, 0.0), ld)
    cum_last = cum[c - 1:c, :]
    g_inv = jnp.exp(-cum)
    g_end = jnp.exp(cum_last - cum)
    own = (_iota((rows, BRANCH_WIDTH), 0) // c) == (_iota((rows, BRANCH_WIDTH), 1) // HEAD_DIM)
    stack = lambda x: jnp.where(own, jnp.concatenate([x] * N_HEADS, axis=0), 0.0)
    tile = lambda x: jnp.concatenate([x] * N_HEADS, axis=0)
    ar = jnp.concatenate([stack(a_ref[...] * jnp.exp(cum - ld)), stack(r_ref[...] * jnp.exp(cum))], axis=0)
    bt = b_ref[...] * g_inv
    kt = k_ref[...] * g_inv
    nt = (((1,), (1,)), ((), ()))
    g_b = lax.dot_general(ar, tile(bt), nt, preferred_element_type=F32, precision=HIGHEST)
    g_k = lax.dot_general(ar, tile(kt), nt, preferred_element_type=F32, precision=HIGHEST)
    step_r = _iota((rows, rows), 0) % c
    step_c = _iota((rows, rows), 1) % c
    same = (_iota((rows, rows), 0) // c) == (_iota((rows, rows), 1) // c)
    strict = same & (step_c < step_r)
    incl = same & (step_c <= step_r)
    a_ab = jnp.where(strict, g_b[0:rows], 0.0)
    a_ak = jnp.where(strict, g_k[0:rows], 0.0)
    a_rb = jnp.where(incl, g_b[rows:], 0.0)
    a_rk = jnp.where(incl, g_k[rows:], 0.0)

    inv = jnp.where(_iota((rows, rows), 0) == _iota((rows, rows), 1), 1.0, 0.0) + a_ab
    power = a_ab
    for _ in range(int(math.log2(c)) - 1):
        power = _dot_hi(power, power)
        inv = inv + _dot_hi(inv, power)

    s = s_ref[...]
    vbd = stack(v_ref[...])
    w = lax.dot_general(ar, s, nt, preferred_element_type=F32, precision=HIGHEST)
    av = _dot_hi(jnp.concatenate([a_ak, a_rk], axis=0), vbd)
    u = _dot_hi(inv, w[0:rows] + av[0:rows])
    y = w[rows:] + av[rows:] + _dot_hi(a_rb, u)
    out = y[0:c]
    for h in range(1, N_HEADS):
        out = out + y[h * c:(h + 1) * c]
    y_ref[...] = out
    upd = lax.dot_general(jnp.concatenate([u, vbd], axis=0),
                          jnp.concatenate([stack(b_ref[...] * g_end), stack(k_ref[...] * g_end)], axis=0),
                          (((0,), (0,)), ((), ())), preferred_element_type=F32, precision=HIGHEST)
    s_ref[...] = s * jnp.exp(cum_last) + upd

    @pl.when(pl.program_id(1) == pl.num_programs(1) - 1)
    def _():
        sout_ref[...] = s_ref[...]


def _rwkv_chunked(r, k, v, ld, a, b):
    bsz, t, _ = r.shape
    blk = pl.BlockSpec((None, RWKV_CHUNK, BRANCH_WIDTH), lambda bi, ci: (bi, ci, 0))
    st = pl.BlockSpec((None, BRANCH_WIDTH, BRANCH_WIDTH), lambda bi, ci: (bi, 0, 0))
    return pl.pallas_call(
        _rwkv_chunk_kernel,
        grid=(bsz, t // RWKV_CHUNK),
        in_specs=[blk] * 6,
        out_specs=[blk, st],
        out_shape=[jax.ShapeDtypeStruct((bsz, t, BRANCH_WIDTH), F32),
                   jax.ShapeDtypeStruct((bsz, BRANCH_WIDTH, BRANCH_WIDTH), F32)],
        scratch_shapes=[pltpu.VMEM((BRANCH_WIDTH, BRANCH_WIDTH), F32)],
        compiler_params=_cparams("parallel", "arbitrary"),
        name="rwkv_chunk",
    )(r, k, v, ld, a, b)


def _rwkv_post_kernel(y_ref, bonus_ref, g_ref, lnw_ref, lnb_ref, o_ref):
    y = y_ref[...]
    head_mean = _head_sum_matrix(1.0 / HEAD_DIM)
    cen = y - _dot_hi(y, head_mean)
    var = _dot_hi(cen * cen, head_mean)
    yn = cen * lax.rsqrt(var + RWKV_LN_EPS) * lnw_ref[...] + lnb_ref[...]
    o_ref[...] = (yn + bonus_ref[...]) * g_ref[...]


def _rwkv_post(y, bonus, g, ln_w, ln_b, tm):
    n = y.shape[0]
    blk = pl.BlockSpec((tm, BRANCH_WIDTH), lambda i: (i, 0))
    return pl.pallas_call(
        _rwkv_post_kernel,
        grid=(n // tm,),
        in_specs=[blk, blk, blk, _const_spec(ln_w.shape), _const_spec(ln_b.shape)],
        out_specs=blk,
        out_shape=jax.ShapeDtypeStruct((n, BRANCH_WIDTH), F32),
        compiler_params=_cparams("parallel"),
        name="rwkv_post",
    )(y, bonus, g, ln_w, ln_b)


def _rwkv_mix(u2, p1, s0, p, ln_w, ln_b, *, b, t, tm, seq_tiles, period):
    r, k, v, d, nkk, kka, g, bonus = _rwkv_prep(u2, p1, p, tm=tm, seq_tiles=seq_tiles, period=period)
    if s0 is None:
        r3 = lambda x: x.reshape(b, t, BRANCH_WIDTH)
        y, s_wide = _rwkv_chunked(r3(r), r3(k), r3(v), r3(d), r3(nkk), r3(kka))
        s_wide = s_wide.reshape(b, N_HEADS, HEAD_DIM, N_HEADS, HEAD_DIM)
        s_fin = jnp.stack([s_wide[:, h, :, h, :] for h in range(N_HEADS)], axis=1)
        o = _rwkv_post(y.reshape(b * t, BRANCH_WIDTH), bonus, g, ln_w, ln_b, tm)
        return o, s_fin.reshape(b * N_HEADS, HEAD_DIM, HEAD_DIM)
    t_rows = max(t, SAMPLE_ROWS)
    t_lanes = -(-t // RWKV_TB) * RWKV_TB

    def chains(a):
        a = a.reshape(b, t, N_HEADS, HEAD_DIM).transpose(0, 2, 1, 3).reshape(b * N_HEADS, t, HEAD_DIM)
        return a

    rows = [jnp.pad(chains(a), ((0, 0), (0, t_rows - t), (0, 0))) for a in (r, k, d, nkk, kka)]
    vt = jnp.pad(chains(v).transpose(0, 2, 1), ((0, 0), (0, 0), (0, t_lanes - t)))
    yt, s_fin = _rwkv_scan(*rows, vt, s0, rows=min(t_rows, RWKV_TB), n_steps=min(t, RWKV_TB))
    y = yt[:, :, :t].transpose(0, 2, 1).reshape(b, N_HEADS, t, HEAD_DIM).transpose(0, 2, 1, 3)
    o = _rwkv_post(y.reshape(b * t, BRANCH_WIDTH), bonus, g, ln_w, ln_b, tm)
    return o, s_fin


def _rwkv_params(mu, w0, w2, a0, a2, g2, k_k, k_a, r_k):
    row = lambda v: v.reshape(1, -1)
    zeros = jnp.zeros((RWKV_W_LORA, BRANCH_WIDTH), F32)
    return dict(mu=row(mu), w0=row(w0), w2=jnp.concatenate([w2, zeros]).astype(BF16), a0=row(a0),
                a2=jnp.concatenate([zeros, a2]).astype(BF16), g2=g2.astype(BF16), k_k=row(k_k), k_a=row(k_a),
                r_k=row(r_k))


def _ssd_params(conv_w, conv_b, dt_bias, a_log, d_skip, norm_g):
    a = a_log.astype(F32)
    pad_row = lambda v: jnp.pad(v, (0, LANES - N_HEADS))[None, :]
    pad_col = lambda v: jnp.pad(v, (0, SSM_HIST - N_HEADS))[:, None]
    return dict(conv_w=conv_w, conv_b=conv_b[None, :], dt_bias=pad_row(dt_bias), dt_bias_t=pad_col(dt_bias),
                a=pad_row(a), a_t=pad_col(a), d_skip=jnp.repeat(d_skip, HEAD_DIM)[None, :], norm_g=norm_g[None, :])


PROMPT_TM = 512
INPROJ_TM = 1024


def _sb_sample_mix(u3, page_table, cache_k, cache_v, layer, n_pool):
    b, t, _ = u3.shape
    q = u3[..., SEG_SB:SEG_SB + BRANCH_WIDTH]
    k = u3[..., SEG_SB + BRANCH_WIDTH:SEG_SB + 2 * BRANCH_WIDTH]
    v = u3[..., SEG_SB + 2 * BRANCH_WIDTH:SEG_SB + 3 * BRANCH_WIDTH]
    qt = jnp.tile(jnp.pad(q, ((0, 0), (0, SAMPLE_ROWS - t), (0, 0))), (1, N_HEADS, 1))
    rows = N_HEADS * SAMPLE_ROWS
    own = (np.arange(rows)[:, None] // SAMPLE_ROWS) == (np.arange(BRANCH_WIDTH)[None, :] // HEAD_DIM)
    qbd = jnp.where(own[None], qt, 0.0)
    page_rows = lambda a: jnp.pad(a, ((0, 0), (0, PAGE_SIZE - t), (0, 0)))
    o = _sb_sample(page_table, qbd, page_rows(k), page_rows(v), cache_k, cache_v, layer, n_pool)
    return o[:, :t]


def kernel(x_prompt, x_sample, cache_nsa_k, cache_nsa_v, cache_sb_k, cache_sb_v, state_win_kv, state_ssm_conv, state_ssm, state_rwkv_shift, state_rwkv, state_ffn_conv, page_table, norm1, norm2, w_in, nsa_q_norm, nsa_k_norm, nsa_cmp_pe, nsa_cmp_w, ssm_conv_w, ssm_conv_b, ssm_dt_bias, ssm_a_log, ssm_d, ssm_norm, rwkv_mu, rwkv_w0, rwkv_w2, rwkv_a0, rwkv_a2, rwkv_g2, rwkv_k_k, rwkv_k_a, rwkv_r_k, rwkv_ln_w, rwkv_ln_b, w_branch, w_out, ffn_up, ffn_conv_w, ffn_conv_b, ffn_down):
    bp, tp, _ = x_prompt.shape
    bs, ts, _ = x_sample.shape
    depth, n_pool = cache_nsa_k.shape[:2]
    n_pages = page_table.shape[1]
    past = n_pages * PAGE_SIZE
    n_buf = state_win_kv.shape[2]
    n_p, n_s = bp * tp, bs * ts
    assert tp % INPROJ_TM == 0 and tp >= NSA_WINDOW and past % NSA_BLOCK == 0 and ts < SAMPLE_ROWS
    assert n_pages % SB_PAGES == 0 and n_pages % NSA_PAGES == 0 and n_s % FFN_HALO == 0

    perm = _in_perm()
    w_in_p = jnp.where(perm >= 0, jnp.take(w_in, np.maximum(perm, 0), axis=2), 0.0).astype(BF16)
    nsa_ck = cache_nsa_k.reshape(depth * n_pool, PAGE_SIZE, LANES)
    nsa_cv = cache_nsa_v.reshape(depth * n_pool, PAGE_SIZE, LANES)
    sb_ck = cache_sb_k.reshape(depth * n_pool, PAGE_SIZE, BRANCH_WIDTH)
    sb_cv = cache_sb_v.reshape(depth * n_pool, PAGE_SIZE, BRANCH_WIDTH)
    dummy = jnp.zeros((FFN_HALO, LANES), F32)
    seg = lambda u, start, width: u[..., start:start + width]

    xp = x_prompt.reshape(n_p, D_MODEL)
    xs = x_sample.reshape(n_s, D_MODEL)
    outs = [[] for _ in range(20)]
    for l in range(depth):
        nsa_p = _nsa_params(nsa_q_norm[l], nsa_k_norm[l], nsa_cmp_pe[l], nsa_cmp_w[l])
        ssd_p = _ssd_params(ssm_conv_w[l], ssm_conv_b[l], ssm_dt_bias[l], ssm_a_log[l], ssm_d[l], ssm_norm[l])
        rwkv_p = _rwkv_params(rwkv_mu[l], rwkv_w0[l], rwkv_w2[l], rwkv_a0[l], rwkv_a2[l], rwkv_g2[l],
                              rwkv_k_k[l], rwkv_k_a[l], rwkv_r_k[l].reshape(-1))
        ln_w, ln_b = rwkv_ln_w[l][None, :], rwkv_ln_b[l][None, :]
        wb, wo = w_branch[l].astype(BF16), w_out[l].astype(BF16)
        wup, wdn = ffn_up[l].astype(BF16), ffn_down[l].astype(BF16)

        up = _inproj(xp, norm1[l][None, :], w_in_p[l], INPROJ_TM)
        u3 = up.reshape(bp, tp, D_IN_PAD)
        r3 = lambda a: a.reshape(bp, tp, -1)
        qs, newk, newv, win, newk_b, win_b, vslc, vwin = _nsa_prep(up, nsa_p["q_gain"], nsa_p["k_gain"], PROMPT_TM)
        kc, vc = _nsa_compress_prompt(r3(newk), r3(newv), nsa_p)
        front = lambda a: jnp.pad(r3(a), ((0, 0), (NSA_WINDOW, 0), (0, 0)))
        o_a = _nsa_prompt(r3(qs), u3, kc, vc, r3(newk_b), r3(vslc), front(win_b), front(vwin))
        dtt = jnp.pad(jnp.swapaxes(seg(u3, SEG_SSM_DT, N_HEADS), 1, 2), ((0, 0), (0, SSM_HIST - N_HEADS), (0, 0)))
        o_b, ssm_fin = _ssd(u3, dtt, jnp.zeros((bp, SSM_HIST, SSM_CONV_DIM), F32),
                            jnp.zeros((bp, SSM_STATE, BRANCH_WIDTH), F32), ssd_p, q=SSM_CHUNK, n_valid=SSM_CHUNK)
        o_c, rwkv_fin = _rwkv_mix(up, dummy, None, rwkv_p, ln_w, ln_b,
                                  b=bp, t=tp, tm=PROMPT_TM, seq_tiles=tp // PROMPT_TM, period=0)
        sb_k = seg(u3, SEG_SB + BRANCH_WIDTH, BRANCH_WIDTH)
        sb_v = seg(u3, SEG_SB + 2 * BRANCH_WIDTH, BRANCH_WIDTH)
        o_d = _sb_prompt(u3, sb_k.astype(BF16), sb_v.astype(BF16))
        flat = lambda a: a.reshape(n_p, BRANCH_WIDTH)
        xp = _merge(xp, [flat(o_a), flat(o_b), o_c, flat(o_d)], up, wb, wo, PROMPT_TM)
        xp, hs = _ffn(xp, dummy, dummy, norm2[l][None, :], wup, ffn_conv_w[l], ffn_conv_b[l][None, :], wdn,
                      tm=PROMPT_TM, seq_tiles=tp // PROMPT_TM, period=0, keep=FFN_HALO)
        prompt_states = (
            r3(newk).reshape(bp, tp, 2, HEAD_DIM), r3(newv).reshape(bp, tp, 2, HEAD_DIM),
            sb_k.reshape(bp, tp, N_HEADS, HEAD_DIM), sb_v.reshape(bp, tp, N_HEADS, HEAD_DIM),
            r3(win)[:, tp - min(NSA_WINDOW, tp):].reshape(bp, -1, 2, HEAD_DIM),
            seg(u3, SEG_SSM_XBC, SSM_CONV_DIM)[:, tp - (SSM_CONV - 1):],
            ssm_fin.reshape(bp, SSM_STATE, N_HEADS, HEAD_DIM).transpose(0, 2, 3, 1),
            seg(u3, SEG_RWKV, RWKV_IN)[:, tp - 1:],
            rwkv_fin.reshape(bp, N_HEADS, HEAD_DIM, HEAD_DIM),
            hs.reshape(bp, tp // PROMPT_TM, FFN_HALO, 2 * D_FF)[:, -1, FFN_HALO - (FFN_CONV - 1):])

        us = _inproj(xs, norm1[l][None, :], w_in_p[l], n_s)
        u3 = us.reshape(bs, ts, D_IN_PAD)
        o_a, newk, newv, win = _nsa_sample_mix(u3, page_table, state_win_kv[l].reshape(bs, n_buf, LANES), nsa_p,
                                               nsa_ck, nsa_cv, l, n_pool)
        t_pad = 2 * SAMPLE_ROWS
        u3_pad = jnp.pad(u3, ((0, 0), (0, t_pad - ts), (0, 0)))
        dtt = jnp.pad(jnp.swapaxes(seg(u3_pad, SEG_SSM_DT, N_HEADS), 1, 2), ((0, 0), (0, SSM_HIST - N_HEADS), (0, 0)))
        hist0 = jnp.pad(state_ssm_conv[l], ((0, 0), (SSM_HIST - (SSM_CONV - 1), 0), (0, 0)))
        h0 = state_ssm[l].transpose(0, 3, 1, 2).reshape(bs, SSM_STATE, BRANCH_WIDTH)
        o_b, ssm_fin = _ssd(u3_pad, dtt, hist0, h0, ssd_p, q=t_pad, n_valid=ts)
        shift_rows = jnp.pad(state_rwkv_shift[l], ((0, 0), (0, ts - 1), (0, 0))).reshape(n_s, RWKV_IN)
        o_c, rwkv_fin = _rwkv_mix(us, shift_rows, state_rwkv[l].reshape(bs * N_HEADS, HEAD_DIM, HEAD_DIM), rwkv_p,
                                  ln_w, ln_b, b=bs, t=ts, tm=n_s, seq_tiles=0, period=ts)
        o_d = _sb_sample_mix(u3, page_table, sb_ck, sb_cv, l, n_pool)
        flat = lambda a: a.reshape(n_s, BRANCH_WIDTH)
        xs = _merge(xs, [flat(o_a), flat(o_b[:, :ts]), o_c, flat(o_d)], us, wb, wo, n_s)
        conv_state = state_ffn_conv[l]
        prev1 = jnp.pad(conv_state[:, 1:2], ((0, 0), (0, ts - 1), (0, 0))).reshape(n_s, 2 * D_FF)
        prev2 = jnp.pad(conv_state, ((0, 0), (0, ts - 2), (0, 0))).reshape(n_s, 2 * D_FF)
        xs, hs = _ffn(xs, prev1, prev2, norm2[l][None, :], wup, ffn_conv_w[l], ffn_conv_b[l][None, :], wdn,
                      tm=n_s, seq_tiles=0, period=ts, keep=n_s)
        tail = lambda old, new, n: jnp.concatenate([old, new], axis=1)[:, -n:]
        sample_states = (
            newk.reshape(bs, ts, 2, HEAD_DIM), newv.reshape(bs, ts, 2, HEAD_DIM),
            seg(u3, SEG_SB + BRANCH_WIDTH, BRANCH_WIDTH).reshape(bs, ts, N_HEADS, HEAD_DIM),
            seg(u3, SEG_SB + 2 * BRANCH_WIDTH, BRANCH_WIDTH).reshape(bs, ts, N_HEADS, HEAD_DIM),
            tail(state_win_kv[l], win.reshape(bs, ts, 2, HEAD_DIM), n_buf),
            tail(state_ssm_conv[l], seg(u3, SEG_SSM_XBC, SSM_CONV_DIM), SSM_CONV - 1),
            ssm_fin.reshape(bs, SSM_STATE, N_HEADS, HEAD_DIM).transpose(0, 2, 3, 1),
            seg(u3, SEG_RWKV, RWKV_IN)[:, ts - 1:],
            rwkv_fin.reshape(bs, N_HEADS, HEAD_DIM, HEAD_DIM),
            tail(conv_state, hs.reshape(bs, ts, 2 * D_FF), FFN_CONV - 1))
        for j in range(10):
            outs[2 * j].append(prompt_states[j])
            outs[2 * j + 1].append(sample_states[j])

    return (xp.reshape(bp, tp, D_MODEL), xs.reshape(bs, ts, D_MODEL)) + tuple(jnp.stack(o) for o in outs)
```

```python
import functools
import math

import numpy as np
import jax
import jax.numpy as jnp
from jax import lax
from jax.experimental import pallas as pl
from jax.experimental.pallas import tpu as pltpu

F32 = jnp.float32
BF16 = jnp.bfloat16
HIGHEST = lax.Precision.HIGHEST

D_MODEL = 1024
N_BRANCH = 4
BRANCH_WIDTH = D_MODEL // N_BRANCH
HEAD_DIM = 64
N_HEADS = BRANCH_WIDTH // HEAD_DIM
Q_BLOCK = 128
PAGE_SIZE = 128
NORM_EPS = 1e-6
NEG = -1e30
NSA_BLOCK = 64
NSA_TOPK = 16
NSA_WINDOW = 512
NSA_FORCED = 2.0 * N_HEADS
SSM_GROUPS = 2
SSM_STATE = 128
SSM_CONV = 4
SSM_CHUNK = 128
SSM_CONV_DIM = BRANCH_WIDTH + 2 * SSM_GROUPS * SSM_STATE
RWKV_W_LORA = 64
RWKV_A_LORA = 64
RWKV_G_LORA = 128
RWKV_IN = 3 * BRANCH_WIDTH + RWKV_W_LORA + RWKV_A_LORA + RWKV_G_LORA
RWKV_LN_EPS = 64e-5
D_FF = 2816
FFN_CONV = 3

LANES = 128
VMEM_LIMIT = 56 * 1024 * 1024

SEG_MERGE = 0
SEG_RWKV = 4096
SEG_NSA_Q = 5120
SEG_SSM_XBC = 5376
SEG_SB = 6144
SEG_NSA_KV = 6912
SEG_NSA_GATE = 7296
SEG_SSM_Z = 7424
SEG_SSM_DT = 7680
D_IN_PAD = 8192


def _in_perm():
    sizes = (BRANCH_WIDTH, 6 * HEAD_DIM, 3 * N_HEADS, BRANCH_WIDTH, SSM_CONV_DIM, N_HEADS, RWKV_IN,
             3 * BRANCH_WIDTH, N_BRANCH * D_MODEL)
    off = np.concatenate([[0], np.cumsum(sizes)])
    o_q, o_kv, o_gate, o_z, o_xbc, o_dt, o_rwkv, o_sb, o_merge = off[:-1]
    perm = -np.ones((D_IN_PAD,), np.int64)
    perm[SEG_MERGE:SEG_MERGE + 4096] = o_merge + np.arange(4096)
    perm[SEG_RWKV:SEG_RWKV + RWKV_IN] = o_rwkv + np.arange(RWKV_IN)
    perm[SEG_NSA_Q:SEG_NSA_Q + 256] = o_q + np.arange(256)
    perm[SEG_SSM_XBC:SEG_SSM_XBC + 768] = o_xbc + np.arange(768)
    perm[SEG_SB:SEG_SB + 768] = o_sb + np.arange(768)
    kv_order = (0, 2, 1, 3, 4, 5)
    for j, src in enumerate(kv_order):
        perm[SEG_NSA_KV + 64 * j:SEG_NSA_KV + 64 * (j + 1)] = o_kv + 64 * src + np.arange(64)
    perm[SEG_NSA_GATE:SEG_NSA_GATE + 12] = o_gate + np.arange(12)
    perm[SEG_SSM_Z:SEG_SSM_Z + 256] = o_z + np.arange(256)
    perm[SEG_SSM_DT:SEG_SSM_DT + 4] = o_dt + np.arange(4)
    return perm


def _cparams(*sem):
    return pltpu.CompilerParams(dimension_semantics=tuple(sem), vmem_limit_bytes=VMEM_LIMIT)


def _const_spec(shape):
    nd = len(shape)
    return pl.BlockSpec(shape, lambda *_: (0,) * nd)


def _iota(shape, dim):
    return lax.broadcasted_iota(jnp.int32, shape, dim)


def _dot(a, b):
    return jnp.dot(a, b, preferred_element_type=F32)


def _dot_hi(a, b):
    return jnp.dot(a, b, preferred_element_type=F32, precision=HIGHEST)


def _dot_nt(a, b):
    return lax.dot_general(a, b, (((1,), (1,)), ((), ())), preferred_element_type=F32)


def _sigmoid(x):
    return 1.0 / (1.0 + jnp.exp(-x))


def _silu(x):
    return x * _sigmoid(x)


def _softplus(x):
    return jnp.maximum(x, 0.0) + jnp.log(1.0 + jnp.exp(-jnp.abs(x)))


def _inproj_kernel(x_ref, g_ref, w_ref, o_ref, xn_ref):
    @pl.when(pl.program_id(1) == 0)
    def _():
        x = x_ref[...]
        ms = jnp.mean(x * x, axis=-1, keepdims=True)
        xn_ref[...] = (x * lax.rsqrt(ms + NORM_EPS) * g_ref[...]).astype(BF16)

    o_ref[...] = _dot(xn_ref[...], w_ref[...])


def _inproj(x, g, w, tm, tn=1024):
    n = x.shape[0]
    return pl.pallas_call(
        _inproj_kernel,
        grid=(n // tm, D_IN_PAD // tn),
        in_specs=[pl.BlockSpec((tm, D_MODEL), lambda i, j: (i, 0)),
                  pl.BlockSpec((1, D_MODEL), lambda i, j: (0, 0)),
                  pl.BlockSpec((D_MODEL, tn), lambda i, j: (0, j))],
        out_specs=pl.BlockSpec((tm, tn), lambda i, j: (i, j)),
        out_shape=jax.ShapeDtypeStruct((n, D_IN_PAD), F32),
        scratch_shapes=[pltpu.VMEM((tm, D_MODEL), BF16)],
        compiler_params=_cparams("parallel", "arbitrary"),
        name="inproj",
    )(x, g, w)


def _merge_kernel(x_ref, oa_ref, ob_ref, oc_ref, od_ref, gate_ref, wb_ref, wo_ref, out_ref):
    acc = None
    for n, o_ref in enumerate((oa_ref, ob_ref, oc_ref, od_ref)):
        proj = _dot(o_ref[...].astype(BF16), wb_ref[n])
        term = _sigmoid(gate_ref[:, n * D_MODEL:(n + 1) * D_MODEL]) * proj
        acc = term if acc is None else acc + term
    out_ref[...] = x_ref[...] + _dot(acc.astype(BF16), wo_ref[...])


def _merge(x, branches, u, wb, wo, tm):
    n = x.shape[0]
    row = lambda i: (i, 0)
    bspec = pl.BlockSpec((tm, BRANCH_WIDTH), row)
    return pl.pallas_call(
        _merge_kernel,
        grid=(n // tm,),
        in_specs=[pl.BlockSpec((tm, D_MODEL), row), bspec, bspec, bspec, bspec,
                  pl.BlockSpec((tm, N_BRANCH * D_MODEL), lambda i: (i, SEG_MERGE // (N_BRANCH * D_MODEL))),
                  _const_spec((N_BRANCH, BRANCH_WIDTH, D_MODEL)),
                  _const_spec((D_MODEL, D_MODEL))],
        out_specs=pl.BlockSpec((tm, D_MODEL), row),
        out_shape=jax.ShapeDtypeStruct((n, D_MODEL), F32),
        compiler_params=_cparams("parallel"),
        name="merge",
    )(x, *branches, u, wb, wo)


FFN_COLS = 256
FFN_HALO = 8


def _ffn_kernel(x_ref, xp_ref, p1_ref, p2_ref, g_ref, wup_ref, cw_ref, cb_ref, wd_ref,
                out_ref, hs_ref, xn_ref, acc_ref, *, seq_tiles, period, keep):
    tm = x_ref.shape[0]
    prompt = seq_tiles > 0
    halo = FFN_HALO if prompt else 0

    def norm(x):
        ms = jnp.mean(x * x, axis=-1, keepdims=True)
        return (x * lax.rsqrt(ms + NORM_EPS) * g_ref[...]).astype(BF16)

    x = x_ref[...]
    xn_ref[halo:halo + tm, :] = norm(x)
    if prompt:
        first = (pl.program_id(0) % seq_tiles) == 0
        xn_ref[0:halo, :] = norm(xp_ref[...])
        hist_ok = jnp.where(first, 0.0, 1.0)
    else:
        step = _iota((tm, 1), 0) % period
    acc_ref[...] = x

    for c in range(D_FF // FFN_COLS):
        gs = slice(c * FFN_COLS, (c + 1) * FFN_COLS)
        us = slice(D_FF + c * FFN_COLS, D_FF + (c + 1) * FFN_COLS)
        xn = xn_ref[...]
        conv = []
        for part, cs in enumerate((gs, us)):
            h = _dot(xn, wup_ref[:, cs])
            if prompt:
                rows = _iota((tm + halo, 1), 0)
                h = jnp.where(rows < halo, h * hist_ok, h)
            h1 = pltpu.roll(h, 1, axis=0)
            h2 = pltpu.roll(h, 2, axis=0)
            if not prompt:
                h1 = jnp.where(step >= 1, h1, p1_ref[:, cs])
                h2 = jnp.where(step >= 2, h2, p2_ref[:, cs])
            y = cw_ref[2:3, cs] * h + cw_ref[1:2, cs] * h1 + cw_ref[0:1, cs] * h2 + cb_ref[:, cs]
            conv.append(y[halo:, :])
            hs_ref[0, :, cs] = h[halo + tm - keep:, :]
        act = (_silu(conv[0]) * conv[1]).astype(BF16)
        acc_ref[...] += _dot(act, wd_ref[gs, :])
    out_ref[...] = acc_ref[...]


def _ffn(x, prev1, prev2, g, wup, cw, cb, wd, *, tm, seq_tiles, period, keep):
    n = x.shape[0]
    nt = n // tm
    prompt = seq_tiles > 0
    halo = FFN_HALO if prompt else 0
    hb = tm // FFN_HALO
    if prompt:
        xp_spec = pl.BlockSpec((FFN_HALO, D_MODEL), lambda i: (jnp.maximum(i * hb - 1, 0), 0))
        xp = x
        p_spec = _const_spec(prev1.shape)
    else:
        xp_spec = _const_spec((FFN_HALO, D_MODEL))
        xp = x
        p_spec = pl.BlockSpec((tm, 2 * D_FF), lambda i: (i, 0))
    single = dict(pipeline_mode=pl.Buffered(1))
    kern = functools.partial(_ffn_kernel, seq_tiles=seq_tiles, period=period, keep=keep)
    return pl.pallas_call(
        kern,
        grid=(nt,),
        in_specs=[pl.BlockSpec((tm, D_MODEL), lambda i: (i, 0)), xp_spec, p_spec, p_spec,
                  _const_spec((1, D_MODEL)),
                  pl.BlockSpec((D_MODEL, 2 * D_FF), lambda i: (0, 0), **single),
                  _const_spec((FFN_CONV, 2 * D_FF)), _const_spec((1, 2 * D_FF)),
                  pl.BlockSpec((D_FF, D_MODEL), lambda i: (0, 0), **single)],
        out_specs=[pl.BlockSpec((tm, D_MODEL), lambda i: (i, 0)),
                   pl.BlockSpec((1, keep, 2 * D_FF), lambda i: (i, 0, 0))],
        out_shape=[jax.ShapeDtypeStruct((n, D_MODEL), F32),
                   jax.ShapeDtypeStruct((nt, keep, 2 * D_FF), F32)],
        scratch_shapes=[pltpu.VMEM((tm + halo, D_MODEL), BF16), pltpu.VMEM((tm, D_MODEL), F32)],
        compiler_params=_cparams("parallel"),
        name="ffn",
    )(x, xp, prev1, prev2, g, wup, cw, cb, wd)


SB_KT = 256
SB_PAGES = 8


def _head_block_mask(rows_per_head, n_rows):
    r = _iota((n_rows, BRANCH_WIDTH), 0) // rows_per_head
    c = _iota((n_rows, BRANCH_WIDTH), 1) // HEAD_DIM
    return r == c


def _suffix_matrix(n):
    return jnp.where(_iota((n, n), 0) > _iota((n, n), 1), 1.0, 0.0).astype(BF16)


def _sb_block(qbd, k, v, mask, carry, tri, token_minor=False):
    z = _dot(qbd, k) if token_minor else _dot_nt(qbd, k)
    sp = _softplus(z)
    l1 = -sp if mask is None else jnp.where(mask, -sp, 0.0)
    hi = l1.astype(BF16)
    lo = (l1 - hi.astype(F32)).astype(BF16)
    after = _dot(hi, tri) + _dot(lo, tri) + carry
    a = jnp.exp(z + l1 + after)
    if mask is not None:
        a = jnp.where(mask, a, 0.0)
    pv = _dot_nt(a.astype(BF16), v) if token_minor else _dot(a.astype(BF16), v)
    return pv, carry + jnp.sum(l1, axis=1, keepdims=True)


SB_DEAD = -104.0


def _sb_alive(carry):
    return (jnp.max(carry) > SB_DEAD).astype(jnp.int32)


def _fold_heads(acc, rows_per_head):
    masked = jnp.where(_head_block_mask(rows_per_head, acc.shape[0]), acc, 0.0)
    out = masked[0:rows_per_head]
    for h in range(1, N_HEADS):
        out = out + masked[h * rows_per_head:(h + 1) * rows_per_head]
    return out


def _sb_prompt_kernel(q_ref, k_ref, v_ref, o_ref, acc_ref, carry_ref):
    i = pl.program_id(1)
    rows = N_HEADS * Q_BLOCK
    q = q_ref[...] * (HEAD_DIM ** -0.5)
    qbd = jnp.where(_head_block_mask(Q_BLOCK, rows), jnp.concatenate([q] * N_HEADS, axis=0), 0.0).astype(BF16)
    tri = _suffix_matrix(SB_KT)
    acc_ref[...] = jnp.zeros_like(acc_ref)
    carry_ref[...] = jnp.zeros_like(carry_ref)
    qpos = i * Q_BLOCK + _iota((rows, 1), 0) % Q_BLOCK
    n_chunks = (i * Q_BLOCK) // SB_KT + 1

    def body(state):
        jj, _ = state
        j = n_chunks - 1 - jj
        start = pl.multiple_of(j * SB_KT, SB_KT)
        kpos = start + _iota((1, SB_KT), 1)
        pv, carry = _sb_block(qbd, k_ref[pl.ds(start, SB_KT), :], v_ref[pl.ds(start, SB_KT), :],
                              kpos < qpos, carry_ref[...], tri)
        acc_ref[...] += pv
        carry_ref[...] = carry
        return jj + 1, _sb_alive(carry)

    lax.while_loop(lambda st: (st[0] < n_chunks) & (st[1] > 0), body, (jnp.int32(0), jnp.int32(1)))
    o_ref[...] = _fold_heads(acc_ref[...], Q_BLOCK)


def _sb_prompt(u3, kb, vb):
    b, t, _ = u3.shape
    rows = N_HEADS * Q_BLOCK
    return pl.pallas_call(
        _sb_prompt_kernel,
        grid=(b, t // Q_BLOCK),
        in_specs=[pl.BlockSpec((None, Q_BLOCK, BRANCH_WIDTH), lambda bi, i: (bi, i, SEG_SB // BRANCH_WIDTH)),
                  pl.BlockSpec((None, t, BRANCH_WIDTH), lambda bi, i: (bi, 0, 0)),
                  pl.BlockSpec((None, t, BRANCH_WIDTH), lambda bi, i: (bi, 0, 0))],
        out_specs=pl.BlockSpec((None, Q_BLOCK, BRANCH_WIDTH), lambda bi, i: (bi, i, 0)),
        out_shape=jax.ShapeDtypeStruct((b, t, BRANCH_WIDTH), F32),
        scratch_shapes=[pltpu.VMEM((rows, BRANCH_WIDTH), F32), pltpu.VMEM((rows, 1), F32)],
        compiler_params=_cparams("parallel", "parallel"),
        name="sb_prompt",
    )(u3, kb, vb)


SAMPLE_ROWS = 8


def _sb_sample_kernel(pt_ref, q_ref, kn_ref, vn_ref, *refs, n_steps):
    k_refs = refs[:SB_PAGES]
    v_refs = refs[SB_PAGES:2 * SB_PAGES]
    o_ref, acc_ref, carry_ref, alive_ref = refs[2 * SB_PAGES:]
    s = pl.program_id(1)
    rows = N_HEADS * SAMPLE_ROWS

    def queries():
        return (q_ref[...] * HEAD_DIM ** -0.5).astype(BF16), _suffix_matrix(PAGE_SIZE)

    @pl.when(s == 0)
    def _():
        qbd, tri = queries()
        step = _iota((rows, 1), 0) % SAMPLE_ROWS
        col = _iota((1, PAGE_SIZE), 1)
        pv, carry = _sb_block(qbd, kn_ref[...].astype(BF16), vn_ref[...].astype(BF16), col < step,
                              jnp.zeros((rows, 1), F32), tri)
        acc_ref[...] = pv
        carry_ref[...] = carry
        alive_ref[0] = _sb_alive(carry)

    for r in range(SB_PAGES):
        @pl.when(alive_ref[0] > 0)
        def _():
            qbd, tri = queries()
            pv, carry = _sb_block(qbd, k_refs[r][...].astype(BF16), v_refs[r][...].astype(BF16), None,
                                  carry_ref[...], tri, token_minor=True)
            acc_ref[...] += pv
            carry_ref[...] = carry
            alive_ref[0] = _sb_alive(carry)

    @pl.when(s == n_steps - 1)
    def _():
        o_ref[...] = _fold_heads(acc_ref[...], SAMPLE_ROWS)


def _sb_sample(page_table, qbd, kn, vn, cache_k, cache_v, layer, n_pool):
    b, n_pages = page_table.shape
    n_steps = n_pages // SB_PAGES
    rows = N_HEADS * SAMPLE_ROWS
    base = layer * n_pool

    def page_spec(r):
        return pl.BlockSpec((None, BRANCH_WIDTH, PAGE_SIZE),
                            lambda bi, s, pt: (base + pt[bi, n_pages - 1 - (s * SB_PAGES + r)], 0, 0))

    per_b = lambda shape: pl.BlockSpec((None,) + shape, lambda bi, s, pt: (bi, 0, 0))
    grid_spec = pltpu.PrefetchScalarGridSpec(
        num_scalar_prefetch=1,
        grid=(b, n_steps),
        in_specs=[per_b((rows, BRANCH_WIDTH)), per_b((PAGE_SIZE, BRANCH_WIDTH)), per_b((PAGE_SIZE, BRANCH_WIDTH))]
                 + [page_spec(r) for r in range(SB_PAGES)] * 2,
        out_specs=per_b((SAMPLE_ROWS, BRANCH_WIDTH)),
        scratch_shapes=[pltpu.VMEM((rows, BRANCH_WIDTH), F32), pltpu.VMEM((rows, 1), F32),
                        pltpu.SMEM((1,), jnp.int32)],
    )
    return pl.pallas_call(
        functools.partial(_sb_sample_kernel, n_steps=n_steps),
        grid_spec=grid_spec,
        out_shape=jax.ShapeDtypeStruct((b, SAMPLE_ROWS, BRANCH_WIDTH), F32),
        compiler_params=_cparams("parallel", "arbitrary"),
        name="sb_sample",
    )(page_table, qbd, kn, vn, *([cache_k] * SB_PAGES), *([cache_v] * SB_PAGES))


NSA_KT = 512
ALIBI_SLOPES = tuple(2.0 ** (-8.0 * (h + 1.0) / N_HEADS) for h in range(N_HEADS))


def _row_slopes(rq):
    h = _iota((N_HEADS * rq, 1), 0) // rq
    out = jnp.full((N_HEADS * rq, 1), ALIBI_SLOPES[0], F32)
    for i in range(1, N_HEADS):
        out = jnp.where(h == i, ALIBI_SLOPES[i], out)
    return out


def _rep_right_matrix():
    r = _iota((LANES, BRANCH_WIDTH), 0)
    c = _iota((LANES, BRANCH_WIDTH), 1)
    return jnp.where(r == HEAD_DIM + c % HEAD_DIM, 1.0, 0.0)


def _nsa_prep_kernel(q_ref, kv_ref, qg_ref, kg_ref, qs_out, newk_out, newv_out, win_out, newkb_out, winb_out,
                     vslc_out, vwin_out):
    q = q_ref[...]
    qn = q * lax.rsqrt(_dot_hi(q * q, _head_sum_matrix(1.0 / HEAD_DIM)) + NORM_EPS) * qg_ref[...]
    qs_out[...] = _dot((qn * HEAD_DIM ** -0.5).astype(BF16), _query_place_matrix()).astype(BF16)
    kv = kv_ref[...]
    w = kv.shape[1]
    same = _iota((w, w), 0) // HEAD_DIM == _iota((w, w), 1) // HEAD_DIM
    ms = _dot_hi(kv * kv, jnp.where(same, 1.0 / HEAD_DIM, 0.0))
    grp = _iota((1, w), 1) // HEAD_DIM
    normed = jnp.where((grp == 1) | (grp == 4), kv * lax.rsqrt(ms + NORM_EPS) * kg_ref[...], kv)
    newk = normed[:, 0:LANES]
    newv = normed[:, LANES:2 * LANES]
    win = normed[:, 2 * LANES:3 * LANES]
    newk_out[...] = newk
    newv_out[...] = newv
    win_out[...] = win
    newkb_out[...] = newk.astype(BF16)
    winb_out[...] = win.astype(BF16)
    rep = _rep_right_matrix().astype(BF16)
    vslc_out[...] = _dot(newv.astype(BF16), rep).astype(BF16)
    vwin_out[...] = _dot(win.astype(BF16), rep).astype(BF16)


def _nsa_prep(u, q_gain, k_gain, tm):
    n = u.shape[0]
    out = lambda w: pl.BlockSpec((tm, w), lambda i: (i, 0))
    shp = lambda w, dt: jax.ShapeDtypeStruct((n, w), dt)
    return pl.pallas_call(
        _nsa_prep_kernel,
        grid=(n // tm,),
        in_specs=[pl.BlockSpec((tm, BRANCH_WIDTH), lambda i: (i, SEG_NSA_Q // BRANCH_WIDTH)),
                  pl.BlockSpec((tm, 3 * LANES), lambda i: (i, SEG_NSA_KV // (3 * LANES))),
                  _const_spec(q_gain.shape), _const_spec(k_gain.shape)],
        out_specs=[out(PLACED_WIDTH), out(LANES), out(LANES), out(LANES), out(LANES), out(LANES),
                   out(BRANCH_WIDTH), out(BRANCH_WIDTH)],
        out_shape=[shp(PLACED_WIDTH, BF16), shp(LANES, F32), shp(LANES, F32), shp(LANES, F32), shp(LANES, BF16),
                   shp(LANES, BF16), shp(BRANCH_WIDTH, BF16), shp(BRANCH_WIDTH, BF16)],
        compiler_params=_cparams("parallel"),
        name="nsa_prep",
    )(u, u, q_gain, k_gain)


def _nsa_compress(k_ref, v_ref, pek_ref, pev_ref, wk_ref, wv_ref, kn_ref, nb):
    def body(m, carry):
        ak, av = carry
        xk = k_ref[pl.ds(m, nb, stride=NSA_BLOCK), :] + pek_ref[pl.ds(m, 1), :]
        xv = v_ref[pl.ds(m, nb, stride=NSA_BLOCK), :] + pev_ref[pl.ds(m, 1), :]
        return ak + _dot(xk.astype(BF16), wk_ref[m]), av + _dot(xv.astype(BF16), wv_ref[m])

    ak, av = lax.fori_loop(0, NSA_BLOCK, body,
                           (jnp.zeros((nb, LANES), F32), jnp.zeros((nb, BRANCH_WIDTH), F32)), unroll=8)
    ms = jnp.sum(ak * ak, axis=1, keepdims=True) * (1.0 / HEAD_DIM)
    kc = ak * lax.rsqrt(ms + NORM_EPS) * kn_ref[...]
    return kc.astype(BF16), av.astype(BF16)


def _nsa_compress_kernel(k_ref, v_ref, pek_ref, pev_ref, wk_ref, wv_ref, kn_ref, kc_out, vc_out):
    kc, vc = _nsa_compress(k_ref, v_ref, pek_ref, pev_ref, wk_ref, wv_ref, kn_ref, kc_out.shape[0])
    kc_out[...] = kc
    vc_out[...] = vc


def _nsa_compress_prompt(newk, newv, p):
    b, t, _ = newk.shape
    nb = t // NSA_BLOCK
    per_b = lambda rows, w: pl.BlockSpec((None, rows, w), lambda bi: (bi, 0, 0))
    consts = [p["pe_k"], p["pe_v"], p["w_k"], p["w_v"], p["k_gain_pad"]]
    return pl.pallas_call(
        _nsa_compress_kernel,
        grid=(b,),
        in_specs=[per_b(t, LANES), per_b(t, LANES)] + [_const_spec(a.shape) for a in consts],
        out_specs=[per_b(nb, LANES), per_b(nb, BRANCH_WIDTH)],
        out_shape=[jax.ShapeDtypeStruct((b, nb, LANES), BF16), jax.ShapeDtypeStruct((b, nb, BRANCH_WIDTH), BF16)],
        compiler_params=_cparams("parallel"),
        name="nsa_compress",
    )(newk, newv, *consts)


def _nsa_cmp_branch(ql, kc, vc, qpos, q0, slopes):
    nb = kc.shape[0]
    bend = (_iota((1, nb), 1) + 1) * NSA_BLOCK - 1
    valid = bend <= qpos
    s = jnp.where(valid, _dot_nt(ql, kc) + slopes * (bend - q0).astype(F32), NEG)
    e = jnp.exp(s - jnp.max(s, axis=1, keepdims=True))
    p = jnp.where(valid, e / jnp.sum(e, axis=1, keepdims=True), 0.0)
    return _dot(p.astype(BF16), vc), p


def _nsa_select(imp, qpos_q, nbl):
    blk = _iota((1, nbl), 1)
    blk_f = blk.astype(F32)
    cur = qpos_q // NSA_BLOCK
    forced = (blk == 0) | (blk == cur) | (blk == cur - 1)
    work = jnp.where(blk > cur, NEG, jnp.where(forced, NSA_FORCED, imp))
    sel = jnp.zeros(work.shape, F32)
    for _ in range(min(NSA_TOPK, nbl)):
        mx = jnp.max(work, axis=1, keepdims=True)
        idx = jnp.min(jnp.where(work == mx, blk_f, float(nbl)), axis=1, keepdims=True)
        hit = blk_f == idx
        sel = jnp.where(hit & (mx > 0.5 * NEG), 1.0, sel)
        work = jnp.where(hit, -3e38, work)
    return sel


def _block_expand_matrix(nbl, start, kt):
    return jnp.where(_iota((nbl, kt), 0) == (start + _iota((nbl, kt), 1)) // NSA_BLOCK, 1.0, 0.0).astype(BF16)


def _gate_expand(gl, j):
    r = _iota((LANES, BRANCH_WIDTH), 0)
    c = _iota((LANES, BRANCH_WIDTH), 1)
    return _sigmoid(_dot_hi(gl, jnp.where(r == (c // HEAD_DIM) * 3 + j, 1.0, 0.0)))


def _softmax_rows(s):
    e = jnp.exp(s - jnp.max(s, axis=1, keepdims=True))
    return e / jnp.sum(e, axis=1, keepdims=True)


PLACED_WIDTH = 2 * N_HEADS * LANES


def _query_place_matrix():
    r = _iota((BRANCH_WIDTH, PLACED_WIDTH), 0)
    j = _iota((BRANCH_WIDTH, PLACED_WIDTH), 1)
    side = j // (N_HEADS * LANES)
    head = (j // LANES) % N_HEADS
    c = j % LANES
    left = (side == 0) & (c < HEAD_DIM) & (r == head * HEAD_DIM + c)
    right = (side == 1) & (c >= HEAD_DIM) & (r == head * HEAD_DIM + c - HEAD_DIM)
    return jnp.where(left | right, 1.0, 0.0).astype(BF16)


def _placed_rows(qs):
    ql = jnp.concatenate([qs[:, h * LANES:(h + 1) * LANES] for h in range(N_HEADS)], axis=0)
    qr = jnp.concatenate([qs[:, (N_HEADS + h) * LANES:(N_HEADS + h + 1) * LANES] for h in range(N_HEADS)], axis=0)
    return ql, qr


NSA_SELECT_TQ = 1024
CHUNK_BLOCKS = NSA_KT // NSA_BLOCK


def _nsa_select_kernel(qs_ref, kc_ref, vc_ref, oc_ref, sel_ref, hit_ref):
    tq = qs_ref.shape[0]
    rows = N_HEADS * tq
    q0 = pl.program_id(1) * tq
    nbl = kc_ref.shape[0]
    ql, _ = _placed_rows(qs_ref[...])
    qpos = q0 + _iota((rows, 1), 0) % tq
    o_c, p_c = _nsa_cmp_branch(ql, kc_ref[...], vc_ref[...], qpos, q0, _row_slopes(tq))
    oc_ref[...] = _fold_heads(o_c, tq)
    imp = p_c[0:tq]
    for h in range(1, N_HEADS):
        imp = imp + p_c[h * tq:(h + 1) * tq]
    sel = _nsa_select(imp, q0 + _iota((tq, 1), 0), nbl)
    sel_ref[...] = sel.astype(BF16)
    union = jnp.max(sel.reshape(tq // Q_BLOCK, Q_BLOCK, nbl), axis=1)
    chunk_of = jnp.where(_iota((nbl, LANES), 0) // CHUNK_BLOCKS == _iota((nbl, LANES), 1), 1.0, 0.0)
    hit_ref[...] = _dot(union.astype(BF16), chunk_of.astype(BF16))


def _nsa_select_prompt(qs, kc, vc):
    b, t, _ = qs.shape
    nb = kc.shape[1]
    tq = min(NSA_SELECT_TQ, t)
    per_b = lambda r, w: pl.BlockSpec((None, r, w), lambda bi, i: (bi, 0, 0))
    tile = lambda r, w: pl.BlockSpec((None, r, w), lambda bi, i: (bi, i, 0))
    return pl.pallas_call(
        _nsa_select_kernel,
        grid=(b, t // tq),
        in_specs=[tile(tq, PLACED_WIDTH), per_b(nb, LANES), per_b(nb, BRANCH_WIDTH)],
        out_specs=[tile(tq, BRANCH_WIDTH), tile(tq, nb), tile(tq // Q_BLOCK, LANES)],
        out_shape=[jax.ShapeDtypeStruct((b, t, BRANCH_WIDTH), F32), jax.ShapeDtypeStruct((b, t, nb), BF16),
                   jax.ShapeDtypeStruct((b, t // Q_BLOCK, LANES), F32)],
        compiler_params=_cparams("parallel", "parallel"),
        name="nsa_select",
    )(qs, kc, vc)


def _nsa_prompt_kernel(hit_ref, qs_ref, gate_ref, sel_ref, oc_ref, kb_ref, vs_ref, wb_ref, vw_ref, o_ref,
                       m_ref, l_ref, acc_ref, *, n_chunks_all):
    i = pl.program_id(1)
    rq = Q_BLOCK
    rows = N_HEADS * rq
    q0 = i * rq
    nbl = sel_ref.shape[1]
    ql, qr = _placed_rows(qs_ref[...])
    slopes = _row_slopes(rq)
    qpos = q0 + _iota((rows, 1), 0) % rq
    qpos_q = q0 + _iota((rq, 1), 0)
    sel = sel_ref[...]
    hit_base = (pl.program_id(0) * pl.num_programs(1) + i) * n_chunks_all

    span = NSA_WINDOW + rq
    wstart = pl.multiple_of(q0, rq)
    kpos_w = q0 - NSA_WINDOW + _iota((1, span), 1)
    dist = qpos - kpos_w
    valid_w = (dist >= 0) & (dist < NSA_WINDOW) & (kpos_w >= 0)
    s_w = _dot_nt(ql, wb_ref[pl.ds(wstart, span), :]) + slopes * (kpos_w - q0).astype(F32)
    o_w = _dot(_softmax_rows(jnp.where(valid_w, s_w, NEG)).astype(BF16), vw_ref[pl.ds(wstart, span), :])
    gl = gate_ref[...]
    o_ref[...] = _gate_expand(gl, 0) * oc_ref[...] + _gate_expand(gl, 2) * _fold_heads(o_w, rq)

    m_ref[...] = jnp.full(m_ref.shape, NEG, F32)
    l_ref[...] = jnp.zeros_like(l_ref)
    acc_ref[...] = jnp.zeros_like(acc_ref)
    n_chunks = (q0 + rq + NSA_KT - 1) // NSA_KT

    def body(j, _):
        @pl.when(hit_ref[hit_base + j] > 0)
        def _():
            start = pl.multiple_of(j * NSA_KT, NSA_KT)
            chosen = _dot(sel, _block_expand_matrix(nbl, start, NSA_KT))
            s = _dot_nt(qr, kb_ref[pl.ds(start, NSA_KT), :])
            v = vs_ref[pl.ds(start, NSA_KT), :]
            kpos = start + _iota((1, NSA_KT), 1)
            ok = (chosen > 0.5) & (kpos <= qpos_q)
            rel = (kpos - q0).astype(F32)
            for h in range(N_HEADS):
                rs = slice(h * rq, (h + 1) * rq)
                s_h = jnp.where(ok, s[rs] + ALIBI_SLOPES[h] * rel, NEG)
                m_old = m_ref[rs]
                m_new = jnp.maximum(m_old, jnp.max(s_h, axis=1, keepdims=True))
                alpha = jnp.exp(m_old - m_new)
                p = jnp.where(ok, jnp.exp(s_h - m_new), 0.0)
                l_ref[rs] = alpha * l_ref[rs] + jnp.sum(p, axis=1, keepdims=True)
                acc_ref[rs] = alpha * acc_ref[rs] + _dot(p.astype(BF16), v)
                m_ref[rs] = m_new

        return 0

    lax.fori_loop(0, n_chunks, body, 0)
    o_ref[...] += _gate_expand(gl, 1) * _fold_heads(acc_ref[...] / l_ref[...], rq)


def _nsa_prompt(qs, u3, kc, vc, newk_b, vslc, win_b_pad, vwin_pad):
    b, t, _ = qs.shape
    nb = kc.shape[1]
    rows = N_HEADS * Q_BLOCK
    n_chunks_all = -(-t // NSA_KT)
    o_c, sel, hits = _nsa_select_prompt(qs, kc, vc)
    hit_flags = (hits[..., :n_chunks_all] > 0.5).astype(jnp.int32).reshape(-1)
    per_b = lambda r, w: pl.BlockSpec((None, r, w), lambda bi, i, hf: (bi, 0, 0))
    tile = lambda w, col=0: pl.BlockSpec((None, Q_BLOCK, w), lambda bi, i, hf: (bi, i, col))
    grid_spec = pltpu.PrefetchScalarGridSpec(
        num_scalar_prefetch=1,
        grid=(b, t // Q_BLOCK),
        in_specs=[tile(PLACED_WIDTH), tile(LANES, SEG_NSA_GATE // LANES), tile(nb), tile(BRANCH_WIDTH),
                  per_b(t, LANES), per_b(t, BRANCH_WIDTH),
                  per_b(t + NSA_WINDOW, LANES), per_b(t + NSA_WINDOW, BRANCH_WIDTH)],
        out_specs=tile(BRANCH_WIDTH),
        scratch_shapes=[pltpu.VMEM((rows, 1), F32), pltpu.VMEM((rows, 1), F32), pltpu.VMEM((rows, BRANCH_WIDTH), F32)],
    )
    return pl.pallas_call(
        functools.partial(_nsa_prompt_kernel, n_chunks_all=n_chunks_all),
        grid_spec=grid_spec,
        out_shape=jax.ShapeDtypeStruct((b, t, BRANCH_WIDTH), F32),
        compiler_params=_cparams("parallel", "parallel"),
        name="nsa_prompt",
    )(hit_flags, qs, u3, sel, o_c, newk_b, vslc, win_b_pad, vwin_pad)


NSA_PAGES = 8


def _nsa_sample_kernel(pt_ref, ql_ref, qr_ref, gate_ref, kn_ref, vn_ref, wbuf_ref, wn_ref, vwn_ref,
                       pek_ref, pev_ref, wk_ref, wv_ref, kg_ref, *refs, n_steps, n_new):
    k_pages = refs[:NSA_PAGES]
    v_pages = refs[NSA_PAGES:2 * NSA_PAGES]
    o_ref, kbuf_ref, vbuf_ref = refs[2 * NSA_PAGES:]
    s = pl.program_id(1)
    for r in range(NSA_PAGES):
        row0 = pl.multiple_of((s * NSA_PAGES + r) * PAGE_SIZE, PAGE_SIZE)
        kbuf_ref[pl.ds(row0, PAGE_SIZE), :] = k_pages[r][...]
        vbuf_ref[pl.ds(row0, PAGE_SIZE), :] = v_pages[r][...]

    @pl.when(s == n_steps - 1)
    def _():
        rq = SAMPLE_ROWS
        rows = N_HEADS * rq
        past = kbuf_ref.shape[0]
        nb = past // NSA_BLOCK
        n_buf = wbuf_ref.shape[0]
        ql = ql_ref[...]
        qr = qr_ref[...]
        slopes = _row_slopes(rq)
        step = _iota((rows, 1), 0) % rq
        qpos = past + step
        step_q = _iota((rq, 1), 0)
        rep = _rep_right_matrix()
        tile = lambda a: jnp.concatenate([a] * N_HEADS, axis=0)

        kc, vc = _nsa_compress(kbuf_ref, vbuf_ref, pek_ref, pev_ref, wk_ref, wv_ref, kg_ref, nb)
        o_c, p_c = _nsa_cmp_branch(ql, kc, vc, qpos, past, slopes)
        imp = p_c[0:rq]
        for h in range(1, N_HEADS):
            imp = imp + p_c[h * rq:(h + 1) * rq]
        width = -(-(nb + 1) // LANES) * LANES
        imp = jnp.concatenate([imp, jnp.zeros((rq, width - nb), F32)], axis=1)
        sel = _nsa_select(imp, past + step_q, width)

        col = _iota((1, PAGE_SIZE), 1)
        ok_p = tile(_dot(sel[:, 0:nb].astype(BF16), _block_expand_matrix(nb, 0, past)) > 0.5)
        s_p = _dot_nt(qr, kbuf_ref[...].astype(BF16)) + slopes * (_iota((1, past), 1) - past).astype(F32)
        s_p = jnp.where(ok_p, s_p, NEG)
        ok_n = tile(sel[:, nb:nb + 1] > 0.5) & (col <= step) & (col < n_new)
        s_n = jnp.where(ok_n, _dot_nt(qr, kn_ref[...]) + slopes * col.astype(F32), NEG)
        m = jnp.maximum(jnp.max(s_p, axis=1, keepdims=True), jnp.max(s_n, axis=1, keepdims=True))
        p_p = jnp.where(ok_p, jnp.exp(s_p - m), 0.0)
        p_n = jnp.where(ok_n, jnp.exp(s_n - m), 0.0)
        den = jnp.sum(p_p, axis=1, keepdims=True) + jnp.sum(p_n, axis=1, keepdims=True)
        o_s = (_dot_hi(_dot(p_p.astype(BF16), vbuf_ref[...].astype(BF16)), rep)
               + _dot(p_n.astype(BF16), vn_ref[...])) / den

        wb = wbuf_ref[...].astype(BF16)
        cw = _iota((1, n_buf), 1)
        kpos_w = past - n_buf + cw
        dist = qpos - kpos_w
        ok_w = (dist >= 0) & (dist < NSA_WINDOW) & (kpos_w >= 0)
        s_w = jnp.where(ok_w, _dot_nt(ql, wb) + slopes * (cw - n_buf).astype(F32), NEG)
        ok_wn = (col <= step) & (col < n_new)
        s_wn = jnp.where(ok_wn, _dot_nt(ql, wn_ref[...]) + slopes * col.astype(F32), NEG)
        m = jnp.maximum(jnp.max(s_w, axis=1, keepdims=True), jnp.max(s_wn, axis=1, keepdims=True))
        p_w = jnp.where(ok_w, jnp.exp(s_w - m), 0.0)
        p_wn = jnp.where(ok_wn, jnp.exp(s_wn - m), 0.0)
        den = jnp.sum(p_w, axis=1, keepdims=True) + jnp.sum(p_wn, axis=1, keepdims=True)
        o_w = (_dot_hi(_dot(p_w.astype(BF16), wb), rep) + _dot(p_wn.astype(BF16), vwn_ref[...])) / den

        gl = gate_ref[...]
        o_ref[...] = (_gate_expand(gl, 0) * _fold_heads(o_c, rq) + _gate_expand(gl, 1) * _fold_heads(o_s, rq)
                      + _gate_expand(gl, 2) * _fold_heads(o_w, rq))


def _nsa_sample(page_table, ql, qr, gate, kn, vn, wbuf, wn, vwn, p, cache_k, cache_v, layer, n_pool, n_new):
    b, n_pages = page_table.shape
    n_steps = n_pages // NSA_PAGES
    past = n_pages * PAGE_SIZE
    base = layer * n_pool

    def page_spec(r):
        return pl.BlockSpec((None, PAGE_SIZE, LANES), lambda bi, s, pt: (base + pt[bi, s * NSA_PAGES + r], 0, 0))

    per_b = lambda a: pl.BlockSpec((None,) + a.shape[1:], lambda bi, s, pt: (bi, 0, 0))
    const = lambda a: pl.BlockSpec(a.shape, lambda bi, s, pt: (0,) * a.ndim)
    consts = [p["pe_k"], p["pe_v"], p["w_k"], p["w_v"], p["k_gain_pad"]]
    seq_ops = [ql, qr, gate, kn, vn, wbuf, wn, vwn]
    grid_spec = pltpu.PrefetchScalarGridSpec(
        num_scalar_prefetch=1,
        grid=(b, n_steps),
        in_specs=[per_b(a) for a in seq_ops] + [const(a) for a in consts]
                 + [page_spec(r) for r in range(NSA_PAGES)] * 2,
        out_specs=pl.BlockSpec((None, SAMPLE_ROWS, BRANCH_WIDTH), lambda bi, s, pt: (bi, 0, 0)),
        scratch_shapes=[pltpu.VMEM((past, LANES), F32), pltpu.VMEM((past, LANES), F32)],
    )
    return pl.pallas_call(
        functools.partial(_nsa_sample_kernel, n_steps=n_steps, n_new=n_new),
        grid_spec=grid_spec,
        out_shape=jax.ShapeDtypeStruct((b, SAMPLE_ROWS, BRANCH_WIDTH), F32),
        compiler_params=_cparams("parallel", "arbitrary"),
        name="nsa_sample",
    )(page_table, *seq_ops, *consts, *([cache_k] * NSA_PAGES), *([cache_v] * NSA_PAGES))


def _nsa_sample_mix(u3, page_table, wbuf, p, cache_k, cache_v, layer, n_pool):
    b, t, _ = u3.shape
    qs, newk, newv, win, newk_b, win_b, vslc, vwin = _nsa_prep(u3.reshape(b * t, -1), p["q_gain"], p["k_gain"], b * t)
    q4 = qs.reshape(b, t, 2, N_HEADS, LANES).transpose(2, 0, 3, 1, 4)
    q4 = jnp.pad(q4, ((0, 0), (0, 0), (0, 0), (0, SAMPLE_ROWS - t), (0, 0)))
    ql, qr = q4.reshape(2, b, N_HEADS * SAMPLE_ROWS, LANES)
    gate = jnp.pad(u3[..., SEG_NSA_GATE:SEG_NSA_GATE + LANES], ((0, 0), (0, SAMPLE_ROWS - t), (0, 0)))
    page_rows = lambda a: jnp.pad(a.reshape(b, t, -1), ((0, 0), (0, PAGE_SIZE - t), (0, 0)))
    o = _nsa_sample(page_table, ql, qr, gate, page_rows(newk_b), page_rows(vslc), wbuf, page_rows(win_b),
                    page_rows(vwin), p, cache_k, cache_v, layer, n_pool, t)
    return o[:, :t], newk, newv, win


def _nsa_params(q_norm, k_norm, cmp_pe, cmp_w):
    lane_pad = lambda a: jnp.pad(a, ((0, 0),) * (a.ndim - 1) + ((0, LANES - HEAD_DIM),))
    wk = cmp_w[0].reshape(NSA_BLOCK, HEAD_DIM, HEAD_DIM)
    wv = cmp_w[1].reshape(NSA_BLOCK, HEAD_DIM, HEAD_DIM)
    pad_rows = lambda a: jnp.pad(a, ((0, 0), (0, LANES - HEAD_DIM), (0, 0)))
    return dict(q_gain=jnp.tile(q_norm, N_HEADS)[None, :], k_gain=jnp.tile(k_norm, 6)[None, :],
                k_gain_pad=lane_pad(k_norm[None, :]), pe_k=lane_pad(cmp_pe[0]), pe_v=lane_pad(cmp_pe[1]),
                w_k=pad_rows(lane_pad(wk)).astype(BF16), w_v=pad_rows(jnp.tile(wv, (1, 1, N_HEADS))).astype(BF16))


SSM_HIST = 8
GROUP_LANES = BRANCH_WIDTH // SSM_GROUPS


def _ssd_kernel(xbc_ref, z_ref, dt_ref, dtt_ref, hist0_ref, h0_ref, cw_ref, cb_ref, bias_ref, biast_ref,
                a_ref, at_ref, dskip_ref, ng_ref, y_ref, hout_ref, hist_ref, state_ref, *, n_valid):
    c = pl.program_id(1)
    q = xbc_ref.shape[0]

    @pl.when(c == 0)
    def _():
        hist_ref[0:SSM_HIST, :] = hist0_ref[...]
        state_ref[...] = h0_ref[...]

    hist_ref[SSM_HIST:SSM_HIST + q, :] = xbc_ref[...]
    full = hist_ref[...]
    conv = cw_ref[3:4, :] * full + cb_ref[...]
    for k in range(1, SSM_CONV):
        conv = conv + cw_ref[3 - k:4 - k, :] * pltpu.roll(full, k, axis=0)
    hist_ref[0:SSM_HIST, :] = full[q:q + SSM_HIST, :]
    act = _silu(conv[SSM_HIST:, :])
    xs = act[:, 0:BRANCH_WIDTH]
    xs_b = xs.astype(BF16)
    bm = act[:, BRANCH_WIDTH:BRANCH_WIDTH + SSM_GROUPS * SSM_STATE].astype(BF16)
    cm = act[:, BRANCH_WIDTH + SSM_GROUPS * SSM_STATE:].astype(BF16)

    dt = jnp.where(_iota((q, LANES), 0) < n_valid, _softplus(dt_ref[...] + bias_ref[...]), 0.0)
    dta = dt * -jnp.exp(a_ref[...])
    dtt = jnp.where(_iota((SSM_HIST, q), 1) < n_valid, _softplus(dtt_ref[...] + biast_ref[...]), 0.0)
    dtat = dtt * -jnp.exp(at_ref[...])
    tril = jnp.where(_iota((q, q), 0) >= _iota((q, q), 1), 1.0, 0.0)
    triu = jnp.where(_iota((q, q), 0) <= _iota((q, q), 1), 1.0, 0.0)
    expand = jnp.where(_iota((LANES, BRANCH_WIDTH), 0) == _iota((LANES, BRANCH_WIDTH), 1) // HEAD_DIM, 1.0, 0.0)
    dt_e = _dot_hi(dt, expand)
    cum_e = _dot_hi(tril, _dot_hi(dta, expand))
    cum_t = _dot_hi(dtat, triu)
    causal = _iota((q, q), 0) >= _iota((q, q), 1)
    lane_head = _iota((1, BRANCH_WIDTH), 1) // HEAD_DIM

    y = dskip_ref[...] * xs
    for g in range(SSM_GROUPS):
        sl = slice(g * SSM_STATE, (g + 1) * SSM_STATE)
        gram = _dot_nt(cm[:, sl], bm[:, sl])
        for h in range(g * (N_HEADS // SSM_GROUPS), (g + 1) * (N_HEADS // SSM_GROUPS)):
            pick = jnp.where(_iota((LANES, q), 0) == h, 1.0, 0.0)
            cum_l = _dot_hi(tril, _dot_hi(dta, pick))
            seg = cum_l - cum_t[h:h + 1, :]
            decay = jnp.where(causal, jnp.exp(jnp.minimum(seg, 0.0)), 0.0)
            scores = (gram * decay * dtt[h:h + 1, :]).astype(BF16)
            y = y + jnp.where(lane_head == h, _dot(scores, xs_b), 0.0)
    state = state_ref[...]
    inter = jnp.concatenate(
        [_dot(cm[:, g * SSM_STATE:(g + 1) * SSM_STATE],
              state[:, g * GROUP_LANES:(g + 1) * GROUP_LANES].astype(BF16)) for g in range(SSM_GROUPS)], axis=1)
    y = y + jnp.exp(cum_e) * inter

    cum_last = cum_e[q - 1:q, :]
    xw = (xs * jnp.exp(cum_last - cum_e) * dt_e).astype(BF16)
    contrib = jnp.concatenate(
        [lax.dot_general(bm[:, g * SSM_STATE:(g + 1) * SSM_STATE], xw[:, g * GROUP_LANES:(g + 1) * GROUP_LANES],
                         (((0,), (0,)), ((), ())), preferred_element_type=F32) for g in range(SSM_GROUPS)], axis=1)
    state_ref[...] = state * jnp.exp(cum_last) + contrib

    y = y * _silu(z_ref[...])
    parts = []
    for g in range(SSM_GROUPS):
        yg = y[:, g * GROUP_LANES:(g + 1) * GROUP_LANES]
        parts.append(yg * lax.rsqrt(jnp.mean(yg * yg, axis=-1, keepdims=True) + NORM_EPS))
    y_ref[...] = jnp.concatenate(parts, axis=1) * ng_ref[...]

    @pl.when(c == pl.num_programs(1) - 1)
    def _():
        hout_ref[...] = state_ref[...]


def _ssd(u3, dtt, hist0, h0, p, *, q, n_valid):
    b, t, _ = u3.shape
    per_b = lambda shape: pl.BlockSpec((None,) + shape, lambda bi, c: (bi, 0, 0))
    col = lambda width, seg: pl.BlockSpec((None, q, width), lambda bi, c: (bi, c, seg // width))
    consts = [p["conv_w"], p["conv_b"], p["dt_bias"], p["dt_bias_t"], p["a"], p["a_t"], p["d_skip"], p["norm_g"]]
    return pl.pallas_call(
        functools.partial(_ssd_kernel, n_valid=n_valid),
        grid=(b, t // q),
        in_specs=[col(SSM_CONV_DIM, SEG_SSM_XBC), col(BRANCH_WIDTH, SEG_SSM_Z), col(LANES, SEG_SSM_DT),
                  pl.BlockSpec((None, SSM_HIST, q), lambda bi, c: (bi, 0, c)),
                  per_b((SSM_HIST, SSM_CONV_DIM)), per_b((SSM_STATE, BRANCH_WIDTH))]
                 + [_const_spec(a.shape) for a in consts],
        out_specs=[pl.BlockSpec((None, q, BRANCH_WIDTH), lambda bi, c: (bi, c, 0)),
                   per_b((SSM_STATE, BRANCH_WIDTH))],
        out_shape=[jax.ShapeDtypeStruct((b, t, BRANCH_WIDTH), F32),
                   jax.ShapeDtypeStruct((b, SSM_STATE, BRANCH_WIDTH), F32)],
        scratch_shapes=[pltpu.VMEM((SSM_HIST + q, SSM_CONV_DIM), F32), pltpu.VMEM((SSM_STATE, BRANCH_WIDTH), F32)],
        compiler_params=_cparams("parallel", "arbitrary"),
        name="ssd",
    )(u3, u3, u3, dtt, hist0, h0, *consts)


def _head_sum_matrix(scale):
    same = _iota((BRANCH_WIDTH, BRANCH_WIDTH), 0) // HEAD_DIM == _iota((BRANCH_WIDTH, BRANCH_WIDTH), 1) // HEAD_DIM
    return jnp.where(same, scale, 0.0)


def _rwkv_prep_kernel(u_ref, up_ref, p1_ref, mu_ref, w0_ref, w2_ref, a0_ref, a2_ref, g2_ref, kk_ref, ka_ref, rk_ref,
                      r_out, k_out, v_out, d_out, nkk_out, kka_out, g_out, bonus_out, *, seq_tiles, period):
    tm = u_ref.shape[0]
    u = u_ref[...]
    rolled = pltpu.roll(u, 1, axis=0)
    if seq_tiles > 0:
        first = (pl.program_id(0) % seq_tiles) == 0
        carry_in = up_ref[FFN_HALO - 1:FFN_HALO, :] * jnp.where(first, 0.0, 1.0)
        prev = jnp.where(_iota((tm, 1), 0) == 0, carry_in, rolled)
    else:
        prev = jnp.where(_iota((tm, 1), 0) % period >= 1, rolled, p1_ref[...])
    us = u + (prev - u) * mu_ref[...]
    r = us[:, 0:BRANCH_WIDTH]
    k = us[:, BRANCH_WIDTH:2 * BRANCH_WIDTH]
    v = us[:, 2 * BRANCH_WIDTH:3 * BRANCH_WIDTH]
    wa = us[:, 3 * BRANCH_WIDTH:3 * BRANCH_WIDTH + LANES]
    gd = us[:, 3 * BRANCH_WIDTH + LANES:]
    is_w = _iota((1, LANES), 1) < RWKV_W_LORA
    w_lora = _dot(jnp.where(is_w, jnp.tanh(wa), 0.0).astype(BF16), w2_ref[...])
    a_lora = _dot(jnp.where(is_w, 0.0, wa).astype(BF16), a2_ref[...])
    w_raw = -_softplus(-(w0_ref[...] + w_lora)) - 0.5
    log_decay = -jnp.exp(w_raw)
    a = _sigmoid(a0_ref[...] + a_lora)
    g = _dot(_sigmoid(gd).astype(BF16), g2_ref[...])
    head_sum = _head_sum_matrix(1.0)
    kk = k * kk_ref[...]
    kk = kk / jnp.maximum(jnp.sqrt(_dot_hi(kk * kk, head_sum)), 1e-12)
    k_mod = k * (1.0 + (a - 1.0) * ka_ref[...])
    r_out[...] = r
    k_out[...] = k_mod
    v_out[...] = v
    d_out[...] = log_decay
    nkk_out[...] = -kk
    kka_out[...] = kk * a
    g_out[...] = g
    bonus_out[...] = _dot_hi(r * k_mod * rk_ref[...], head_sum) * v


def _rwkv_prep(u, p1, p, *, tm, seq_tiles, period):
    n = u.shape[0]
    hb = tm // FFN_HALO
    seg = SEG_RWKV // RWKV_IN
    if seq_tiles > 0:
        up_spec = pl.BlockSpec((FFN_HALO, RWKV_IN), lambda i: (jnp.maximum(i * hb - 1, 0), seg))
        p_spec = _const_spec(p1.shape)
    else:
        up_spec = pl.BlockSpec((FFN_HALO, RWKV_IN), lambda i: (0, seg))
        p_spec = pl.BlockSpec((tm, RWKV_IN), lambda i: (i, 0))
    consts = [p["mu"], p["w0"], p["w2"], p["a0"], p["a2"], p["g2"], p["k_k"], p["k_a"], p["r_k"]]
    out = pl.BlockSpec((tm, BRANCH_WIDTH), lambda i: (i, 0))
    return pl.pallas_call(
        functools.partial(_rwkv_prep_kernel, seq_tiles=seq_tiles, period=period),
        grid=(n // tm,),
        in_specs=[pl.BlockSpec((tm, RWKV_IN), lambda i: (i, seg)), up_spec, p_spec]
                 + [_const_spec(a.shape) for a in consts],
        out_specs=[out] * 8,
        out_shape=[jax.ShapeDtypeStruct((n, BRANCH_WIDTH), F32)] * 8,
        compiler_params=_cparams("parallel"),
        name="rwkv_prep",
    )(u, u, p1, *consts)


RWKV_CHAINS = 8
RWKV_TB = 128


def _rwkv_scan_kernel(r_ref, k_ref, d_ref, nkk_ref, kka_ref, vt_ref, s0_ref, yt_ref, sout_ref, s_ref, *, n_steps):
    @pl.when(pl.program_id(1) == 0)
    def _():
        s_ref[...] = s0_ref[...]

    yt_ref[...] = jnp.zeros_like(yt_ref)
    lane = _iota((HEAD_DIM, RWKV_TB), 1)

    def step(t, _):
        for c in range(RWKV_CHAINS):
            row = lambda ref: ref[c, pl.ds(t, 1), :]
            s = s_ref[c]
            sa = jnp.sum(s * row(nkk_ref), axis=1, keepdims=True)
            v_col = jnp.sum(jnp.where(lane == t, vt_ref[c], 0.0), axis=1, keepdims=True)
            s = s * jnp.exp(row(d_ref)) + sa * row(kka_ref) + v_col * row(k_ref)
            s_ref[c] = s
            y_col = jnp.sum(s * row(r_ref), axis=1, keepdims=True)
            yt_ref[c] = jnp.where(lane == t, y_col, yt_ref[c])
        return 0

    lax.fori_loop(0, n_steps, step, 0)

    @pl.when(pl.program_id(1) == pl.num_programs(1) - 1)
    def _():
        sout_ref[...] = s_ref[...]


def _rwkv_scan(r, k, d, nkk, kka, vt, s0, *, rows, n_steps):
    chains, t_rows, _ = r.shape
    t_lanes = vt.shape[2]
    row_spec = pl.BlockSpec((RWKV_CHAINS, rows, HEAD_DIM), lambda ci, tb: (ci, tb, 0))
    lane_spec = pl.BlockSpec((RWKV_CHAINS, HEAD_DIM, RWKV_TB), lambda ci, tb: (ci, 0, tb))
    state_spec = pl.BlockSpec((RWKV_CHAINS, HEAD_DIM, HEAD_DIM), lambda ci, tb: (ci, 0, 0))
    return pl.pallas_call(
        functools.partial(_rwkv_scan_kernel, n_steps=n_steps),
        grid=(chains // RWKV_CHAINS, t_lanes // RWKV_TB),
        in_specs=[row_spec] * 5 + [lane_spec, state_spec],
        out_specs=[lane_spec, state_spec],
        out_shape=[jax.ShapeDtypeStruct((chains, HEAD_DIM, t_lanes), F32),
                   jax.ShapeDtypeStruct((chains, HEAD_DIM, HEAD_DIM), F32)],
        scratch_shapes=[pltpu.VMEM((RWKV_CHAINS, HEAD_DIM, HEAD_DIM), F32)],
        compiler_params=_cparams("parallel", "arbitrary"),
        name="rwkv_scan",
    )(r, k, d, nkk, kka, vt, s0)


RWKV_CHUNK = 64


def _split2(x):
    hi = x.astype(BF16)
    return hi, (x - hi.astype(F32)).astype(BF16)


def _dot_split(a2, b2, dims=(((1,), (0,)), ((), ()))):
    (ah, al), (bh, bl) = a2, b2
    dg = lambda x, y: lax.dot_general(x, y, dims, preferred_element_type=F32)
    return dg(ah, bh) + dg(ah, bl) + dg(al, bh)


def _rwkv_chunk_kernel(r_ref, k_ref, v_ref, ld_ref, a_ref, b_ref, y_ref, sout_ref, s_ref):
    c = RWKV_CHUNK
    rows = N_HEADS * c

    @pl.when(pl.program_id(1) == 0)
    def _():
        s_ref[...] = jnp.zeros_like(s_ref)

    ld = ld_ref[...]
    cum = _dot_hi(jnp.where(_iota((c, c), 0) >= _iota((c, c), 1), 1.0, 0.0), ld)
    cum_last = cum[c - 1:c, :]
    g_inv = jnp.exp(-cum)
    g_end = jnp.exp(cum_last - cum)
    own = (_iota((rows, BRANCH_WIDTH), 0) // c) == (_iota((rows, BRANCH_WIDTH), 1) // HEAD_DIM)
    stack = lambda x: jnp.where(own, jnp.concatenate([x] * N_HEADS, axis=0), 0.0)
    tile = lambda x: jnp.concatenate([x] * N_HEADS, axis=0)
    ar = jnp.concatenate([stack(a_ref[...] * jnp.exp(cum - ld)), stack(r_ref[...] * jnp.exp(cum))], axis=0)
    bt = b_ref[...] * g_inv
    kt = k_ref[...] * g_inv
    nt = (((1,), (1,)), ((), ()))
    ar2 = _split2(ar)
    g_b = _dot_split(ar2, _split2(tile(bt)), nt)
    g_k = _dot_split(ar2, _split2(tile(kt)), nt)
    step_r = _iota((rows, rows), 0) % c
    step_c = _iota((rows, rows), 1) % c
    same = (_iota((rows, rows), 0) // c) == (_iota((rows, rows), 1) // c)
    strict = same & (step_c < step_r)
    incl = same & (step_c <= step_r)
    a_ab = jnp.where(strict, g_b[0:rows], 0.0)
    a_ak = jnp.where(strict, g_k[0:rows], 0.0)
    a_rb = jnp.where(incl, g_b[rows:], 0.0)
    a_rk = jnp.where(incl, g_k[rows:], 0.0)

    inv = jnp.where(_iota((rows, rows), 0) == _iota((rows, rows), 1), 1.0, 0.0) + a_ab
    power = a_ab
    for _ in range(int(math.log2(c)) - 1):
        p2 = _split2(power)
        power = _dot_split(p2, p2)
        inv = inv + _dot_split(_split2(inv), _split2(power))

    s = s_ref[...]
    vbd = stack(v_ref[...])
    vbd2 = _split2(vbd)
    w = _dot_split(ar2, _split2(s), nt)
    av = _dot_split(_split2(jnp.concatenate([a_ak, a_rk], axis=0)), vbd2)
    u = _dot_split(_split2(inv), _split2(w[0:rows] + av[0:rows]))
    u2 = _split2(u)
    y = w[rows:] + av[rows:] + _dot_split(_split2(a_rb), u2)
    out = y[0:c]
    for h in range(1, N_HEADS):
        out = out + y[h * c:(h + 1) * c]
    y_ref[...] = out
    cat2 = lambda p, q: (jnp.concatenate([p[0], q[0]], axis=0), jnp.concatenate([p[1], q[1]], axis=0))
    upd = _dot_split(cat2(u2, vbd2), cat2(_split2(stack(b_ref[...] * g_end)), _split2(stack(k_ref[...] * g_end))),
                     (((0,), (0,)), ((), ())))
    s_ref[...] = s * jnp.exp(cum_last) + upd

    @pl.when(pl.program_id(1) == pl.num_programs(1) - 1)
    def _():
        sout_ref[...] = s_ref[...]


def _rwkv_chunked(r, k, v, ld, a, b):
    bsz, t, _ = r.shape
    blk = pl.BlockSpec((None, RWKV_CHUNK, BRANCH_WIDTH), lambda bi, ci: (bi, ci, 0))
    st = pl.BlockSpec((None, BRANCH_WIDTH, BRANCH_WIDTH), lambda bi, ci: (bi, 0, 0))
    return pl.pallas_call(
        _rwkv_chunk_kernel,
        grid=(bsz, t // RWKV_CHUNK),
        in_specs=[blk] * 6,
        out_specs=[blk, st],
        out_shape=[jax.ShapeDtypeStruct((bsz, t, BRANCH_WIDTH), F32),
                   jax.ShapeDtypeStruct((bsz, BRANCH_WIDTH, BRANCH_WIDTH), F32)],
        scratch_shapes=[pltpu.VMEM((BRANCH_WIDTH, BRANCH_WIDTH), F32)],
        compiler_params=_cparams("parallel", "arbitrary"),
        name="rwkv_chunk",
    )(r, k, v, ld, a, b)


def _rwkv_post_kernel(y_ref, bonus_ref, g_ref, lnw_ref, lnb_ref, o_ref):
    y = y_ref[...]
    head_mean = _head_sum_matrix(1.0 / HEAD_DIM)
    cen = y - _dot_hi(y, head_mean)
    var = _dot_hi(cen * cen, head_mean)
    yn = cen * lax.rsqrt(var + RWKV_LN_EPS) * lnw_ref[...] + lnb_ref[...]
    o_ref[...] = (yn + bonus_ref[...]) * g_ref[...]


def _rwkv_post(y, bonus, g, ln_w, ln_b, tm):
    n = y.shape[0]
    blk = pl.BlockSpec((tm, BRANCH_WIDTH), lambda i: (i, 0))
    return pl.pallas_call(
        _rwkv_post_kernel,
        grid=(n // tm,),
        in_specs=[blk, blk, blk, _const_spec(ln_w.shape), _const_spec(ln_b.shape)],
        out_specs=blk,
        out_shape=jax.ShapeDtypeStruct((n, BRANCH_WIDTH), F32),
        compiler_params=_cparams("parallel"),
        name="rwkv_post",
    )(y, bonus, g, ln_w, ln_b)


def _rwkv_mix(u2, p1, s0, p, ln_w, ln_b, *, b, t, tm, seq_tiles, period):
    r, k, v, d, nkk, kka, g, bonus = _rwkv_prep(u2, p1, p, tm=tm, seq_tiles=seq_tiles, period=period)
    if s0 is None:
        r3 = lambda x: x.reshape(b, t, BRANCH_WIDTH)
        y, s_wide = _rwkv_chunked(r3(r), r3(k), r3(v), r3(d), r3(nkk), r3(kka))
        s_wide = s_wide.reshape(b, N_HEADS, HEAD_DIM, N_HEADS, HEAD_DIM)
        s_fin = jnp.stack([s_wide[:, h, :, h, :] for h in range(N_HEADS)], axis=1)
        o = _rwkv_post(y.reshape(b * t, BRANCH_WIDTH), bonus, g, ln_w, ln_b, tm)
        return o, s_fin.reshape(b * N_HEADS, HEAD_DIM, HEAD_DIM)
    t_rows = max(t, SAMPLE_ROWS)
    t_lanes = -(-t // RWKV_TB) * RWKV_TB

    def chains(a):
        a = a.reshape(b, t, N_HEADS, HEAD_DIM).transpose(0, 2, 1, 3).reshape(b * N_HEADS, t, HEAD_DIM)
        return a

    rows = [jnp.pad(chains(a), ((0, 0), (0, t_rows - t), (0, 0))) for a in (r, k, d, nkk, kka)]
    vt = jnp.pad(chains(v).transpose(0, 2, 1), ((0, 0), (0, 0), (0, t_lanes - t)))
    yt, s_fin = _rwkv_scan(*rows, vt, s0, rows=min(t_rows, RWKV_TB), n_steps=min(t, RWKV_TB))
    y = yt[:, :, :t].transpose(0, 2, 1).reshape(b, N_HEADS, t, HEAD_DIM).transpose(0, 2, 1, 3)
    o = _rwkv_post(y.reshape(b * t, BRANCH_WIDTH), bonus, g, ln_w, ln_b, tm)
    return o, s_fin


def _rwkv_params(mu, w0, w2, a0, a2, g2, k_k, k_a, r_k):
    row = lambda v: v.reshape(1, -1)
    zeros = jnp.zeros((RWKV_W_LORA, BRANCH_WIDTH), F32)
    return dict(mu=row(mu), w0=row(w0), w2=jnp.concatenate([w2, zeros]).astype(BF16), a0=row(a0),
                a2=jnp.concatenate([zeros, a2]).astype(BF16), g2=g2.astype(BF16), k_k=row(k_k), k_a=row(k_a),
                r_k=row(r_k))


def _ssd_params(conv_w, conv_b, dt_bias, a_log, d_skip, norm_g):
    a = a_log.astype(F32)
    pad_row = lambda v: jnp.pad(v, (0, LANES - N_HEADS))[None, :]
    pad_col = lambda v: jnp.pad(v, (0, SSM_HIST - N_HEADS))[:, None]
    return dict(conv_w=conv_w, conv_b=conv_b[None, :], dt_bias=pad_row(dt_bias), dt_bias_t=pad_col(dt_bias),
                a=pad_row(a), a_t=pad_col(a), d_skip=jnp.repeat(d_skip, HEAD_DIM)[None, :], norm_g=norm_g[None, :])


PROMPT_TM = 512
INPROJ_TM = 1024


def _sb_sample_mix(u3, page_table, cache_k, cache_v, layer, n_pool):
    b, t, _ = u3.shape
    q = u3[..., SEG_SB:SEG_SB + BRANCH_WIDTH]
    k = u3[..., SEG_SB + BRANCH_WIDTH:SEG_SB + 2 * BRANCH_WIDTH]
    v = u3[..., SEG_SB + 2 * BRANCH_WIDTH:SEG_SB + 3 * BRANCH_WIDTH]
    qt = jnp.tile(jnp.pad(q, ((0, 0), (0, SAMPLE_ROWS - t), (0, 0))), (1, N_HEADS, 1))
    rows = N_HEADS * SAMPLE_ROWS
    own = (np.arange(rows)[:, None] // SAMPLE_ROWS) == (np.arange(BRANCH_WIDTH)[None, :] // HEAD_DIM)
    qbd = jnp.where(own[None], qt, 0.0)
    page_rows = lambda a: jnp.pad(a, ((0, 0), (0, PAGE_SIZE - t), (0, 0)))
    o = _sb_sample(page_table, qbd, page_rows(k), page_rows(v), cache_k, cache_v, layer, n_pool)
    return o[:, :t]


def kernel(x_prompt, x_sample, cache_nsa_k, cache_nsa_v, cache_sb_k, cache_sb_v, state_win_kv, state_ssm_conv, state_ssm, state_rwkv_shift, state_rwkv, state_ffn_conv, page_table, norm1, norm2, w_in, nsa_q_norm, nsa_k_norm, nsa_cmp_pe, nsa_cmp_w, ssm_conv_w, ssm_conv_b, ssm_dt_bias, ssm_a_log, ssm_d, ssm_norm, rwkv_mu, rwkv_w0, rwkv_w2, rwkv_a0, rwkv_a2, rwkv_g2, rwkv_k_k, rwkv_k_a, rwkv_r_k, rwkv_ln_w, rwkv_ln_b, w_branch, w_out, ffn_up, ffn_conv_w, ffn_conv_b, ffn_down):
    bp, tp, _ = x_prompt.shape
    bs, ts, _ = x_sample.shape
    depth, n_pool = cache_nsa_k.shape[:2]
    n_pages = page_table.shape[1]
    past = n_pages * PAGE_SIZE
    n_buf = state_win_kv.shape[2]
    n_p, n_s = bp * tp, bs * ts
    assert tp % INPROJ_TM == 0 and tp >= NSA_WINDOW and past % NSA_BLOCK == 0 and ts < SAMPLE_ROWS
    assert n_pages % SB_PAGES == 0 and n_pages % NSA_PAGES == 0 and n_s % FFN_HALO == 0

    perm = _in_perm()
    w_in_p = jnp.where(perm >= 0, jnp.take(w_in, np.maximum(perm, 0), axis=2), 0.0).astype(BF16)
    nsa_ck = cache_nsa_k.reshape(depth * n_pool, PAGE_SIZE, LANES)
    nsa_cv = cache_nsa_v.reshape(depth * n_pool, PAGE_SIZE, LANES)
    token_minor = lambda c: jnp.transpose(c, (0, 1, 3, 4, 2)).reshape(depth * n_pool, -1, PAGE_SIZE)
    sb_ck = token_minor(cache_sb_k)
    sb_cv = token_minor(cache_sb_v)
    dummy = jnp.zeros((FFN_HALO, LANES), F32)
    seg = lambda u, start, width: u[..., start:start + width]

    xp = x_prompt.reshape(n_p, D_MODEL)
    xs = x_sample.reshape(n_s, D_MODEL)
    outs = [[] for _ in range(20)]
    for l in range(depth):
        nsa_p = _nsa_params(nsa_q_norm[l], nsa_k_norm[l], nsa_cmp_pe[l], nsa_cmp_w[l])
        ssd_p = _ssd_params(ssm_conv_w[l], ssm_conv_b[l], ssm_dt_bias[l], ssm_a_log[l], ssm_d[l], ssm_norm[l])
        rwkv_p = _rwkv_params(rwkv_mu[l], rwkv_w0[l], rwkv_w2[l], rwkv_a0[l], rwkv_a2[l], rwkv_g2[l],
                              rwkv_k_k[l], rwkv_k_a[l], rwkv_r_k[l].reshape(-1))
        ln_w, ln_b = rwkv_ln_w[l][None, :], rwkv_ln_b[l][None, :]
        wb, wo = w_branch[l].astype(BF16), w_out[l].astype(BF16)
        wup, wdn = ffn_up[l].astype(BF16), ffn_down[l].astype(BF16)

        up = _inproj(xp, norm1[l][None, :], w_in_p[l], INPROJ_TM)
        u3 = up.reshape(bp, tp, D_IN_PAD)
        r3 = lambda a: a.reshape(bp, tp, -1)
        qs, newk, newv, win, newk_b, win_b, vslc, vwin = _nsa_prep(up, nsa_p["q_gain"], nsa_p["k_gain"], PROMPT_TM)
        kc, vc = _nsa_compress_prompt(r3(newk), r3(newv), nsa_p)
        front = lambda a: jnp.pad(r3(a), ((0, 0), (NSA_WINDOW, 0), (0, 0)))
        o_a = _nsa_prompt(r3(qs), u3, kc, vc, r3(newk_b), r3(vslc), front(win_b), front(vwin))
        dtt = jnp.pad(jnp.swapaxes(seg(u3, SEG_SSM_DT, N_HEADS), 1, 2), ((0, 0), (0, SSM_HIST - N_HEADS), (0, 0)))
        o_b, ssm_fin = _ssd(u3, dtt, jnp.zeros((bp, SSM_HIST, SSM_CONV_DIM), F32),
                            jnp.zeros((bp, SSM_STATE, BRANCH_WIDTH), F32), ssd_p, q=SSM_CHUNK, n_valid=SSM_CHUNK)
        o_c, rwkv_fin = _rwkv_mix(up, dummy, None, rwkv_p, ln_w, ln_b,
                                  b=bp, t=tp, tm=PROMPT_TM, seq_tiles=tp // PROMPT_TM, period=0)
        sb_k = seg(u3, SEG_SB + BRANCH_WIDTH, BRANCH_WIDTH)
        sb_v = seg(u3, SEG_SB + 2 * BRANCH_WIDTH, BRANCH_WIDTH)
        o_d = _sb_prompt(u3, sb_k.astype(BF16), sb_v.astype(BF16))
        flat = lambda a: a.reshape(n_p, BRANCH_WIDTH)
        xp = _merge(xp, [flat(o_a), flat(o_b), o_c, flat(o_d)], up, wb, wo, PROMPT_TM)
        xp, hs = _ffn(xp, dummy, dummy, norm2[l][None, :], wup, ffn_conv_w[l], ffn_conv_b[l][None, :], wdn,
                      tm=PROMPT_TM, seq_tiles=tp // PROMPT_TM, period=0, keep=FFN_HALO)
        prompt_states = (
            r3(newk).reshape(bp, tp, 2, HEAD_DIM), r3(newv).reshape(bp, tp, 2, HEAD_DIM),
            sb_k.reshape(bp, tp, N_HEADS, HEAD_DIM), sb_v.reshape(bp, tp, N_HEADS, HEAD_DIM),
            r3(win)[:, tp - min(NSA_WINDOW, tp):].reshape(bp, -1, 2, HEAD_DIM),
            seg(u3, SEG_SSM_XBC, SSM_CONV_DIM)[:, tp - (SSM_CONV - 1):],
            ssm_fin.reshape(bp, SSM_STATE, N_HEADS, HEAD_DIM).transpose(0, 2, 3, 1),
            seg(u3, SEG_RWKV, RWKV_IN)[:, tp - 1:],
            rwkv_fin.reshape(bp, N_HEADS, HEAD_DIM, HEAD_DIM),
            hs.reshape(bp, tp // PROMPT_TM, FFN_HALO, 2 * D_FF)[:, -1, FFN_HALO - (FFN_CONV - 1):])

        us = _inproj(xs, norm1[l][None, :], w_in_p[l], n_s)
        u3 = us.reshape(bs, ts, D_IN_PAD)
        o_a, newk, newv, win = _nsa_sample_mix(u3, page_table, state_win_kv[l].reshape(bs, n_buf, LANES), nsa_p,
                                               nsa_ck, nsa_cv, l, n_pool)
        t_pad = 2 * SAMPLE_ROWS
        u3_pad = jnp.pad(u3, ((0, 0), (0, t_pad - ts), (0, 0)))
        dtt = jnp.pad(jnp.swapaxes(seg(u3_pad, SEG_SSM_DT, N_HEADS), 1, 2), ((0, 0), (0, SSM_HIST - N_HEADS), (0, 0)))
        hist0 = jnp.pad(state_ssm_conv[l], ((0, 0), (SSM_HIST - (SSM_CONV - 1), 0), (0, 0)))
        h0 = state_ssm[l].transpose(0, 3, 1, 2).reshape(bs, SSM_STATE, BRANCH_WIDTH)
        o_b, ssm_fin = _ssd(u3_pad, dtt, hist0, h0, ssd_p, q=t_pad, n_valid=ts)
        shift_rows = jnp.pad(state_rwkv_shift[l], ((0, 0), (0, ts - 1), (0, 0))).reshape(n_s, RWKV_IN)
        o_c, rwkv_fin = _rwkv_mix(us, shift_rows, state_rwkv[l].reshape(bs * N_HEADS, HEAD_DIM, HEAD_DIM), rwkv_p,
                                  ln_w, ln_b, b=bs, t=ts, tm=n_s, seq_tiles=0, period=ts)
        o_d = _sb_sample_mix(u3, page_table, sb_ck, sb_cv, l, n_pool)
        flat = lambda a: a.reshape(n_s, BRANCH_WIDTH)
        xs = _merge(xs, [flat(o_a), flat(o_b[:, :ts]), o_c, flat(o_d)], us, wb, wo, n_s)
        conv_state = state_ffn_conv[l]
        prev1 = jnp.pad(conv_state[:, 1:2], ((0, 0), (0, ts - 1), (0, 0))).reshape(n_s, 2 * D_FF)
        prev2 = jnp.pad(conv_state, ((0, 0), (0, ts - 2), (0, 0))).reshape(n_s, 2 * D_FF)
        xs, hs = _ffn(xs, prev1, prev2, norm2[l][None, :], wup, ffn_conv_w[l], ffn_conv_b[l][None, :], wdn,
                      tm=n_s, seq_tiles=0, period=ts, keep=n_s)
        tail = lambda old, new, n: jnp.concatenate([old, new], axis=1)[:, -n:]
        sample_states = (
            newk.reshape(bs, ts, 2, HEAD_DIM), newv.reshape(bs, ts, 2, HEAD_DIM),
            seg(u3, SEG_SB + BRANCH_WIDTH, BRANCH_WIDTH).reshape(bs, ts, N_HEADS, HEAD_DIM),
            seg(u3, SEG_SB + 2 * BRANCH_WIDTH, BRANCH_WIDTH).reshape(bs, ts, N_HEADS, HEAD_DIM),
            tail(state_win_kv[l], win.reshape(bs, ts, 2, HEAD_DIM), n_buf),
            tail(state_ssm_conv[l], seg(u3, SEG_SSM_XBC, SSM_CONV_DIM), SSM_CONV - 1),
            ssm_fin.reshape(bs, SSM_STATE, N_HEADS, HEAD_DIM).transpose(0, 2, 3, 1),
            seg(u3, SEG_RWKV, RWKV_IN)[:, ts - 1:],
            rwkv_fin.reshape(bs, N_HEADS, HEAD_DIM, HEAD_DIM),
            tail(conv_state, hs.reshape(bs, ts, 2 * D_FF), FFN_CONV - 1))
        for j in range(10):
            outs[2 * j].append(prompt_states[j])
            outs[2 * j + 1].append(sample_states[j])

    return (xp.reshape(bp, tp, D_MODEL), xs.reshape(bs, ts, D_MODEL)) + tuple(jnp.stack(o) for o in outs)
```

```python
import functools
import math

import numpy as np
import jax
import jax.numpy as jnp
from jax import lax
from jax.experimental import pallas as pl
from jax.experimental.pallas import tpu as pltpu

F32 = jnp.float32
BF16 = jnp.bfloat16
HIGHEST = lax.Precision.HIGHEST

D_MODEL = 1024
N_BRANCH = 4
BRANCH_WIDTH = D_MODEL // N_BRANCH
HEAD_DIM = 64
N_HEADS = BRANCH_WIDTH // HEAD_DIM
Q_BLOCK = 128
PAGE_SIZE = 128
NORM_EPS = 1e-6
NEG = -1e30
NSA_BLOCK = 64
NSA_TOPK = 16
NSA_WINDOW = 512
NSA_FORCED = 2.0 * N_HEADS
SSM_GROUPS = 2
SSM_STATE = 128
SSM_CONV = 4
SSM_CHUNK = 128
SSM_CONV_DIM = BRANCH_WIDTH + 2 * SSM_GROUPS * SSM_STATE
RWKV_W_LORA = 64
RWKV_A_LORA = 64
RWKV_G_LORA = 128
RWKV_IN = 3 * BRANCH_WIDTH + RWKV_W_LORA + RWKV_A_LORA + RWKV_G_LORA
RWKV_LN_EPS = 64e-5
D_FF = 2816
FFN_CONV = 3

LANES = 128
VMEM_LIMIT = 56 * 1024 * 1024

SEG_MERGE = 0
SEG_RWKV = 4096
SEG_NSA_Q = 5120
SEG_SSM_XBC = 5376
SEG_SB = 6144
SEG_NSA_KV = 6912
SEG_NSA_GATE = 7296
SEG_SSM_Z = 7424
SEG_SSM_DT = 7680
D_IN_PAD = 8192


def _in_perm():
    sizes = (BRANCH_WIDTH, 6 * HEAD_DIM, 3 * N_HEADS, BRANCH_WIDTH, SSM_CONV_DIM, N_HEADS, RWKV_IN,
             3 * BRANCH_WIDTH, N_BRANCH * D_MODEL)
    off = np.concatenate([[0], np.cumsum(sizes)])
    o_q, o_kv, o_gate, o_z, o_xbc, o_dt, o_rwkv, o_sb, o_merge = off[:-1]
    perm = -np.ones((D_IN_PAD,), np.int64)
    perm[SEG_MERGE:SEG_MERGE + 4096] = o_merge + np.arange(4096)
    perm[SEG_RWKV:SEG_RWKV + RWKV_IN] = o_rwkv + np.arange(RWKV_IN)
    perm[SEG_NSA_Q:SEG_NSA_Q + 256] = o_q + np.arange(256)
    perm[SEG_SSM_XBC:SEG_SSM_XBC + 768] = o_xbc + np.arange(768)
    perm[SEG_SB:SEG_SB + 768] = o_sb + np.arange(768)
    kv_order = (0, 2, 1, 3, 4, 5)
    for j, src in enumerate(kv_order):
        perm[SEG_NSA_KV + 64 * j:SEG_NSA_KV + 64 * (j + 1)] = o_kv + 64 * src + np.arange(64)
    perm[SEG_NSA_GATE:SEG_NSA_GATE + 12] = o_gate + np.arange(12)
    perm[SEG_SSM_Z:SEG_SSM_Z + 256] = o_z + np.arange(256)
    perm[SEG_SSM_DT:SEG_SSM_DT + 4] = o_dt + np.arange(4)
    return perm


def _cparams(*sem):
    return pltpu.CompilerParams(dimension_semantics=tuple(sem), vmem_limit_bytes=VMEM_LIMIT)


def _const_spec(shape):
    nd = len(shape)
    return pl.BlockSpec(shape, lambda *_: (0,) * nd)


def _iota(shape, dim):
    return lax.broadcasted_iota(jnp.int32, shape, dim)


def _dot(a, b):
    return jnp.dot(a, b, preferred_element_type=F32)


def _dot_hi(a, b):
    return jnp.dot(a, b, preferred_element_type=F32, precision=HIGHEST)


def _dot_nt(a, b):
    return lax.dot_general(a, b, (((1,), (1,)), ((), ())), preferred_element_type=F32)


def _sigmoid(x):
    return 1.0 / (1.0 + jnp.exp(-x))


def _silu(x):
    return x * _sigmoid(x)


def _softplus(x):
    return jnp.maximum(x, 0.0) + jnp.log(1.0 + jnp.exp(-jnp.abs(x)))


def _inproj_kernel(x_ref, g_ref, w_ref, o_ref, xn_ref):
    @pl.when(pl.program_id(1) == 0)
    def _():
        x = x_ref[...]
        ms = jnp.mean(x * x, axis=-1, keepdims=True)
        xn_ref[...] = (x * lax.rsqrt(ms + NORM_EPS) * g_ref[...]).astype(BF16)

    o_ref[...] = _dot(xn_ref[...], w_ref[...])


def _inproj(x, g, w, tm, tn=1024):
    n = x.shape[0]
    return pl.pallas_call(
        _inproj_kernel,
        grid=(n // tm, D_IN_PAD // tn),
        in_specs=[pl.BlockSpec((tm, D_MODEL), lambda i, j: (i, 0)),
                  pl.BlockSpec((1, D_MODEL), lambda i, j: (0, 0)),
                  pl.BlockSpec((D_MODEL, tn), lambda i, j: (0, j))],
        out_specs=pl.BlockSpec((tm, tn), lambda i, j: (i, j)),
        out_shape=jax.ShapeDtypeStruct((n, D_IN_PAD), F32),
        scratch_shapes=[pltpu.VMEM((tm, D_MODEL), BF16)],
        compiler_params=_cparams("parallel", "arbitrary"),
        name="inproj",
    )(x, g, w)


def _merge_kernel(x_ref, oa_ref, ob_ref, oc_ref, od_ref, gate_ref, wb_ref, wo_ref, out_ref):
    acc = None
    for n, o_ref in enumerate((oa_ref, ob_ref, oc_ref, od_ref)):
        proj = _dot(o_ref[...].astype(BF16), wb_ref[n])
        term = _sigmoid(gate_ref[:, n * D_MODEL:(n + 1) * D_MODEL]) * proj
        acc = term if acc is None else acc + term
    out_ref[...] = x_ref[...] + _dot(acc.astype(BF16), wo_ref[...])


def _merge(x, branches, u, wb, wo, tm):
    n = x.shape[0]
    row = lambda i: (i, 0)
    bspec = pl.BlockSpec((tm, BRANCH_WIDTH), row)
    return pl.pallas_call(
        _merge_kernel,
        grid=(n // tm,),
        in_specs=[pl.BlockSpec((tm, D_MODEL), row), bspec, bspec, bspec, bspec,
                  pl.BlockSpec((tm, N_BRANCH * D_MODEL), lambda i: (i, SEG_MERGE // (N_BRANCH * D_MODEL))),
                  _const_spec((N_BRANCH, BRANCH_WIDTH, D_MODEL)),
                  _const_spec((D_MODEL, D_MODEL))],
        out_specs=pl.BlockSpec((tm, D_MODEL), row),
        out_shape=jax.ShapeDtypeStruct((n, D_MODEL), F32),
        compiler_params=_cparams("parallel"),
        name="merge",
    )(x, *branches, u, wb, wo)


FFN_COLS = 256
FFN_HALO = 8


def _ffn_kernel(x_ref, xp_ref, p1_ref, p2_ref, g_ref, wup_ref, cw_ref, cb_ref, wd_ref,
                out_ref, hs_ref, xn_ref, acc_ref, *, seq_tiles, period, keep):
    tm = x_ref.shape[0]
    prompt = seq_tiles > 0
    halo = FFN_HALO if prompt else 0

    def norm(x):
        ms = jnp.mean(x * x, axis=-1, keepdims=True)
        return (x * lax.rsqrt(ms + NORM_EPS) * g_ref[...]).astype(BF16)

    x = x_ref[...]
    xn_ref[halo:halo + tm, :] = norm(x)
    if prompt:
        first = (pl.program_id(0) % seq_tiles) == 0
        xn_ref[0:halo, :] = norm(xp_ref[...])
        hist_ok = jnp.where(first, 0.0, 1.0)
    else:
        step = _iota((tm, 1), 0) % period
    acc_ref[...] = x

    for c in range(D_FF // FFN_COLS):
        gs = slice(c * FFN_COLS, (c + 1) * FFN_COLS)
        us = slice(D_FF + c * FFN_COLS, D_FF + (c + 1) * FFN_COLS)
        xn = xn_ref[...]
        conv = []
        for part, cs in enumerate((gs, us)):
            h = _dot(xn, wup_ref[:, cs])
            if prompt:
                rows = _iota((tm + halo, 1), 0)
                h = jnp.where(rows < halo, h * hist_ok, h)
            h1 = pltpu.roll(h, 1, axis=0)
            h2 = pltpu.roll(h, 2, axis=0)
            if not prompt:
                h1 = jnp.where(step >= 1, h1, p1_ref[:, cs])
                h2 = jnp.where(step >= 2, h2, p2_ref[:, cs])
            y = cw_ref[2:3, cs] * h + cw_ref[1:2, cs] * h1 + cw_ref[0:1, cs] * h2 + cb_ref[:, cs]
            conv.append(y[halo:, :])
            hs_ref[0, :, cs] = h[halo + tm - keep:, :]
        act = (_silu(conv[0]) * conv[1]).astype(BF16)
        acc_ref[...] += _dot(act, wd_ref[gs, :])
    out_ref[...] = acc_ref[...]


def _ffn(x, prev1, prev2, g, wup, cw, cb, wd, *, tm, seq_tiles, period, keep):
    n = x.shape[0]
    nt = n // tm
    prompt = seq_tiles > 0
    halo = FFN_HALO if prompt else 0
    hb = tm // FFN_HALO
    if prompt:
        xp_spec = pl.BlockSpec((FFN_HALO, D_MODEL), lambda i: (jnp.maximum(i * hb - 1, 0), 0))
        xp = x
        p_spec = _const_spec(prev1.shape)
    else:
        xp_spec = _const_spec((FFN_HALO, D_MODEL))
        xp = x
        p_spec = pl.BlockSpec((tm, 2 * D_FF), lambda i: (i, 0))
    single = dict(pipeline_mode=pl.Buffered(1))
    kern = functools.partial(_ffn_kernel, seq_tiles=seq_tiles, period=period, keep=keep)
    return pl.pallas_call(
        kern,
        grid=(nt,),
        in_specs=[pl.BlockSpec((tm, D_MODEL), lambda i: (i, 0)), xp_spec, p_spec, p_spec,
                  _const_spec((1, D_MODEL)),
                  pl.BlockSpec((D_MODEL, 2 * D_FF), lambda i: (0, 0), **single),
                  _const_spec((FFN_CONV, 2 * D_FF)), _const_spec((1, 2 * D_FF)),
                  pl.BlockSpec((D_FF, D_MODEL), lambda i: (0, 0), **single)],
        out_specs=[pl.BlockSpec((tm, D_MODEL), lambda i: (i, 0)),
                   pl.BlockSpec((1, keep, 2 * D_FF), lambda i: (i, 0, 0))],
        out_shape=[jax.ShapeDtypeStruct((n, D_MODEL), F32),
                   jax.ShapeDtypeStruct((nt, keep, 2 * D_FF), F32)],
        scratch_shapes=[pltpu.VMEM((tm + halo, D_MODEL), BF16), pltpu.VMEM((tm, D_MODEL), F32)],
        compiler_params=_cparams("parallel"),
        name="ffn",
    )(x, xp, prev1, prev2, g, wup, cw, cb, wd)


SB_KT = 256
SB_PAGES = 16


def _head_block_mask(rows_per_head, n_rows):
    r = _iota((n_rows, BRANCH_WIDTH), 0) // rows_per_head
    c = _iota((n_rows, BRANCH_WIDTH), 1) // HEAD_DIM
    return r == c


def _suffix_matrix(n):
    return jnp.where(_iota((n, n), 0) > _iota((n, n), 1), 1.0, 0.0).astype(BF16)


def _sb_block(qbd, k, v, mask, carry, tri, token_minor=False):
    z = _dot(qbd, k) if token_minor else _dot_nt(qbd, k)
    sp = _softplus(z)
    l1 = -sp if mask is None else jnp.where(mask, -sp, 0.0)
    hi = l1.astype(BF16)
    lo = (l1 - hi.astype(F32)).astype(BF16)
    after = _dot(hi, tri) + _dot(lo, tri) + carry
    a = jnp.exp(z + l1 + after)
    if mask is not None:
        a = jnp.where(mask, a, 0.0)
    pv = _dot_nt(a.astype(BF16), v) if token_minor else _dot(a.astype(BF16), v)
    return pv, carry + jnp.sum(l1, axis=1, keepdims=True)


SB_DEAD = -104.0


def _sb_alive(carry):
    return (jnp.max(carry) > SB_DEAD).astype(jnp.int32)


def _fold_heads(acc, rows_per_head):
    masked = jnp.where(_head_block_mask(rows_per_head, acc.shape[0]), acc, 0.0)
    out = masked[0:rows_per_head]
    for h in range(1, N_HEADS):
        out = out + masked[h * rows_per_head:(h + 1) * rows_per_head]
    return out


def _sb_prompt_kernel(q_ref, k_ref, v_ref, o_ref, acc_ref, carry_ref):
    i = pl.program_id(1)
    rows = N_HEADS * Q_BLOCK
    q = q_ref[...] * (HEAD_DIM ** -0.5)
    qbd = jnp.where(_head_block_mask(Q_BLOCK, rows), jnp.concatenate([q] * N_HEADS, axis=0), 0.0).astype(BF16)
    tri = _suffix_matrix(SB_KT)
    acc_ref[...] = jnp.zeros_like(acc_ref)
    carry_ref[...] = jnp.zeros_like(carry_ref)
    qpos = i * Q_BLOCK + _iota((rows, 1), 0) % Q_BLOCK
    n_chunks = (i * Q_BLOCK) // SB_KT + 1

    def body(state):
        jj, _ = state
        j = n_chunks - 1 - jj
        start = pl.multiple_of(j * SB_KT, SB_KT)
        kpos = start + _iota((1, SB_KT), 1)
        pv, carry = _sb_block(qbd, k_ref[pl.ds(start, SB_KT), :], v_ref[pl.ds(start, SB_KT), :],
                              kpos < qpos, carry_ref[...], tri)
        acc_ref[...] += pv
        carry_ref[...] = carry
        return jj + 1, _sb_alive(carry)

    lax.while_loop(lambda st: (st[0] < n_chunks) & (st[1] > 0), body, (jnp.int32(0), jnp.int32(1)))
    o_ref[...] = _fold_heads(acc_ref[...], Q_BLOCK)


def _sb_prompt(u3, kb, vb):
    b, t, _ = u3.shape
    rows = N_HEADS * Q_BLOCK
    return pl.pallas_call(
        _sb_prompt_kernel,
        grid=(b, t // Q_BLOCK),
        in_specs=[pl.BlockSpec((None, Q_BLOCK, BRANCH_WIDTH), lambda bi, i: (bi, i, SEG_SB // BRANCH_WIDTH)),
                  pl.BlockSpec((None, t, BRANCH_WIDTH), lambda bi, i: (bi, 0, 0)),
                  pl.BlockSpec((None, t, BRANCH_WIDTH), lambda bi, i: (bi, 0, 0))],
        out_specs=pl.BlockSpec((None, Q_BLOCK, BRANCH_WIDTH), lambda bi, i: (bi, i, 0)),
        out_shape=jax.ShapeDtypeStruct((b, t, BRANCH_WIDTH), F32),
        scratch_shapes=[pltpu.VMEM((rows, BRANCH_WIDTH), F32), pltpu.VMEM((rows, 1), F32)],
        compiler_params=_cparams("parallel", "parallel"),
        name="sb_prompt",
    )(u3, kb, vb)


SAMPLE_ROWS = 8


def _sb_sample_kernel(pt_ref, q_ref, kn_ref, vn_ref, *refs, n_steps):
    k_refs = refs[:SB_PAGES]
    v_refs = refs[SB_PAGES:2 * SB_PAGES]
    o_ref, acc_ref, carry_ref, alive_ref = refs[2 * SB_PAGES:]
    s = pl.program_id(1)
    rows = N_HEADS * SAMPLE_ROWS

    def queries():
        return (q_ref[...] * HEAD_DIM ** -0.5).astype(BF16), _suffix_matrix(PAGE_SIZE)

    @pl.when(s == 0)
    def _():
        qbd, tri = queries()
        step = _iota((rows, 1), 0) % SAMPLE_ROWS
        col = _iota((1, PAGE_SIZE), 1)
        pv, carry = _sb_block(qbd, kn_ref[...].astype(BF16), vn_ref[...].astype(BF16), col < step,
                              jnp.zeros((rows, 1), F32), tri)
        acc_ref[...] = pv
        carry_ref[...] = carry
        alive_ref[0] = _sb_alive(carry)

    for r in range(SB_PAGES):
        @pl.when(alive_ref[0] > 0)
        def _():
            qbd, tri = queries()
            pv, carry = _sb_block(qbd, k_refs[r][...].astype(BF16), v_refs[r][...].astype(BF16), None,
                                  carry_ref[...], tri, token_minor=True)
            acc_ref[...] += pv
            carry_ref[...] = carry
            alive_ref[0] = _sb_alive(carry)

    @pl.when(s == n_steps - 1)
    def _():
        o_ref[...] = _fold_heads(acc_ref[...], SAMPLE_ROWS)


def _sb_sample(page_table, qbd, kn, vn, cache_k, cache_v, layer, n_pool):
    b, n_pages = page_table.shape
    n_steps = n_pages // SB_PAGES
    rows = N_HEADS * SAMPLE_ROWS
    base = layer * n_pool

    def page_spec(r):
        return pl.BlockSpec((None, BRANCH_WIDTH, PAGE_SIZE),
                            lambda bi, s, pt: (base + pt[bi, n_pages - 1 - (s * SB_PAGES + r)], 0, 0))

    per_b = lambda shape: pl.BlockSpec((None,) + shape, lambda bi, s, pt: (bi, 0, 0))
    grid_spec = pltpu.PrefetchScalarGridSpec(
        num_scalar_prefetch=1,
        grid=(b, n_steps),
        in_specs=[per_b((rows, BRANCH_WIDTH)), per_b((PAGE_SIZE, BRANCH_WIDTH)), per_b((PAGE_SIZE, BRANCH_WIDTH))]
                 + [page_spec(r) for r in range(SB_PAGES)] * 2,
        out_specs=per_b((SAMPLE_ROWS, BRANCH_WIDTH)),
        scratch_shapes=[pltpu.VMEM((rows, BRANCH_WIDTH), F32), pltpu.VMEM((rows, 1), F32),
                        pltpu.SMEM((1,), jnp.int32)],
    )
    return pl.pallas_call(
        functools.partial(_sb_sample_kernel, n_steps=n_steps),
        grid_spec=grid_spec,
        out_shape=jax.ShapeDtypeStruct((b, SAMPLE_ROWS, BRANCH_WIDTH), F32),
        compiler_params=_cparams("parallel", "arbitrary"),
        name="sb_sample",
    )(page_table, qbd, kn, vn, *([cache_k] * SB_PAGES), *([cache_v] * SB_PAGES))


NSA_KT = 512
ALIBI_SLOPES = tuple(2.0 ** (-8.0 * (h + 1.0) / N_HEADS) for h in range(N_HEADS))


def _row_slopes(rq):
    h = _iota((N_HEADS * rq, 1), 0) // rq
    out = jnp.full((N_HEADS * rq, 1), ALIBI_SLOPES[0], F32)
    for i in range(1, N_HEADS):
        out = jnp.where(h == i, ALIBI_SLOPES[i], out)
    return out


def _rep_right_matrix():
    r = _iota((LANES, BRANCH_WIDTH), 0)
    c = _iota((LANES, BRANCH_WIDTH), 1)
    return jnp.where(r == HEAD_DIM + c % HEAD_DIM, 1.0, 0.0)


def _nsa_prep_kernel(q_ref, kv_ref, qg_ref, kg_ref, qs_out, newk_out, newv_out, win_out, newkb_out, winb_out,
                     vslc_out, vwin_out):
    q = q_ref[...]
    qn = q * lax.rsqrt(_dot_hi(q * q, _head_sum_matrix(1.0 / HEAD_DIM)) + NORM_EPS) * qg_ref[...]
    qs_out[...] = _dot((qn * HEAD_DIM ** -0.5).astype(BF16), _query_place_matrix()).astype(BF16)
    kv = kv_ref[...]
    w = kv.shape[1]
    same = _iota((w, w), 0) // HEAD_DIM == _iota((w, w), 1) // HEAD_DIM
    ms = _dot_hi(kv * kv, jnp.where(same, 1.0 / HEAD_DIM, 0.0))
    grp = _iota((1, w), 1) // HEAD_DIM
    normed = jnp.where((grp == 1) | (grp == 4), kv * lax.rsqrt(ms + NORM_EPS) * kg_ref[...], kv)
    newk = normed[:, 0:LANES]
    newv = normed[:, LANES:2 * LANES]
    win = normed[:, 2 * LANES:3 * LANES]
    newk_out[...] = newk
    newv_out[...] = newv
    win_out[...] = win
    newkb_out[...] = newk.astype(BF16)
    winb_out[...] = win.astype(BF16)
    rep = _rep_right_matrix().astype(BF16)
    vslc_out[...] = _dot(newv.astype(BF16), rep).astype(BF16)
    vwin_out[...] = _dot(win.astype(BF16), rep).astype(BF16)


def _nsa_prep(u, q_gain, k_gain, tm):
    n = u.shape[0]
    out = lambda w: pl.BlockSpec((tm, w), lambda i: (i, 0))
    shp = lambda w, dt: jax.ShapeDtypeStruct((n, w), dt)
    return pl.pallas_call(
        _nsa_prep_kernel,
        grid=(n // tm,),
        in_specs=[pl.BlockSpec((tm, BRANCH_WIDTH), lambda i: (i, SEG_NSA_Q // BRANCH_WIDTH)),
                  pl.BlockSpec((tm, 3 * LANES), lambda i: (i, SEG_NSA_KV // (3 * LANES))),
                  _const_spec(q_gain.shape), _const_spec(k_gain.shape)],
        out_specs=[out(PLACED_WIDTH), out(LANES), out(LANES), out(LANES), out(LANES), out(LANES),
                   out(BRANCH_WIDTH), out(BRANCH_WIDTH)],
        out_shape=[shp(PLACED_WIDTH, BF16), shp(LANES, F32), shp(LANES, F32), shp(LANES, F32), shp(LANES, BF16),
                   shp(LANES, BF16), shp(BRANCH_WIDTH, BF16), shp(BRANCH_WIDTH, BF16)],
        compiler_params=_cparams("parallel"),
        name="nsa_prep",
    )(u, u, q_gain, k_gain)


def _nsa_compress(k_ref, v_ref, pek_ref, pev_ref, wk_ref, wv_ref, kn_ref, nb):
    def body(m, carry):
        ak, av = carry
        xk = k_ref[pl.ds(m, nb, stride=NSA_BLOCK), :] + pek_ref[pl.ds(m, 1), :]
        xv = v_ref[pl.ds(m, nb, stride=NSA_BLOCK), :] + pev_ref[pl.ds(m, 1), :]
        return ak + _dot(xk.astype(BF16), wk_ref[m]), av + _dot(xv.astype(BF16), wv_ref[m])

    ak, av = lax.fori_loop(0, NSA_BLOCK, body,
                           (jnp.zeros((nb, LANES), F32), jnp.zeros((nb, BRANCH_WIDTH), F32)), unroll=8)
    ms = jnp.sum(ak * ak, axis=1, keepdims=True) * (1.0 / HEAD_DIM)
    kc = ak * lax.rsqrt(ms + NORM_EPS) * kn_ref[...]
    return kc.astype(BF16), av.astype(BF16)


def _nsa_compress_pages(k_ref, v_ref, pek_ref, pev_ref, wk_ref, wv_ref, kn_ref, n_pages):
    def body(d, carry):
        ak, av = carry
        xk = k_ref[pl.ds(d, n_pages, stride=PAGE_SIZE), :] + pek_ref[pl.ds(d, 1), :]
        xv = v_ref[pl.ds(d, n_pages, stride=PAGE_SIZE), :] + pev_ref[pl.ds(d, 1), :]
        return ak + _dot(xk.astype(BF16), wk_ref[d]), av + _dot(xv.astype(BF16), wv_ref[d])

    ak, av = lax.fori_loop(0, HEAD_DIM, body,
                           (jnp.zeros((n_pages, 2 * LANES), F32), jnp.zeros((n_pages, 2 * BRANCH_WIDTH), F32)),
                           unroll=8)
    ak = jnp.concatenate([ak[:, 0:LANES], ak[:, LANES:]], axis=0)
    av = jnp.concatenate([av[:, 0:BRANCH_WIDTH], av[:, BRANCH_WIDTH:]], axis=0)
    ms = jnp.sum(ak * ak, axis=1, keepdims=True) * (1.0 / HEAD_DIM)
    kc = ak * lax.rsqrt(ms + NORM_EPS) * kn_ref[...]
    return kc.astype(BF16), av.astype(BF16)


def _nsa_compress_kernel(k_ref, v_ref, pek_ref, pev_ref, wk_ref, wv_ref, kn_ref, kc_out, vc_out):
    kc, vc = _nsa_compress(k_ref, v_ref, pek_ref, pev_ref, wk_ref, wv_ref, kn_ref, kc_out.shape[0])
    kc_out[...] = kc
    vc_out[...] = vc


def _nsa_compress_prompt(newk, newv, p):
    b, t, _ = newk.shape
    nb = t // NSA_BLOCK
    per_b = lambda rows, w: pl.BlockSpec((None, rows, w), lambda bi: (bi, 0, 0))
    consts = [p["pe_k"], p["pe_v"], p["w_k"], p["w_v"], p["k_gain_pad"]]
    return pl.pallas_call(
        _nsa_compress_kernel,
        grid=(b,),
        in_specs=[per_b(t, LANES), per_b(t, LANES)] + [_const_spec(a.shape) for a in consts],
        out_specs=[per_b(nb, LANES), per_b(nb, BRANCH_WIDTH)],
        out_shape=[jax.ShapeDtypeStruct((b, nb, LANES), BF16), jax.ShapeDtypeStruct((b, nb, BRANCH_WIDTH), BF16)],
        compiler_params=_cparams("parallel"),
        name="nsa_compress",
    )(newk, newv, *consts)


def _nsa_cmp_branch(ql, kc, vc, qpos, q0, slopes, blk=None):
    nb = kc.shape[0]
    blk = _iota((1, nb), 1) if blk is None else blk
    bend = (blk + 1) * NSA_BLOCK - 1
    valid = bend <= qpos
    s = jnp.where(valid, _dot_nt(ql, kc) + slopes * (bend - q0).astype(F32), NEG)
    e = jnp.exp(s - jnp.max(s, axis=1, keepdims=True))
    p = jnp.where(valid, e / jnp.sum(e, axis=1, keepdims=True), 0.0)
    return _dot(p.astype(BF16), vc), p


def _nsa_select(imp, qpos_q, nbl, blk=None):
    blk = _iota((1, nbl), 1) if blk is None else blk
    blk_f = blk.astype(F32)
    cur = qpos_q // NSA_BLOCK
    forced = (blk == 0) | (blk == cur) | (blk == cur - 1)
    work = jnp.where(blk > cur, NEG, jnp.where(forced, NSA_FORCED, imp))
    sel = jnp.zeros(work.shape, F32)
    for _ in range(min(NSA_TOPK, nbl)):
        mx = jnp.max(work, axis=1, keepdims=True)
        idx = jnp.min(jnp.where(work == mx, blk_f, float(nbl)), axis=1, keepdims=True)
        hit = blk_f == idx
        sel = jnp.where(hit & (mx > 0.5 * NEG), 1.0, sel)
        work = jnp.where(hit, -3e38, work)
    return sel


def _block_expand_matrix(nbl, start, kt, blk_col=None):
    blk_col = _iota((nbl, kt), 0) if blk_col is None else blk_col
    return jnp.where(blk_col == (start + _iota((nbl, kt), 1)) // NSA_BLOCK, 1.0, 0.0).astype(BF16)


def _gate_expand(gl, j):
    r = _iota((LANES, BRANCH_WIDTH), 0)
    c = _iota((LANES, BRANCH_WIDTH), 1)
    return _sigmoid(_dot_hi(gl, jnp.where(r == (c // HEAD_DIM) * 3 + j, 1.0, 0.0)))


def _softmax_rows(s):
    e = jnp.exp(s - jnp.max(s, axis=1, keepdims=True))
    return e / jnp.sum(e, axis=1, keepdims=True)


PLACED_WIDTH = 2 * N_HEADS * LANES


def _query_place_matrix():
    r = _iota((BRANCH_WIDTH, PLACED_WIDTH), 0)
    j = _iota((BRANCH_WIDTH, PLACED_WIDTH), 1)
    side = j // (N_HEADS * LANES)
    head = (j // LANES) % N_HEADS
    c = j % LANES
    left = (side == 0) & (c < HEAD_DIM) & (r == head * HEAD_DIM + c)
    right = (side == 1) & (c >= HEAD_DIM) & (r == head * HEAD_DIM + c - HEAD_DIM)
    return jnp.where(left | right, 1.0, 0.0).astype(BF16)


def _placed_rows(qs):
    ql = jnp.concatenate([qs[:, h * LANES:(h + 1) * LANES] for h in range(N_HEADS)], axis=0)
    qr = jnp.concatenate([qs[:, (N_HEADS + h) * LANES:(N_HEADS + h + 1) * LANES] for h in range(N_HEADS)], axis=0)
    return ql, qr


NSA_SELECT_TQ = 1024
CHUNK_BLOCKS = NSA_KT // NSA_BLOCK


def _nsa_select_kernel(qs_ref, kc_ref, vc_ref, oc_ref, sel_ref, hit_ref):
    tq = qs_ref.shape[0]
    rows = N_HEADS * tq
    q0 = pl.program_id(1) * tq
    nbl = kc_ref.shape[0]
    ql, _ = _placed_rows(qs_ref[...])
    qpos = q0 + _iota((rows, 1), 0) % tq
    o_c, p_c = _nsa_cmp_branch(ql, kc_ref[...], vc_ref[...], qpos, q0, _row_slopes(tq))
    oc_ref[...] = _fold_heads(o_c, tq)
    imp = p_c[0:tq]
    for h in range(1, N_HEADS):
        imp = imp + p_c[h * tq:(h + 1) * tq]
    sel = _nsa_select(imp, q0 + _iota((tq, 1), 0), nbl)
    sel_ref[...] = sel.astype(BF16)
    union = jnp.max(sel.reshape(tq // Q_BLOCK, Q_BLOCK, nbl), axis=1)
    chunk_of = jnp.where(_iota((nbl, LANES), 0) // CHUNK_BLOCKS == _iota((nbl, LANES), 1), 1.0, 0.0)
    hit_ref[...] = _dot(union.astype(BF16), chunk_of.astype(BF16))


def _nsa_select_prompt(qs, kc, vc):
    b, t, _ = qs.shape
    nb = kc.shape[1]
    tq = min(NSA_SELECT_TQ, t)
    per_b = lambda r, w: pl.BlockSpec((None, r, w), lambda bi, i: (bi, 0, 0))
    tile = lambda r, w: pl.BlockSpec((None, r, w), lambda bi, i: (bi, i, 0))
    return pl.pallas_call(
        _nsa_select_kernel,
        grid=(b, t // tq),
        in_specs=[tile(tq, PLACED_WIDTH), per_b(nb, LANES), per_b(nb, BRANCH_WIDTH)],
        out_specs=[tile(tq, BRANCH_WIDTH), tile(tq, nb), tile(tq // Q_BLOCK, LANES)],
        out_shape=[jax.ShapeDtypeStruct((b, t, BRANCH_WIDTH), F32), jax.ShapeDtypeStruct((b, t, nb), BF16),
                   jax.ShapeDtypeStruct((b, t // Q_BLOCK, LANES), F32)],
        compiler_params=_cparams("parallel", "parallel"),
        name="nsa_select",
    )(qs, kc, vc)


def _nsa_prompt_kernel(hit_ref, qs_ref, gate_ref, sel_ref, oc_ref, kb_ref, vs_ref, wb_ref, vw_ref, o_ref,
                       m_ref, l_ref, acc_ref, *, n_chunks_all):
    i = pl.program_id(1)
    rq = Q_BLOCK
    rows = N_HEADS * rq
    q0 = i * rq
    nbl = sel_ref.shape[1]
    ql, qr = _placed_rows(qs_ref[...])
    slopes = _row_slopes(rq)
    qpos = q0 + _iota((rows, 1), 0) % rq
    qpos_q = q0 + _iota((rq, 1), 0)
    sel = sel_ref[...]
    hit_base = (pl.program_id(0) * pl.num_programs(1) + i) * n_chunks_all

    span = NSA_WINDOW + rq
    wstart = pl.multiple_of(q0, rq)
    kpos_w = q0 - NSA_WINDOW + _iota((1, span), 1)
    dist = qpos - kpos_w
    valid_w = (dist >= 0) & (dist < NSA_WINDOW) & (kpos_w >= 0)
    s_w = _dot_nt(ql, wb_ref[pl.ds(wstart, span), :]) + slopes * (kpos_w - q0).astype(F32)
    o_w = _dot(_softmax_rows(jnp.where(valid_w, s_w, NEG)).astype(BF16), vw_ref[pl.ds(wstart, span), :])
    gl = gate_ref[...]
    o_ref[...] = _gate_expand(gl, 0) * oc_ref[...] + _gate_expand(gl, 2) * _fold_heads(o_w, rq)

    m_ref[...] = jnp.full(m_ref.shape, NEG, F32)
    l_ref[...] = jnp.zeros_like(l_ref)
    acc_ref[...] = jnp.zeros_like(acc_ref)
    n_chunks = (q0 + rq + NSA_KT - 1) // NSA_KT

    def body(j, _):
        @pl.when(hit_ref[hit_base + j] > 0)
        def _():
            start = pl.multiple_of(j * NSA_KT, NSA_KT)
            chosen = _dot(sel, _block_expand_matrix(nbl, start, NSA_KT))
            s = _dot_nt(qr, kb_ref[pl.ds(start, NSA_KT), :])
            v = vs_ref[pl.ds(start, NSA_KT), :]
            kpos = start + _iota((1, NSA_KT), 1)
            ok = (chosen > 0.5) & (kpos <= qpos_q)
            rel = (kpos - q0).astype(F32)
            for h in range(N_HEADS):
                rs = slice(h * rq, (h + 1) * rq)
                s_h = jnp.where(ok, s[rs] + ALIBI_SLOPES[h] * rel, NEG)
                m_old = m_ref[rs]
                m_new = jnp.maximum(m_old, jnp.max(s_h, axis=1, keepdims=True))
                alpha = jnp.exp(m_old - m_new)
                p = jnp.where(ok, jnp.exp(s_h - m_new), 0.0)
                l_ref[rs] = alpha * l_ref[rs] + jnp.sum(p, axis=1, keepdims=True)
                acc_ref[rs] = alpha * acc_ref[rs] + _dot(p.astype(BF16), v)
                m_ref[rs] = m_new

        return 0

    lax.fori_loop(0, n_chunks, body, 0)
    o_ref[...] += _gate_expand(gl, 1) * _fold_heads(acc_ref[...] / l_ref[...], rq)


def _nsa_prompt(qs, u3, kc, vc, newk_b, vslc, win_b_pad, vwin_pad):
    b, t, _ = qs.shape
    nb = kc.shape[1]
    rows = N_HEADS * Q_BLOCK
    n_chunks_all = -(-t // NSA_KT)
    o_c, sel, hits = _nsa_select_prompt(qs, kc, vc)
    hit_flags = (hits[..., :n_chunks_all] > 0.5).astype(jnp.int32).reshape(-1)
    per_b = lambda r, w: pl.BlockSpec((None, r, w), lambda bi, i, hf: (bi, 0, 0))
    tile = lambda w, col=0: pl.BlockSpec((None, Q_BLOCK, w), lambda bi, i, hf: (bi, i, col))
    grid_spec = pltpu.PrefetchScalarGridSpec(
        num_scalar_prefetch=1,
        grid=(b, t // Q_BLOCK),
        in_specs=[tile(PLACED_WIDTH), tile(LANES, SEG_NSA_GATE // LANES), tile(nb), tile(BRANCH_WIDTH),
                  per_b(t, LANES), per_b(t, BRANCH_WIDTH),
                  per_b(t + NSA_WINDOW, LANES), per_b(t + NSA_WINDOW, BRANCH_WIDTH)],
        out_specs=tile(BRANCH_WIDTH),
        scratch_shapes=[pltpu.VMEM((rows, 1), F32), pltpu.VMEM((rows, 1), F32), pltpu.VMEM((rows, BRANCH_WIDTH), F32)],
    )
    return pl.pallas_call(
        functools.partial(_nsa_prompt_kernel, n_chunks_all=n_chunks_all),
        grid_spec=grid_spec,
        out_shape=jax.ShapeDtypeStruct((b, t, BRANCH_WIDTH), F32),
        compiler_params=_cparams("parallel", "parallel"),
        name="nsa_prompt",
    )(hit_flags, qs, u3, sel, o_c, newk_b, vslc, win_b_pad, vwin_pad)


NSA_PAGES = 8


def _nsa_sample_kernel(pt_ref, ql_ref, qr_ref, gate_ref, kn_ref, vn_ref, wbuf_ref, wn_ref, vwn_ref,
                       pek_ref, pev_ref, wk_ref, wv_ref, kg_ref, *refs, n_steps, n_new):
    k_pages = refs[:NSA_PAGES]
    v_pages = refs[NSA_PAGES:2 * NSA_PAGES]
    o_ref, kbuf_ref, vbuf_ref, kst_ref, vst_ref = refs[2 * NSA_PAGES:]
    s = pl.program_id(1)
    for r in range(NSA_PAGES):
        row0 = pl.multiple_of((s * NSA_PAGES + r) * PAGE_SIZE, PAGE_SIZE)
        kbuf_ref[pl.ds(row0, PAGE_SIZE), :] = k_pages[r][...]
        vbuf_ref[pl.ds(row0, PAGE_SIZE), :] = v_pages[r][...]

    @pl.when(s == n_steps - 1)
    def _():
        rq = SAMPLE_ROWS
        rows = N_HEADS * rq
        n_pages = kbuf_ref.shape[0] // PAGE_SIZE
        past = n_pages * PAGE_SIZE
        nb = past // NSA_BLOCK
        n_buf = wbuf_ref.shape[0]
        ql = ql_ref[...]
        qr = qr_ref[...]
        slopes = _row_slopes(rq)
        step = _iota((rows, 1), 0) % rq
        qpos = past + step
        step_q = _iota((rq, 1), 0)
        rep = _rep_right_matrix()
        tile = lambda a: jnp.concatenate([a] * N_HEADS, axis=0)

        kc, vc = _nsa_compress_pages(kbuf_ref, vbuf_ref, pek_ref, pev_ref, wk_ref, wv_ref, kg_ref, n_pages)
        order = lambda idx: 2 * (idx % n_pages) + idx // n_pages
        o_c, p_c = _nsa_cmp_branch(ql, kc, vc, qpos, past, slopes, blk=order(_iota((1, nb), 1)))
        imp = p_c[0:rq]
        for h in range(1, N_HEADS):
            imp = imp + p_c[h * rq:(h + 1) * rq]
        width = -(-(nb + 1) // LANES) * LANES
        imp = jnp.concatenate([imp, jnp.zeros((rq, width - nb), F32)], axis=1)
        lane = _iota((1, width), 1)
        sel = _nsa_select(imp, past + step_q, width, blk=jnp.where(lane < nb, order(lane), lane))

        for pg in range(n_pages):
            cols = slice(pg * PAGE_SIZE, (pg + 1) * PAGE_SIZE)
            slc_rows = slice(pg * PAGE_SIZE + HEAD_DIM, (pg + 1) * PAGE_SIZE)
            kst_ref[:, cols] = kbuf_ref[slc_rows, :].astype(BF16)
            vst_ref[:, cols] = vbuf_ref[slc_rows, :].astype(BF16)
        col = _iota((1, PAGE_SIZE), 1)
        expand = _block_expand_matrix(nb, 0, past, blk_col=order(_iota((nb, past), 0)))
        ok_p = tile(_dot(sel[:, 0:nb].astype(BF16), expand) > 0.5)
        s_p = _dot(ql[:, 0:HEAD_DIM], kst_ref[...]) + slopes * (_iota((1, past), 1) - past).astype(F32)
        s_p = jnp.where(ok_p, s_p, NEG)
        ok_n = tile(sel[:, nb:nb + 1] > 0.5) & (col <= step) & (col < n_new)
        s_n = jnp.where(ok_n, _dot_nt(qr, kn_ref[...]) + slopes * col.astype(F32), NEG)
        m = jnp.maximum(jnp.max(s_p, axis=1, keepdims=True), jnp.max(s_n, axis=1, keepdims=True))
        p_p = jnp.where(ok_p, jnp.exp(s_p - m), 0.0)
        p_n = jnp.where(ok_n, jnp.exp(s_n - m), 0.0)
        den = jnp.sum(p_p, axis=1, keepdims=True) + jnp.sum(p_n, axis=1, keepdims=True)
        spread = jnp.where(_iota((HEAD_DIM, BRANCH_WIDTH), 0) == _iota((HEAD_DIM, BRANCH_WIDTH), 1) % HEAD_DIM, 1.0, 0.0)
        o_s = (_dot_hi(_dot_nt(p_p.astype(BF16), vst_ref[...]), spread)
               + _dot(p_n.astype(BF16), vn_ref[...])) / den

        wb = wbuf_ref[...].astype(BF16)
        cw = _iota((1, n_buf), 1)
        kpos_w = past - n_buf + cw
        dist = qpos - kpos_w
        ok_w = (dist >= 0) & (dist < NSA_WINDOW) & (kpos_w >= 0)
        s_w = jnp.where(ok_w, _dot_nt(ql, wb) + slopes * (cw - n_buf).astype(F32), NEG)
        ok_wn = (col <= step) & (col < n_new)
        s_wn = jnp.where(ok_wn, _dot_nt(ql, wn_ref[...]) + slopes * col.astype(F32), NEG)
        m = jnp.maximum(jnp.max(s_w, axis=1, keepdims=True), jnp.max(s_wn, axis=1, keepdims=True))
        p_w = jnp.where(ok_w, jnp.exp(s_w - m), 0.0)
        p_wn = jnp.where(ok_wn, jnp.exp(s_wn - m), 0.0)
        den = jnp.sum(p_w, axis=1, keepdims=True) + jnp.sum(p_wn, axis=1, keepdims=True)
        o_w = (_dot_hi(_dot(p_w.astype(BF16), wb), rep) + _dot(p_wn.astype(BF16), vwn_ref[...])) / den

        gl = gate_ref[...]
        o_ref[...] = (_gate_expand(gl, 0) * _fold_heads(o_c, rq) + _gate_expand(gl, 1) * _fold_heads(o_s, rq)
                      + _gate_expand(gl, 2) * _fold_heads(o_w, rq))


def _nsa_sample(page_table, ql, qr, gate, kn, vn, wbuf, wn, vwn, p, cache_k, cache_v, layer, n_pool, n_new):
    b, n_pages = page_table.shape
    n_steps = n_pages // NSA_PAGES
    past = n_pages * PAGE_SIZE
    base = layer * n_pool

    def page_spec(r):
        return pl.BlockSpec((None, PAGE_SIZE, LANES), lambda bi, s, pt: (base + pt[bi, s * NSA_PAGES + r], 0, 0))

    per_b = lambda a: pl.BlockSpec((None,) + a.shape[1:], lambda bi, s, pt: (bi, 0, 0))
    const = lambda a: pl.BlockSpec(a.shape, lambda bi, s, pt: (0,) * a.ndim)
    consts = [p["pe_k_t"], p["pe_v_t"], p["w_k_t"], p["w_v_t"], p["k_gain_pad"]]
    seq_ops = [ql, qr, gate, kn, vn, wbuf, wn, vwn]
    grid_spec = pltpu.PrefetchScalarGridSpec(
        num_scalar_prefetch=1,
        grid=(b, n_steps),
        in_specs=[per_b(a) for a in seq_ops] + [const(a) for a in consts]
                 + [page_spec(r) for r in range(NSA_PAGES)] * 2,
        out_specs=pl.BlockSpec((None, SAMPLE_ROWS, BRANCH_WIDTH), lambda bi, s, pt: (bi, 0, 0)),
        scratch_shapes=[pltpu.VMEM((past, LANES), F32), pltpu.VMEM((past, LANES), F32),
                        pltpu.VMEM((HEAD_DIM, past), BF16), pltpu.VMEM((HEAD_DIM, past), BF16)],
    )
    return pl.pallas_call(
        functools.partial(_nsa_sample_kernel, n_steps=n_steps, n_new=n_new),
        grid_spec=grid_spec,
        out_shape=jax.ShapeDtypeStruct((b, SAMPLE_ROWS, BRANCH_WIDTH), F32),
        compiler_params=_cparams("parallel", "arbitrary"),
        name="nsa_sample",
    )(page_table, *seq_ops, *consts, *([cache_k] * NSA_PAGES), *([cache_v] * NSA_PAGES))


def _nsa_sample_mix(u3, page_table, wbuf, p, cache_k, cache_v, layer, n_pool):
    b, t, _ = u3.shape
    qs, newk, newv, win, newk_b, win_b, vslc, vwin = _nsa_prep(u3.reshape(b * t, -1), p["q_gain"], p["k_gain"], b * t)
    q4 = qs.reshape(b, t, 2, N_HEADS, LANES).transpose(2, 0, 3, 1, 4)
    q4 = jnp.pad(q4, ((0, 0), (0, 0), (0, 0), (0, SAMPLE_ROWS - t), (0, 0)))
    ql, qr = q4.reshape(2, b, N_HEADS * SAMPLE_ROWS, LANES)
    gate = jnp.pad(u3[..., SEG_NSA_GATE:SEG_NSA_GATE + LANES], ((0, 0), (0, SAMPLE_ROWS - t), (0, 0)))
    page_rows = lambda a: jnp.pad(a.reshape(b, t, -1), ((0, 0), (0, PAGE_SIZE - t), (0, 0)))
    o = _nsa_sample(page_table, ql, qr, gate, page_rows(newk_b), page_rows(vslc), wbuf, page_rows(win_b),
                    page_rows(vwin), p, cache_k, cache_v, layer, n_pool, t)
    return o[:, :t], newk, newv, win


def _nsa_params(q_norm, k_norm, cmp_pe, cmp_w):
    lane_pad = lambda a: jnp.pad(a, ((0, 0),) * (a.ndim - 1) + ((0, LANES - HEAD_DIM),))
    wk = cmp_w[0].reshape(NSA_BLOCK, HEAD_DIM, HEAD_DIM)
    wv = cmp_w[1].reshape(NSA_BLOCK, HEAD_DIM, HEAD_DIM)
    pad_rows = lambda a: jnp.pad(a, ((0, 0), (0, LANES - HEAD_DIM), (0, 0)))

    def per_dim(w):
        base = w.transpose(1, 0, 2)
        width = base.shape[2]
        out = jnp.zeros((HEAD_DIM, PAGE_SIZE, 2 * width), F32)
        for c in range(PAGE_SIZE // NSA_BLOCK):
            out = out.at[:, c * NSA_BLOCK:(c + 1) * NSA_BLOCK, c * width:(c + 1) * width].set(base)
        return out.astype(BF16)

    return dict(q_gain=jnp.tile(q_norm, N_HEADS)[None, :], k_gain=jnp.tile(k_norm, 6)[None, :],
                k_gain_pad=lane_pad(k_norm[None, :]), pe_k=lane_pad(cmp_pe[0]), pe_v=lane_pad(cmp_pe[1]),
                w_k=pad_rows(lane_pad(wk)).astype(BF16), w_v=pad_rows(jnp.tile(wv, (1, 1, N_HEADS))).astype(BF16),
                pe_k_t=jnp.tile(cmp_pe[0].T, (1, 2)), pe_v_t=jnp.tile(cmp_pe[1].T, (1, 2)),
                w_k_t=per_dim(lane_pad(wk)), w_v_t=per_dim(jnp.tile(wv, (1, 1, N_HEADS))))


SSM_HIST = 8
GROUP_LANES = BRANCH_WIDTH // SSM_GROUPS


def _ssd_kernel(xbc_ref, z_ref, dt_ref, dtt_ref, hist0_ref, h0_ref, cw_ref, cb_ref, bias_ref, biast_ref,
                a_ref, at_ref, dskip_ref, ng_ref, y_ref, hout_ref, hist_ref, state_ref, *, n_valid):
    c = pl.program_id(1)
    q = xbc_ref.shape[0]

    @pl.when(c == 0)
    def _():
        hist_ref[0:SSM_HIST, :] = hist0_ref[...]
        state_ref[...] = h0_ref[...]

    hist_ref[SSM_HIST:SSM_HIST + q, :] = xbc_ref[...]
    full = hist_ref[...]
    conv = cw_ref[3:4, :] * full + cb_ref[...]
    for k in range(1, SSM_CONV):
        conv = conv + cw_ref[3 - k:4 - k, :] * pltpu.roll(full, k, axis=0)
    hist_ref[0:SSM_HIST, :] = full[q:q + SSM_HIST, :]
    act = _silu(conv[SSM_HIST:, :])
    xs = act[:, 0:BRANCH_WIDTH]
    xs_b = xs.astype(BF16)
    bm = act[:, BRANCH_WIDTH:BRANCH_WIDTH + SSM_GROUPS * SSM_STATE].astype(BF16)
    cm = act[:, BRANCH_WIDTH + SSM_GROUPS * SSM_STATE:].astype(BF16)

    dt = jnp.where(_iota((q, LANES), 0) < n_valid, _softplus(dt_ref[...] + bias_ref[...]), 0.0)
    dta = dt * -jnp.exp(a_ref[...])
    dtt = jnp.where(_iota((SSM_HIST, q), 1) < n_valid, _softplus(dtt_ref[...] + biast_ref[...]), 0.0)
    dtat = dtt * -jnp.exp(at_ref[...])
    tril = jnp.where(_iota((q, q), 0) >= _iota((q, q), 1), 1.0, 0.0)
    triu = jnp.where(_iota((q, q), 0) <= _iota((q, q), 1), 1.0, 0.0)
    expand = jnp.where(_iota((LANES, BRANCH_WIDTH), 0) == _iota((LANES, BRANCH_WIDTH), 1) // HEAD_DIM, 1.0, 0.0)
    dt_e = _dot_hi(dt, expand)
    cum = _dot_hi(tril, dta)
    cum_e = _dot_hi(cum, expand)
    cum_t = _dot_hi(dtat, triu)
    causal = _iota((q, q), 0) >= _iota((q, q), 1)
    lane_head = _iota((1, BRANCH_WIDTH), 1) // HEAD_DIM

    y = dskip_ref[...] * xs
    for g in range(SSM_GROUPS):
        sl = slice(g * SSM_STATE, (g + 1) * SSM_STATE)
        gram = _dot_nt(cm[:, sl], bm[:, sl])
        for h in range(g * (N_HEADS // SSM_GROUPS), (g + 1) * (N_HEADS // SSM_GROUPS)):
            pick = jnp.where(_iota((LANES, q), 0) == h, 1.0, 0.0)
            cum_l = _dot_hi(cum, pick)
            seg = cum_l - cum_t[h:h + 1, :]
            decay = jnp.where(causal, jnp.exp(jnp.minimum(seg, 0.0)), 0.0)
            scores = (gram * decay * dtt[h:h + 1, :]).astype(BF16)
            y = y + jnp.where(lane_head == h, _dot(scores, xs_b), 0.0)
    state = state_ref[...]
    inter = jnp.concatenate(
        [_dot(cm[:, g * SSM_STATE:(g + 1) * SSM_STATE],
              state[:, g * GROUP_LANES:(g + 1) * GROUP_LANES].astype(BF16)) for g in range(SSM_GROUPS)], axis=1)
    y = y + jnp.exp(cum_e) * inter

    cum_last = cum_e[q - 1:q, :]
    xw = (xs * jnp.exp(cum_last - cum_e) * dt_e).astype(BF16)
    contrib = jnp.concatenate(
        [lax.dot_general(bm[:, g * SSM_STATE:(g + 1) * SSM_STATE], xw[:, g * GROUP_LANES:(g + 1) * GROUP_LANES],
                         (((0,), (0,)), ((), ())), preferred_element_type=F32) for g in range(SSM_GROUPS)], axis=1)
    state_ref[...] = state * jnp.exp(cum_last) + contrib

    y = y * _silu(z_ref[...])
    parts = []
    for g in range(SSM_GROUPS):
        yg = y[:, g * GROUP_LANES:(g + 1) * GROUP_LANES]
        parts.append(yg * lax.rsqrt(jnp.mean(yg * yg, axis=-1, keepdims=True) + NORM_EPS))
    y_ref[...] = jnp.concatenate(parts, axis=1) * ng_ref[...]

    @pl.when(c == pl.num_programs(1) - 1)
    def _():
        hout_ref[...] = state_ref[...]


def _ssd(u3, dtt, hist0, h0, p, *, q, n_valid):
    b, t, _ = u3.shape
    per_b = lambda shape: pl.BlockSpec((None,) + shape, lambda bi, c: (bi, 0, 0))
    col = lambda width, seg: pl.BlockSpec((None, q, width), lambda bi, c: (bi, c, seg // width))
    consts = [p["conv_w"], p["conv_b"], p["dt_bias"], p["dt_bias_t"], p["a"], p["a_t"], p["d_skip"], p["norm_g"]]
    return pl.pallas_call(
        functools.partial(_ssd_kernel, n_valid=n_valid),
        grid=(b, t // q),
        in_specs=[col(SSM_CONV_DIM, SEG_SSM_XBC), col(BRANCH_WIDTH, SEG_SSM_Z), col(LANES, SEG_SSM_DT),
                  pl.BlockSpec((None, SSM_HIST, q), lambda bi, c: (bi, 0, c)),
                  per_b((SSM_HIST, SSM_CONV_DIM)), per_b((SSM_STATE, BRANCH_WIDTH))]
                 + [_const_spec(a.shape) for a in consts],
        out_specs=[pl.BlockSpec((None, q, BRANCH_WIDTH), lambda bi, c: (bi, c, 0)),
                   per_b((SSM_STATE, BRANCH_WIDTH))],
        out_shape=[jax.ShapeDtypeStruct((b, t, BRANCH_WIDTH), F32),
                   jax.ShapeDtypeStruct((b, SSM_STATE, BRANCH_WIDTH), F32)],
        scratch_shapes=[pltpu.VMEM((SSM_HIST + q, SSM_CONV_DIM), F32), pltpu.VMEM((SSM_STATE, BRANCH_WIDTH), F32)],
        compiler_params=_cparams("parallel", "arbitrary"),
        name="ssd",
    )(u3, u3, u3, dtt, hist0, h0, *consts)


def _head_sum_matrix(scale):
    same = _iota((BRANCH_WIDTH, BRANCH_WIDTH), 0) // HEAD_DIM == _iota((BRANCH_WIDTH, BRANCH_WIDTH), 1) // HEAD_DIM
    return jnp.where(same, scale, 0.0)


def _rwkv_prep_kernel(u_ref, up_ref, p1_ref, mu_ref, w0_ref, w2_ref, a0_ref, a2_ref, g2_ref, kk_ref, ka_ref, rk_ref,
                      r_out, k_out, v_out, d_out, nkk_out, kka_out, g_out, bonus_out, *, seq_tiles, period):
    tm = u_ref.shape[0]
    u = u_ref[...]
    rolled = pltpu.roll(u, 1, axis=0)
    if seq_tiles > 0:
        first = (pl.program_id(0) % seq_tiles) == 0
        carry_in = up_ref[FFN_HALO - 1:FFN_HALO, :] * jnp.where(first, 0.0, 1.0)
        prev = jnp.where(_iota((tm, 1), 0) == 0, carry_in, rolled)
    else:
        prev = jnp.where(_iota((tm, 1), 0) % period >= 1, rolled, p1_ref[...])
    us = u + (prev - u) * mu_ref[...]
    r = us[:, 0:BRANCH_WIDTH]
    k = us[:, BRANCH_WIDTH:2 * BRANCH_WIDTH]
    v = us[:, 2 * BRANCH_WIDTH:3 * BRANCH_WIDTH]
    wa = us[:, 3 * BRANCH_WIDTH:3 * BRANCH_WIDTH + LANES]
    gd = us[:, 3 * BRANCH_WIDTH + LANES:]
    is_w = _iota((1, LANES), 1) < RWKV_W_LORA
    w_lora = _dot(jnp.where(is_w, jnp.tanh(wa), 0.0).astype(BF16), w2_ref[...])
    a_lora = _dot(jnp.where(is_w, 0.0, wa).astype(BF16), a2_ref[...])
    w_raw = -_softplus(-(w0_ref[...] + w_lora)) - 0.5
    log_decay = -jnp.exp(w_raw)
    a = _sigmoid(a0_ref[...] + a_lora)
    g = _dot(_sigmoid(gd).astype(BF16), g2_ref[...])
    head_sum = _head_sum_matrix(1.0)
    kk = k * kk_ref[...]
    kk = kk / jnp.maximum(jnp.sqrt(_dot_hi(kk * kk, head_sum)), 1e-12)
    k_mod = k * (1.0 + (a - 1.0) * ka_ref[...])
    r_out[...] = r
    k_out[...] = k_mod
    v_out[...] = v
    d_out[...] = log_decay
    nkk_out[...] = -kk
    kka_out[...] = kk * a
    g_out[...] = g
    bonus_out[...] = _dot_hi(r * k_mod * rk_ref[...], head_sum) * v


def _rwkv_prep(u, p1, p, *, tm, seq_tiles, period):
    n = u.shape[0]
    hb = tm // FFN_HALO
    seg = SEG_RWKV // RWKV_IN
    if seq_tiles > 0:
        up_spec = pl.BlockSpec((FFN_HALO, RWKV_IN), lambda i: (jnp.maximum(i * hb - 1, 0), seg))
        p_spec = _const_spec(p1.shape)
    else:
        up_spec = pl.BlockSpec((FFN_HALO, RWKV_IN), lambda i: (0, seg))
        p_spec = pl.BlockSpec((tm, RWKV_IN), lambda i: (i, 0))
    consts = [p["mu"], p["w0"], p["w2"], p["a0"], p["a2"], p["g2"], p["k_k"], p["k_a"], p["r_k"]]
    out = pl.BlockSpec((tm, BRANCH_WIDTH), lambda i: (i, 0))
    return pl.pallas_call(
        functools.partial(_rwkv_prep_kernel, seq_tiles=seq_tiles, period=period),
        grid=(n // tm,),
        in_specs=[pl.BlockSpec((tm, RWKV_IN), lambda i: (i, seg)), up_spec, p_spec]
                 + [_const_spec(a.shape) for a in consts],
        out_specs=[out] * 8,
        out_shape=[jax.ShapeDtypeStruct((n, BRANCH_WIDTH), F32)] * 8,
        compiler_params=_cparams("parallel"),
        name="rwkv_prep",
    )(u, u, p1, *consts)


RWKV_CHAINS = 8
RWKV_TB = 128


def _rwkv_scan_kernel(r_ref, k_ref, d_ref, nkk_ref, kka_ref, vt_ref, s0_ref, yt_ref, sout_ref, s_ref, *, n_steps):
    @pl.when(pl.program_id(1) == 0)
    def _():
        s_ref[...] = s0_ref[...]

    yt_ref[...] = jnp.zeros_like(yt_ref)
    lane = _iota((HEAD_DIM, RWKV_TB), 1)

    def step(t, _):
        for c in range(RWKV_CHAINS):
            row = lambda ref: ref[c, pl.ds(t, 1), :]
            s = s_ref[c]
            sa = jnp.sum(s * row(nkk_ref), axis=1, keepdims=True)
            v_col = jnp.sum(jnp.where(lane == t, vt_ref[c], 0.0), axis=1, keepdims=True)
            s = s * jnp.exp(row(d_ref)) + sa * row(kka_ref) + v_col * row(k_ref)
            s_ref[c] = s
            y_col = jnp.sum(s * row(r_ref), axis=1, keepdims=True)
            yt_ref[c] = jnp.where(lane == t, y_col, yt_ref[c])
        return 0

    lax.fori_loop(0, n_steps, step, 0)

    @pl.when(pl.program_id(1) == pl.num_programs(1) - 1)
    def _():
        sout_ref[...] = s_ref[...]


def _rwkv_scan(r, k, d, nkk, kka, vt, s0, *, rows, n_steps):
    chains, t_rows, _ = r.shape
    t_lanes = vt.shape[2]
    row_spec = pl.BlockSpec((RWKV_CHAINS, rows, HEAD_DIM), lambda ci, tb: (ci, tb, 0))
    lane_spec = pl.BlockSpec((RWKV_CHAINS, HEAD_DIM, RWKV_TB), lambda ci, tb: (ci, 0, tb))
    state_spec = pl.BlockSpec((RWKV_CHAINS, HEAD_DIM, HEAD_DIM), lambda ci, tb: (ci, 0, 0))
    return pl.pallas_call(
        functools.partial(_rwkv_scan_kernel, n_steps=n_steps),
        grid=(chains // RWKV_CHAINS, t_lanes // RWKV_TB),
        in_specs=[row_spec] * 5 + [lane_spec, state_spec],
        out_specs=[lane_spec, state_spec],
        out_shape=[jax.ShapeDtypeStruct((chains, HEAD_DIM, t_lanes), F32),
                   jax.ShapeDtypeStruct((chains, HEAD_DIM, HEAD_DIM), F32)],
        scratch_shapes=[pltpu.VMEM((RWKV_CHAINS, HEAD_DIM, HEAD_DIM), F32)],
        compiler_params=_cparams("parallel", "arbitrary"),
        name="rwkv_scan",
    )(r, k, d, nkk, kka, vt, s0)


RWKV_CHUNK = 64


def _split2(x):
    hi = x.astype(BF16)
    return hi, (x - hi.astype(F32)).astype(BF16)


def _dot_split(a2, b2, dims=(((1,), (0,)), ((), ()))):
    (ah, al), (bh, bl) = a2, b2
    dg = lambda x, y: lax.dot_general(x, y, dims, preferred_element_type=F32)
    return dg(ah, bh) + dg(ah, bl) + dg(al, bh)


def _rwkv_chunk_step(r, k, v, ld, a, b, s):
    c = RWKV_CHUNK
    rows = N_HEADS * c
    cum = _dot_hi(jnp.where(_iota((c, c), 0) >= _iota((c, c), 1), 1.0, 0.0), ld)
    cum_last = cum[c - 1:c, :]
    g_inv = jnp.exp(-cum)
    g_end = jnp.exp(cum_last - cum)
    own = (_iota((rows, BRANCH_WIDTH), 0) // c) == (_iota((rows, BRANCH_WIDTH), 1) // HEAD_DIM)
    stack = lambda x: jnp.where(own, jnp.concatenate([x] * N_HEADS, axis=0), 0.0)
    tile = lambda x: jnp.concatenate([x] * N_HEADS, axis=0)
    ar = jnp.concatenate([stack(a * jnp.exp(cum - ld)), stack(r * jnp.exp(cum))], axis=0)
    bt = b * g_inv
    kt = k * g_inv
    nt = (((1,), (1,)), ((), ()))
    ar2 = _split2(ar)
    g_b = _dot_split(ar2, _split2(tile(bt)), nt)
    g_k = _dot_split(ar2, _split2(tile(kt)), nt)
    yield
    step_r = _iota((rows, rows), 0) % c
    step_c = _iota((rows, rows), 1) % c
    same = (_iota((rows, rows), 0) // c) == (_iota((rows, rows), 1) // c)
    strict = same & (step_c < step_r)
    incl = same & (step_c <= step_r)
    a_ab = jnp.where(strict, g_b[0:rows], 0.0)
    a_ak = jnp.where(strict, g_k[0:rows], 0.0)
    a_rb = jnp.where(incl, g_b[rows:], 0.0)
    a_rk = jnp.where(incl, g_k[rows:], 0.0)

    vbd = stack(v)
    vbd2 = _split2(vbd)
    w = _dot_split(ar2, _split2(s), nt)
    av = _dot_split(_split2(jnp.concatenate([a_ak, a_rk], axis=0)), vbd2)
    yield

    power2 = _split2(a_ab)
    u = w[0:rows] + av[0:rows]
    u = u + _dot_split(power2, _split2(u))
    for _ in range(int(math.log2(c)) - 1):
        yield
        power2 = _split2(_dot_split(power2, power2))
        u = u + _dot_split(power2, _split2(u))

    yield
    u2 = _split2(u)
    y = w[rows:] + av[rows:] + _dot_split(_split2(a_rb), u2)
    out = y[0:c]
    for h in range(1, N_HEADS):
        out = out + y[h * c:(h + 1) * c]
    cat2 = lambda p, q: (jnp.concatenate([p[0], q[0]], axis=0), jnp.concatenate([p[1], q[1]], axis=0))
    upd = _dot_split(cat2(u2, vbd2), cat2(_split2(stack(b * g_end)), _split2(stack(k * g_end))),
                     (((0,), (0,)), ((), ())))
    return out, s * jnp.exp(cum_last) + upd


def _rwkv_chunk_kernel(r_ref, k_ref, v_ref, ld_ref, a_ref, b_ref, y_ref, sout_ref, s_ref):
    @pl.when(pl.program_id(0) == 0)
    def _():
        s_ref[...] = jnp.zeros_like(s_ref)

    bsz = s_ref.shape[0]
    steps = [_rwkv_chunk_step(r_ref[bi], k_ref[bi], v_ref[bi], ld_ref[bi], a_ref[bi], b_ref[bi], s_ref[bi])
             for bi in range(bsz)]
    done = {}
    while len(done) < bsz:
        for bi in range(bsz):
            if bi not in done:
                try:
                    next(steps[bi])
                except StopIteration as fin:
                    done[bi] = fin.value
    for bi in range(bsz):
        y_ref[bi], s_ref[bi] = done[bi]

    @pl.when(pl.program_id(0) == pl.num_programs(0) - 1)
    def _():
        sout_ref[...] = s_ref[...]


def _rwkv_chunked(r, k, v, ld, a, b):
    bsz, t, _ = r.shape
    blk = pl.BlockSpec((bsz, RWKV_CHUNK, BRANCH_WIDTH), lambda ci: (0, ci, 0))
    st = pl.BlockSpec((bsz, BRANCH_WIDTH, BRANCH_WIDTH), lambda ci: (0, 0, 0))
    return pl.pallas_call(
        _rwkv_chunk_kernel,
        grid=(t // RWKV_CHUNK,),
        in_specs=[blk] * 6,
        out_specs=[blk, st],
        out_shape=[jax.ShapeDtypeStruct((bsz, t, BRANCH_WIDTH), F32),
                   jax.ShapeDtypeStruct((bsz, BRANCH_WIDTH, BRANCH_WIDTH), F32)],
        scratch_shapes=[pltpu.VMEM((bsz, BRANCH_WIDTH, BRANCH_WIDTH), F32)],
        compiler_params=_cparams("arbitrary"),
        name="rwkv_chunk",
    )(r, k, v, ld, a, b)


def _rwkv_post_kernel(y_ref, bonus_ref, g_ref, lnw_ref, lnb_ref, o_ref):
    y = y_ref[...]
    head_mean = _head_sum_matrix(1.0 / HEAD_DIM)
    cen = y - _dot_hi(y, head_mean)
    var = _dot_hi(cen * cen, head_mean)
    yn = cen * lax.rsqrt(var + RWKV_LN_EPS) * lnw_ref[...] + lnb_ref[...]
    o_ref[...] = (yn + bonus_ref[...]) * g_ref[...]


def _rwkv_post(y, bonus, g, ln_w, ln_b, tm):
    n = y.shape[0]
    blk = pl.BlockSpec((tm, BRANCH_WIDTH), lambda i: (i, 0))
    return pl.pallas_call(
        _rwkv_post_kernel,
        grid=(n // tm,),
        in_specs=[blk, blk, blk, _const_spec(ln_w.shape), _const_spec(ln_b.shape)],
        out_specs=blk,
        out_shape=jax.ShapeDtypeStruct((n, BRANCH_WIDTH), F32),
        compiler_params=_cparams("parallel"),
        name="rwkv_post",
    )(y, bonus, g, ln_w, ln_b)


def _rwkv_mix(u2, p1, s0, p, ln_w, ln_b, *, b, t, tm, seq_tiles, period):
    r, k, v, d, nkk, kka, g, bonus = _rwkv_prep(u2, p1, p, tm=tm, seq_tiles=seq_tiles, period=period)
    if s0 is None:
        r3 = lambda x: x.reshape(b, t, BRANCH_WIDTH)
        y, s_wide = _rwkv_chunked(r3(r), r3(k), r3(v), r3(d), r3(nkk), r3(kka))
        s_wide = s_wide.reshape(b, N_HEADS, HEAD_DIM, N_HEADS, HEAD_DIM)
        s_fin = jnp.stack([s_wide[:, h, :, h, :] for h in range(N_HEADS)], axis=1)
        o = _rwkv_post(y.reshape(b * t, BRANCH_WIDTH), bonus, g, ln_w, ln_b, tm)
        return o, s_fin.reshape(b * N_HEADS, HEAD_DIM, HEAD_DIM)
    t_rows = max(t, SAMPLE_ROWS)
    t_lanes = -(-t // RWKV_TB) * RWKV_TB

    def chains(a):
        a = a.reshape(b, t, N_HEADS, HEAD_DIM).transpose(0, 2, 1, 3).reshape(b * N_HEADS, t, HEAD_DIM)
        return a

    rows = [jnp.pad(chains(a), ((0, 0), (0, t_rows - t), (0, 0))) for a in (r, k, d, nkk, kka)]
    vt = jnp.pad(chains(v).transpose(0, 2, 1), ((0, 0), (0, 0), (0, t_lanes - t)))
    yt, s_fin = _rwkv_scan(*rows, vt, s0, rows=min(t_rows, RWKV_TB), n_steps=min(t, RWKV_TB))
    y = yt[:, :, :t].transpose(0, 2, 1).reshape(b, N_HEADS, t, HEAD_DIM).transpose(0, 2, 1, 3)
    o = _rwkv_post(y.reshape(b * t, BRANCH_WIDTH), bonus, g, ln_w, ln_b, tm)
    return o, s_fin


def _rwkv_params(mu, w0, w2, a0, a2, g2, k_k, k_a, r_k):
    row = lambda v: v.reshape(1, -1)
    zeros = jnp.zeros((RWKV_W_LORA, BRANCH_WIDTH), F32)
    return dict(mu=row(mu), w0=row(w0), w2=jnp.concatenate([w2, zeros]).astype(BF16), a0=row(a0),
                a2=jnp.concatenate([zeros, a2]).astype(BF16), g2=g2.astype(BF16), k_k=row(k_k), k_a=row(k_a),
                r_k=row(r_k))


def _ssd_params(conv_w, conv_b, dt_bias, a_log, d_skip, norm_g):
    a = a_log.astype(F32)
    pad_row = lambda v: jnp.pad(v, (0, LANES - N_HEADS))[None, :]
    pad_col = lambda v: jnp.pad(v, (0, SSM_HIST - N_HEADS))[:, None]
    return dict(conv_w=conv_w, conv_b=conv_b[None, :], dt_bias=pad_row(dt_bias), dt_bias_t=pad_col(dt_bias),
                a=pad_row(a), a_t=pad_col(a), d_skip=jnp.repeat(d_skip, HEAD_DIM)[None, :], norm_g=norm_g[None, :])


PROMPT_TM = 512
INPROJ_TM = 1024


def _sb_sample_mix(u3, page_table, cache_k, cache_v, layer, n_pool):
    b, t, _ = u3.shape
    q = u3[..., SEG_SB:SEG_SB + BRANCH_WIDTH]
    k = u3[..., SEG_SB + BRANCH_WIDTH:SEG_SB + 2 * BRANCH_WIDTH]
    v = u3[..., SEG_SB + 2 * BRANCH_WIDTH:SEG_SB + 3 * BRANCH_WIDTH]
    qt = jnp.tile(jnp.pad(q, ((0, 0), (0, SAMPLE_ROWS - t), (0, 0))), (1, N_HEADS, 1))
    rows = N_HEADS * SAMPLE_ROWS
    own = (np.arange(rows)[:, None] // SAMPLE_ROWS) == (np.arange(BRANCH_WIDTH)[None, :] // HEAD_DIM)
    qbd = jnp.where(own[None], qt, 0.0)
    page_rows = lambda a: jnp.pad(a, ((0, 0), (0, PAGE_SIZE - t), (0, 0)))
    o = _sb_sample(page_table, qbd, page_rows(k), page_rows(v), cache_k, cache_v, layer, n_pool)
    return o[:, :t]


def kernel(x_prompt, x_sample, cache_nsa_k, cache_nsa_v, cache_sb_k, cache_sb_v, state_win_kv, state_ssm_conv, state_ssm, state_rwkv_shift, state_rwkv, state_ffn_conv, page_table, norm1, norm2, w_in, nsa_q_norm, nsa_k_norm, nsa_cmp_pe, nsa_cmp_w, ssm_conv_w, ssm_conv_b, ssm_dt_bias, ssm_a_log, ssm_d, ssm_norm, rwkv_mu, rwkv_w0, rwkv_w2, rwkv_a0, rwkv_a2, rwkv_g2, rwkv_k_k, rwkv_k_a, rwkv_r_k, rwkv_ln_w, rwkv_ln_b, w_branch, w_out, ffn_up, ffn_conv_w, ffn_conv_b, ffn_down):
    bp, tp, _ = x_prompt.shape
    bs, ts, _ = x_sample.shape
    depth, n_pool = cache_nsa_k.shape[:2]
    n_pages = page_table.shape[1]
    past = n_pages * PAGE_SIZE
    n_buf = state_win_kv.shape[2]
    n_p, n_s = bp * tp, bs * ts
    assert tp % INPROJ_TM == 0 and tp >= NSA_WINDOW and past % NSA_BLOCK == 0 and ts < SAMPLE_ROWS
    assert n_pages % SB_PAGES == 0 and n_pages % NSA_PAGES == 0 and n_s % FFN_HALO == 0

    perm = _in_perm()
    w_in_p = jnp.where(perm >= 0, jnp.take(w_in, np.maximum(perm, 0), axis=2), 0.0).astype(BF16)
    token_minor = lambda c: jnp.transpose(c, (0, 1, 3, 4, 2)).reshape(depth * n_pool, -1, PAGE_SIZE)
    nsa_ck = token_minor(cache_nsa_k)
    nsa_cv = token_minor(cache_nsa_v)
    sb_ck = token_minor(cache_sb_k)
    sb_cv = token_minor(cache_sb_v)
    dummy = jnp.zeros((FFN_HALO, LANES), F32)
    seg = lambda u, start, width: u[..., start:start + width]

    xp = x_prompt.reshape(n_p, D_MODEL)
    xs = x_sample.reshape(n_s, D_MODEL)
    outs = [[] for _ in range(20)]
    for l in range(depth):
        nsa_p = _nsa_params(nsa_q_norm[l], nsa_k_norm[l], nsa_cmp_pe[l], nsa_cmp_w[l])
        ssd_p = _ssd_params(ssm_conv_w[l], ssm_conv_b[l], ssm_dt_bias[l], ssm_a_log[l], ssm_d[l], ssm_norm[l])
        rwkv_p = _rwkv_params(rwkv_mu[l], rwkv_w0[l], rwkv_w2[l], rwkv_a0[l], rwkv_a2[l], rwkv_g2[l],
                              rwkv_k_k[l], rwkv_k_a[l], rwkv_r_k[l].reshape(-1))
        ln_w, ln_b = rwkv_ln_w[l][None, :], rwkv_ln_b[l][None, :]
        wb, wo = w_branch[l].astype(BF16), w_out[l].astype(BF16)
        wup, wdn = ffn_up[l].astype(BF16), ffn_down[l].astype(BF16)

        up = _inproj(xp, norm1[l][None, :], w_in_p[l], INPROJ_TM)
        u3 = up.reshape(bp, tp, D_IN_PAD)
        r3 = lambda a: a.reshape(bp, tp, -1)
        qs, newk, newv, win, newk_b, win_b, vslc, vwin = _nsa_prep(up, nsa_p["q_gain"], nsa_p["k_gain"], PROMPT_TM)
        kc, vc = _nsa_compress_prompt(r3(newk), r3(newv), nsa_p)
        front = lambda a: jnp.pad(r3(a), ((0, 0), (NSA_WINDOW, 0), (0, 0)))
        o_a = _nsa_prompt(r3(qs), u3, kc, vc, r3(newk_b), r3(vslc), front(win_b), front(vwin))
        dtt = jnp.pad(jnp.swapaxes(seg(u3, SEG_SSM_DT, N_HEADS), 1, 2), ((0, 0), (0, SSM_HIST - N_HEADS), (0, 0)))
        o_b, ssm_fin = _ssd(u3, dtt, jnp.zeros((bp, SSM_HIST, SSM_CONV_DIM), F32),
                            jnp.zeros((bp, SSM_STATE, BRANCH_WIDTH), F32), ssd_p, q=SSM_CHUNK, n_valid=SSM_CHUNK)
        o_c, rwkv_fin = _rwkv_mix(up, dummy, None, rwkv_p, ln_w, ln_b,
                                  b=bp, t=tp, tm=PROMPT_TM, seq_tiles=tp // PROMPT_TM, period=0)
        sb_k = seg(u3, SEG_SB + BRANCH_WIDTH, BRANCH_WIDTH)
        sb_v = seg(u3, SEG_SB + 2 * BRANCH_WIDTH, BRANCH_WIDTH)
        o_d = _sb_prompt(u3, sb_k.astype(BF16), sb_v.astype(BF16))
        flat = lambda a: a.reshape(n_p, BRANCH_WIDTH)
        xp = _merge(xp, [flat(o_a), flat(o_b), o_c, flat(o_d)], up, wb, wo, PROMPT_TM)
        xp, hs = _ffn(xp, dummy, dummy, norm2[l][None, :], wup, ffn_conv_w[l], ffn_conv_b[l][None, :], wdn,
                      tm=PROMPT_TM, seq_tiles=tp // PROMPT_TM, period=0, keep=FFN_HALO)
        prompt_states = (
            r3(newk).reshape(bp, tp, 2, HEAD_DIM), r3(newv).reshape(bp, tp, 2, HEAD_DIM),
            sb_k.reshape(bp, tp, N_HEADS, HEAD_DIM), sb_v.reshape(bp, tp, N_HEADS, HEAD_DIM),
            r3(win)[:, tp - min(NSA_WINDOW, tp):].reshape(bp, -1, 2, HEAD_DIM),
            seg(u3, SEG_SSM_XBC, SSM_CONV_DIM)[:, tp - (SSM_CONV - 1):],
            ssm_fin.reshape(bp, SSM_STATE, N_HEADS, HEAD_DIM).transpose(0, 2, 3, 1),
            seg(u3, SEG_RWKV, RWKV_IN)[:, tp - 1:],
            rwkv_fin.reshape(bp, N_HEADS, HEAD_DIM, HEAD_DIM),
            hs.reshape(bp, tp // PROMPT_TM, FFN_HALO, 2 * D_FF)[:, -1, FFN_HALO - (FFN_CONV - 1):])

        us = _inproj(xs, norm1[l][None, :], w_in_p[l], n_s)
        u3 = us.reshape(bs, ts, D_IN_PAD)
        o_a, newk, newv, win = _nsa_sample_mix(u3, page_table, state_win_kv[l].reshape(bs, n_buf, LANES), nsa_p,
                                               nsa_ck, nsa_cv, l, n_pool)
        t_pad = 2 * SAMPLE_ROWS
        u3_pad = jnp.pad(u3, ((0, 0), (0, t_pad - ts), (0, 0)))
        dtt = jnp.pad(jnp.swapaxes(seg(u3_pad, SEG_SSM_DT, N_HEADS), 1, 2), ((0, 0), (0, SSM_HIST - N_HEADS), (0, 0)))
        hist0 = jnp.pad(state_ssm_conv[l], ((0, 0), (SSM_HIST - (SSM_CONV - 1), 0), (0, 0)))
        h0 = state_ssm[l].transpose(0, 3, 1, 2).reshape(bs, SSM_STATE, BRANCH_WIDTH)
        o_b, ssm_fin = _ssd(u3_pad, dtt, hist0, h0, ssd_p, q=t_pad, n_valid=ts)
        shift_rows = jnp.pad(state_rwkv_shift[l], ((0, 0), (0, ts - 1), (0, 0))).reshape(n_s, RWKV_IN)
        o_c, rwkv_fin = _rwkv_mix(us, shift_rows, state_rwkv[l].reshape(bs * N_HEADS, HEAD_DIM, HEAD_DIM), rwkv_p,
                                  ln_w, ln_b, b=bs, t=ts, tm=n_s, seq_tiles=0, period=ts)
        o_d = _sb_sample_mix(u3, page_table, sb_ck, sb_cv, l, n_pool)
        flat = lambda a: a.reshape(n_s, BRANCH_WIDTH)
        xs = _merge(xs, [flat(o_a), flat(o_b[:, :ts]), o_c, flat(o_d)], us, wb, wo, n_s)
        conv_state = state_ffn_conv[l]
        prev1 = jnp.pad(conv_state[:, 1:2], ((0, 0), (0, ts - 1), (0, 0))).reshape(n_s, 2 * D_FF)
        prev2 = jnp.pad(conv_state, ((0, 0), (0, ts - 2), (0, 0))).reshape(n_s, 2 * D_FF)
        xs, hs = _ffn(xs, prev1, prev2, norm2[l][None, :], wup, ffn_conv_w[l], ffn_conv_b[l][None, :], wdn,
                      tm=n_s, seq_tiles=0, period=ts, keep=n_s)
        tail = lambda old, new, n: jnp.concatenate([old, new], axis=1)[:, -n:]
        sample_states = (
            newk.reshape(bs, ts, 2, HEAD_DIM), newv.reshape(bs, ts, 2, HEAD_DIM),
            seg(u3, SEG_SB + BRANCH_WIDTH, BRANCH_WIDTH).reshape(bs, ts, N_HEADS, HEAD_DIM),
            seg(u3, SEG_SB + 2 * BRANCH_WIDTH, BRANCH_WIDTH).reshape(bs, ts, N_HEADS, HEAD_DIM),
            tail(state_win_kv[l], win.reshape(bs, ts, 2, HEAD_DIM), n_buf),
            tail(state_ssm_conv[l], seg(u3, SEG_SSM_XBC, SSM_CONV_DIM), SSM_CONV - 1),
            ssm_fin.reshape(bs, SSM_STATE, N_HEADS, HEAD_DIM).transpose(0, 2, 3, 1),
            seg(u3, SEG_RWKV, RWKV_IN)[:, ts - 1:],
            rwkv_fin.reshape(bs, N_HEADS, HEAD_DIM, HEAD_DIM),
            tail(conv_state, hs.reshape(bs, ts, 2 * D_FF), FFN_CONV - 1))
        for j in range(10):
            outs[2 * j].append(prompt_states[j])
            outs[2 * j + 1].append(sample_states[j])

    return (xp.reshape(bp, tp, D_MODEL), xs.reshape(bs, ts, D_MODEL)) + tuple(jnp.stack(o) for o in outs)
```

```python
import functools
import math

import numpy as np
import jax
import jax.numpy as jnp
from jax import lax
from jax.experimental import pallas as pl
from jax.experimental.pallas import tpu as pltpu

F32 = jnp.float32
BF16 = jnp.bfloat16
HIGHEST = lax.Precision.HIGHEST

D_MODEL = 1024
N_BRANCH = 4
BRANCH_WIDTH = D_MODEL // N_BRANCH
HEAD_DIM = 64
N_HEADS = BRANCH_WIDTH // HEAD_DIM
Q_BLOCK = 128
PAGE_SIZE = 128
NORM_EPS = 1e-6
NEG = -1e30
NSA_BLOCK = 64
NSA_TOPK = 16
NSA_WINDOW = 512
NSA_FORCED = 2.0 * N_HEADS
SSM_GROUPS = 2
SSM_STATE = 128
SSM_CONV = 4
SSM_CHUNK = 128
SSM_CONV_DIM = BRANCH_WIDTH + 2 * SSM_GROUPS * SSM_STATE
RWKV_W_LORA = 64
RWKV_A_LORA = 64
RWKV_G_LORA = 128
RWKV_IN = 3 * BRANCH_WIDTH + RWKV_W_LORA + RWKV_A_LORA + RWKV_G_LORA
RWKV_LN_EPS = 64e-5
D_FF = 2816
FFN_CONV = 3

LANES = 128
VMEM_LIMIT = 56 * 1024 * 1024

SEG_MERGE = 0
SEG_RWKV = 4096
SEG_NSA_Q = 5120
SEG_SSM_XBC = 5376
SEG_SB = 6144
SEG_NSA_KV = 6912
SEG_NSA_GATE = 7296
SEG_SSM_Z = 7424
SEG_SSM_DT = 7680
D_IN_PAD = 8192


def _in_perm():
    sizes = (BRANCH_WIDTH, 6 * HEAD_DIM, 3 * N_HEADS, BRANCH_WIDTH, SSM_CONV_DIM, N_HEADS, RWKV_IN,
             3 * BRANCH_WIDTH, N_BRANCH * D_MODEL)
    off = np.concatenate([[0], np.cumsum(sizes)])
    o_q, o_kv, o_gate, o_z, o_xbc, o_dt, o_rwkv, o_sb, o_merge = off[:-1]
    perm = -np.ones((D_IN_PAD,), np.int64)
    perm[SEG_MERGE:SEG_MERGE + 4096] = o_merge + np.arange(4096)
    perm[SEG_RWKV:SEG_RWKV + RWKV_IN] = o_rwkv + np.arange(RWKV_IN)
    perm[SEG_NSA_Q:SEG_NSA_Q + 256] = o_q + np.arange(256)
    perm[SEG_SSM_XBC:SEG_SSM_XBC + 768] = o_xbc + np.arange(768)
    perm[SEG_SB:SEG_SB + 768] = o_sb + np.arange(768)
    kv_order = (0, 2, 1, 3, 4, 5)
    for j, src in enumerate(kv_order):
        perm[SEG_NSA_KV + 64 * j:SEG_NSA_KV + 64 * (j + 1)] = o_kv + 64 * src + np.arange(64)
    perm[SEG_NSA_GATE:SEG_NSA_GATE + 12] = o_gate + np.arange(12)
    perm[SEG_SSM_Z:SEG_SSM_Z + 256] = o_z + np.arange(256)
    perm[SEG_SSM_DT:SEG_SSM_DT + 4] = o_dt + np.arange(4)
    return perm


def _cparams(*sem):
    return pltpu.CompilerParams(dimension_semantics=tuple(sem), vmem_limit_bytes=VMEM_LIMIT)


def _const_spec(shape):
    nd = len(shape)
    return pl.BlockSpec(shape, lambda *_: (0,) * nd)


def _iota(shape, dim):
    return lax.broadcasted_iota(jnp.int32, shape, dim)


def _dot(a, b):
    return jnp.dot(a, b, preferred_element_type=F32)


def _dot_hi(a, b):
    return jnp.dot(a, b, preferred_element_type=F32, precision=HIGHEST)


def _dot_nt(a, b):
    return lax.dot_general(a, b, (((1,), (1,)), ((), ())), preferred_element_type=F32)


def _sigmoid(x):
    return 1.0 / (1.0 + jnp.exp(-x))


def _silu(x):
    return x * _sigmoid(x)


def _softplus(x):
    return jnp.maximum(x, 0.0) + jnp.log(1.0 + jnp.exp(-jnp.abs(x)))


def _inproj_kernel(x_ref, g_ref, w_ref, o_ref, xn_ref):
    @pl.when(pl.program_id(1) == 0)
    def _():
        x = x_ref[...]
        ms = jnp.mean(x * x, axis=-1, keepdims=True)
        xn_ref[...] = (x * lax.rsqrt(ms + NORM_EPS) * g_ref[...]).astype(BF16)

    o_ref[...] = _dot(xn_ref[...], w_ref[...])


def _inproj(x, g, w, tm, tn=1024):
    n = x.shape[0]
    return pl.pallas_call(
        _inproj_kernel,
        grid=(n // tm, D_IN_PAD // tn),
        in_specs=[pl.BlockSpec((tm, D_MODEL), lambda i, j: (i, 0)),
                  pl.BlockSpec((1, D_MODEL), lambda i, j: (0, 0)),
                  pl.BlockSpec((D_MODEL, tn), lambda i, j: (0, j))],
        out_specs=pl.BlockSpec((tm, tn), lambda i, j: (i, j)),
        out_shape=jax.ShapeDtypeStruct((n, D_IN_PAD), F32),
        scratch_shapes=[pltpu.VMEM((tm, D_MODEL), BF16)],
        compiler_params=_cparams("parallel", "arbitrary"),
        name="inproj",
    )(x, g, w)


def _merge_kernel(x_ref, oa_ref, ob_ref, oc_ref, od_ref, gate_ref, wb_ref, wo_ref, out_ref):
    acc = None
    for n, o_ref in enumerate((oa_ref, ob_ref, oc_ref, od_ref)):
        proj = _dot(o_ref[...].astype(BF16), wb_ref[n])
        term = _sigmoid(gate_ref[:, n * D_MODEL:(n + 1) * D_MODEL]) * proj
        acc = term if acc is None else acc + term
    out_ref[...] = x_ref[...] + _dot(acc.astype(BF16), wo_ref[...])


def _merge(x, branches, u, wb, wo, tm):
    n = x.shape[0]
    row = lambda i: (i, 0)
    bspec = pl.BlockSpec((tm, BRANCH_WIDTH), row)
    return pl.pallas_call(
        _merge_kernel,
        grid=(n // tm,),
        in_specs=[pl.BlockSpec((tm, D_MODEL), row), bspec, bspec, bspec, bspec,
                  pl.BlockSpec((tm, N_BRANCH * D_MODEL), lambda i: (i, SEG_MERGE // (N_BRANCH * D_MODEL))),
                  _const_spec((N_BRANCH, BRANCH_WIDTH, D_MODEL)),
                  _const_spec((D_MODEL, D_MODEL))],
        out_specs=pl.BlockSpec((tm, D_MODEL), row),
        out_shape=jax.ShapeDtypeStruct((n, D_MODEL), F32),
        compiler_params=_cparams("parallel"),
        name="merge",
    )(x, *branches, u, wb, wo)


FFN_COLS = 256
FFN_HALO = 8


def _ffn_kernel(x_ref, xp_ref, p1_ref, p2_ref, g_ref, wup_ref, cw_ref, cb_ref, wd_ref,
                out_ref, hs_ref, xn_ref, acc_ref, *, seq_tiles, period, keep):
    tm = x_ref.shape[0]
    prompt = seq_tiles > 0
    halo = FFN_HALO if prompt else 0

    def norm(x):
        ms = jnp.mean(x * x, axis=-1, keepdims=True)
        return (x * lax.rsqrt(ms + NORM_EPS) * g_ref[...]).astype(BF16)

    x = x_ref[...]
    xn_ref[halo:halo + tm, :] = norm(x)
    if prompt:
        first = (pl.program_id(0) % seq_tiles) == 0
        xn_ref[0:halo, :] = norm(xp_ref[...])
        hist_ok = jnp.where(first, 0.0, 1.0)
    else:
        step = _iota((tm, 1), 0) % period
    acc_ref[...] = x

    for c in range(D_FF // FFN_COLS):
        gs = slice(c * FFN_COLS, (c + 1) * FFN_COLS)
        us = slice(D_FF + c * FFN_COLS, D_FF + (c + 1) * FFN_COLS)
        xn = xn_ref[...]
        conv = []
        for part, cs in enumerate((gs, us)):
            h = _dot(xn, wup_ref[:, cs])
            if prompt:
                rows = _iota((tm + halo, 1), 0)
                h = jnp.where(rows < halo, h * hist_ok, h)
            h1 = pltpu.roll(h, 1, axis=0)
            h2 = pltpu.roll(h, 2, axis=0)
            if not prompt:
                h1 = jnp.where(step >= 1, h1, p1_ref[:, cs])
                h2 = jnp.where(step >= 2, h2, p2_ref[:, cs])
            y = cw_ref[2:3, cs] * h + cw_ref[1:2, cs] * h1 + cw_ref[0:1, cs] * h2 + cb_ref[:, cs]
            conv.append(y[halo:, :])
            hs_ref[0, :, cs] = h[halo + tm - keep:, :]
        act = (_silu(conv[0]) * conv[1]).astype(BF16)
        acc_ref[...] += _dot(act, wd_ref[gs, :])
    out_ref[...] = acc_ref[...]


def _ffn(x, prev1, prev2, g, wup, cw, cb, wd, *, tm, seq_tiles, period, keep):
    n = x.shape[0]
    nt = n // tm
    prompt = seq_tiles > 0
    halo = FFN_HALO if prompt else 0
    hb = tm // FFN_HALO
    if prompt:
        xp_spec = pl.BlockSpec((FFN_HALO, D_MODEL), lambda i: (jnp.maximum(i * hb - 1, 0), 0))
        xp = x
        p_spec = _const_spec(prev1.shape)
    else:
        xp_spec = _const_spec((FFN_HALO, D_MODEL))
        xp = x
        p_spec = pl.BlockSpec((tm, 2 * D_FF), lambda i: (i, 0))
    single = dict(pipeline_mode=pl.Buffered(1))
    kern = functools.partial(_ffn_kernel, seq_tiles=seq_tiles, period=period, keep=keep)
    return pl.pallas_call(
        kern,
        grid=(nt,),
        in_specs=[pl.BlockSpec((tm, D_MODEL), lambda i: (i, 0)), xp_spec, p_spec, p_spec,
                  _const_spec((1, D_MODEL)),
                  pl.BlockSpec((D_MODEL, 2 * D_FF), lambda i: (0, 0), **single),
                  _const_spec((FFN_CONV, 2 * D_FF)), _const_spec((1, 2 * D_FF)),
                  pl.BlockSpec((D_FF, D_MODEL), lambda i: (0, 0), **single)],
        out_specs=[pl.BlockSpec((tm, D_MODEL), lambda i: (i, 0)),
                   pl.BlockSpec((1, keep, 2 * D_FF), lambda i: (i, 0, 0))],
        out_shape=[jax.ShapeDtypeStruct((n, D_MODEL), F32),
                   jax.ShapeDtypeStruct((nt, keep, 2 * D_FF), F32)],
        scratch_shapes=[pltpu.VMEM((tm + halo, D_MODEL), BF16), pltpu.VMEM((tm, D_MODEL), F32)],
        compiler_params=_cparams("parallel"),
        name="ffn",
    )(x, xp, prev1, prev2, g, wup, cw, cb, wd)


SB_KT = 256
SB_PAGES = 16


def _head_block_mask(rows_per_head, n_rows):
    r = _iota((n_rows, BRANCH_WIDTH), 0) // rows_per_head
    c = _iota((n_rows, BRANCH_WIDTH), 1) // HEAD_DIM
    return r == c


def _suffix_matrix(n):
    return jnp.where(_iota((n, n), 0) > _iota((n, n), 1), 1.0, 0.0).astype(BF16)


def _sb_block(qbd, k, v, mask, carry, tri, token_minor=False):
    z = _dot(qbd, k) if token_minor else _dot_nt(qbd, k)
    sp = _softplus(z)
    l1 = -sp if mask is None else jnp.where(mask, -sp, 0.0)
    hi = l1.astype(BF16)
    lo = (l1 - hi.astype(F32)).astype(BF16)
    after = _dot(hi, tri) + _dot(lo, tri) + carry
    a = jnp.exp(z + l1 + after)
    if mask is not None:
        a = jnp.where(mask, a, 0.0)
    pv = _dot_nt(a.astype(BF16), v) if token_minor else _dot(a.astype(BF16), v)
    return pv, carry + jnp.sum(l1, axis=1, keepdims=True)


SB_DEAD = -104.0


def _sb_alive(carry):
    return (jnp.max(carry) > SB_DEAD).astype(jnp.int32)


def _fold_heads(acc, rows_per_head):
    masked = jnp.where(_head_block_mask(rows_per_head, acc.shape[0]), acc, 0.0)
    out = masked[0:rows_per_head]
    for h in range(1, N_HEADS):
        out = out + masked[h * rows_per_head:(h + 1) * rows_per_head]
    return out


def _sb_prompt_kernel(q_ref, k_ref, v_ref, o_ref, acc_ref, carry_ref):
    i = pl.program_id(1)
    rows = N_HEADS * Q_BLOCK
    q = q_ref[...] * (HEAD_DIM ** -0.5)
    qbd = jnp.where(_head_block_mask(Q_BLOCK, rows), jnp.concatenate([q] * N_HEADS, axis=0), 0.0).astype(BF16)
    tri = _suffix_matrix(SB_KT)
    acc_ref[...] = jnp.zeros_like(acc_ref)
    carry_ref[...] = jnp.zeros_like(carry_ref)
    qpos = i * Q_BLOCK + _iota((rows, 1), 0) % Q_BLOCK
    n_chunks = (i * Q_BLOCK) // SB_KT + 1

    def body(state):
        jj, _ = state
        j = n_chunks - 1 - jj
        start = pl.multiple_of(j * SB_KT, SB_KT)
        kpos = start + _iota((1, SB_KT), 1)
        pv, carry = _sb_block(qbd, k_ref[pl.ds(start, SB_KT), :], v_ref[pl.ds(start, SB_KT), :],
                              kpos < qpos, carry_ref[...], tri)
        acc_ref[...] += pv
        carry_ref[...] = carry
        return jj + 1, _sb_alive(carry)

    lax.while_loop(lambda st: (st[0] < n_chunks) & (st[1] > 0), body, (jnp.int32(0), jnp.int32(1)))
    o_ref[...] = _fold_heads(acc_ref[...], Q_BLOCK)


def _sb_prompt(u3, kb, vb):
    b, t, _ = u3.shape
    rows = N_HEADS * Q_BLOCK
    return pl.pallas_call(
        _sb_prompt_kernel,
        grid=(b, t // Q_BLOCK),
        in_specs=[pl.BlockSpec((None, Q_BLOCK, BRANCH_WIDTH), lambda bi, i: (bi, i, SEG_SB // BRANCH_WIDTH)),
                  pl.BlockSpec((None, t, BRANCH_WIDTH), lambda bi, i: (bi, 0, 0)),
                  pl.BlockSpec((None, t, BRANCH_WIDTH), lambda bi, i: (bi, 0, 0))],
        out_specs=pl.BlockSpec((None, Q_BLOCK, BRANCH_WIDTH), lambda bi, i: (bi, i, 0)),
        out_shape=jax.ShapeDtypeStruct((b, t, BRANCH_WIDTH), F32),
        scratch_shapes=[pltpu.VMEM((rows, BRANCH_WIDTH), F32), pltpu.VMEM((rows, 1), F32)],
        compiler_params=_cparams("parallel", "parallel"),
        name="sb_prompt",
    )(u3, kb, vb)


SAMPLE_ROWS = 8


def _sb_sample_kernel(pt_ref, q_ref, kn_ref, vn_ref, *refs, n_steps):
    k_refs = refs[:SB_PAGES]
    v_refs = refs[SB_PAGES:2 * SB_PAGES]
    o_ref, acc_ref, carry_ref, alive_ref = refs[2 * SB_PAGES:]
    s = pl.program_id(1)
    rows = N_HEADS * SAMPLE_ROWS

    def queries():
        return (q_ref[...] * HEAD_DIM ** -0.5).astype(BF16), _suffix_matrix(PAGE_SIZE)

    @pl.when(s == 0)
    def _():
        qbd, tri = queries()
        step = _iota((rows, 1), 0) % SAMPLE_ROWS
        col = _iota((1, PAGE_SIZE), 1)
        pv, carry = _sb_block(qbd, kn_ref[...].astype(BF16), vn_ref[...].astype(BF16), col < step,
                              jnp.zeros((rows, 1), F32), tri)
        acc_ref[...] = pv
        carry_ref[...] = carry
        alive_ref[0] = _sb_alive(carry)

    for r in range(SB_PAGES):
        @pl.when(alive_ref[0] > 0)
        def _():
            qbd, tri = queries()
            pv, carry = _sb_block(qbd, k_refs[r][...].astype(BF16), v_refs[r][...].astype(BF16), None,
                                  carry_ref[...], tri, token_minor=True)
            acc_ref[...] += pv
            carry_ref[...] = carry
            alive_ref[0] = _sb_alive(carry)

    @pl.when(s == n_steps - 1)
    def _():
        o_ref[...] = _fold_heads(acc_ref[...], SAMPLE_ROWS)


def _sb_sample(page_table, qbd, kn, vn, cache_k, cache_v, layer, n_pool):
    b, n_pages = page_table.shape
    n_steps = n_pages // SB_PAGES
    rows = N_HEADS * SAMPLE_ROWS
    base = layer * n_pool

    def page_spec(r):
        return pl.BlockSpec((None, BRANCH_WIDTH, PAGE_SIZE),
                            lambda bi, s, pt: (base + pt[bi, n_pages - 1 - (s * SB_PAGES + r)], 0, 0))

    per_b = lambda shape: pl.BlockSpec((None,) + shape, lambda bi, s, pt: (bi, 0, 0))
    grid_spec = pltpu.PrefetchScalarGridSpec(
        num_scalar_prefetch=1,
        grid=(b, n_steps),
        in_specs=[per_b((rows, BRANCH_WIDTH)), per_b((PAGE_SIZE, BRANCH_WIDTH)), per_b((PAGE_SIZE, BRANCH_WIDTH))]
                 + [page_spec(r) for r in range(SB_PAGES)] * 2,
        out_specs=per_b((SAMPLE_ROWS, BRANCH_WIDTH)),
        scratch_shapes=[pltpu.VMEM((rows, BRANCH_WIDTH), F32), pltpu.VMEM((rows, 1), F32),
                        pltpu.SMEM((1,), jnp.int32)],
    )
    return pl.pallas_call(
        functools.partial(_sb_sample_kernel, n_steps=n_steps),
        grid_spec=grid_spec,
        out_shape=jax.ShapeDtypeStruct((b, SAMPLE_ROWS, BRANCH_WIDTH), F32),
        compiler_params=_cparams("parallel", "arbitrary"),
        name="sb_sample",
    )(page_table, qbd, kn, vn, *([cache_k] * SB_PAGES), *([cache_v] * SB_PAGES))


NSA_KT = 512
ALIBI_SLOPES = tuple(2.0 ** (-8.0 * (h + 1.0) / N_HEADS) for h in range(N_HEADS))


def _row_slopes(rq):
    h = _iota((N_HEADS * rq, 1), 0) // rq
    out = jnp.full((N_HEADS * rq, 1), ALIBI_SLOPES[0], F32)
    for i in range(1, N_HEADS):
        out = jnp.where(h == i, ALIBI_SLOPES[i], out)
    return out


def _rep_right_matrix():
    r = _iota((LANES, BRANCH_WIDTH), 0)
    c = _iota((LANES, BRANCH_WIDTH), 1)
    return jnp.where(r == HEAD_DIM + c % HEAD_DIM, 1.0, 0.0)


def _nsa_prep_kernel(q_ref, kv_ref, qg_ref, kg_ref, qs_out, newk_out, newv_out, win_out, newkb_out, winb_out,
                     vslc_out, vwin_out):
    q = q_ref[...]
    qn = q * lax.rsqrt(_dot_hi(q * q, _head_sum_matrix(1.0 / HEAD_DIM)) + NORM_EPS) * qg_ref[...]
    qs_out[...] = _dot((qn * HEAD_DIM ** -0.5).astype(BF16), _query_place_matrix()).astype(BF16)
    kv = kv_ref[...]
    w = kv.shape[1]
    same = _iota((w, w), 0) // HEAD_DIM == _iota((w, w), 1) // HEAD_DIM
    ms = _dot_hi(kv * kv, jnp.where(same, 1.0 / HEAD_DIM, 0.0))
    grp = _iota((1, w), 1) // HEAD_DIM
    normed = jnp.where((grp == 1) | (grp == 4), kv * lax.rsqrt(ms + NORM_EPS) * kg_ref[...], kv)
    newk = normed[:, 0:LANES]
    newv = normed[:, LANES:2 * LANES]
    win = normed[:, 2 * LANES:3 * LANES]
    newk_out[...] = newk
    newv_out[...] = newv
    win_out[...] = win
    newkb_out[...] = newk.astype(BF16)
    winb_out[...] = win.astype(BF16)
    rep = _rep_right_matrix().astype(BF16)
    vslc_out[...] = _dot(newv.astype(BF16), rep).astype(BF16)
    vwin_out[...] = _dot(win.astype(BF16), rep).astype(BF16)


def _nsa_prep(u, q_gain, k_gain, tm):
    n = u.shape[0]
    out = lambda w: pl.BlockSpec((tm, w), lambda i: (i, 0))
    shp = lambda w, dt: jax.ShapeDtypeStruct((n, w), dt)
    return pl.pallas_call(
        _nsa_prep_kernel,
        grid=(n // tm,),
        in_specs=[pl.BlockSpec((tm, BRANCH_WIDTH), lambda i: (i, SEG_NSA_Q // BRANCH_WIDTH)),
                  pl.BlockSpec((tm, 3 * LANES), lambda i: (i, SEG_NSA_KV // (3 * LANES))),
                  _const_spec(q_gain.shape), _const_spec(k_gain.shape)],
        out_specs=[out(PLACED_WIDTH), out(LANES), out(LANES), out(LANES), out(LANES), out(LANES),
                   out(BRANCH_WIDTH), out(BRANCH_WIDTH)],
        out_shape=[shp(PLACED_WIDTH, BF16), shp(LANES, F32), shp(LANES, F32), shp(LANES, F32), shp(LANES, BF16),
                   shp(LANES, BF16), shp(BRANCH_WIDTH, BF16), shp(BRANCH_WIDTH, BF16)],
        compiler_params=_cparams("parallel"),
        name="nsa_prep",
    )(u, u, q_gain, k_gain)


def _nsa_compress(k_ref, v_ref, pek_ref, pev_ref, wk_ref, wv_ref, kn_ref, nb):
    def body(m, carry):
        ak, av = carry
        xk = k_ref[pl.ds(m, nb, stride=NSA_BLOCK), :] + pek_ref[pl.ds(m, 1), :]
        xv = v_ref[pl.ds(m, nb, stride=NSA_BLOCK), :] + pev_ref[pl.ds(m, 1), :]
        return ak + _dot(xk.astype(BF16), wk_ref[m]), av + _dot(xv.astype(BF16), wv_ref[m])

    ak, av = lax.fori_loop(0, NSA_BLOCK, body,
                           (jnp.zeros((nb, LANES), F32), jnp.zeros((nb, BRANCH_WIDTH), F32)), unroll=8)
    ms = jnp.sum(ak * ak, axis=1, keepdims=True) * (1.0 / HEAD_DIM)
    kc = ak * lax.rsqrt(ms + NORM_EPS) * kn_ref[...]
    return kc.astype(BF16), av.astype(BF16)


def _nsa_compress_pages(k_ref, v_ref, pek_ref, pev_ref, wk_ref, wv_ref, kn_ref, n_pages):
    def body(d, carry):
        ak, av = carry
        xk = k_ref[pl.ds(d, n_pages, stride=PAGE_SIZE), :] + pek_ref[pl.ds(d, 1), :]
        xv = v_ref[pl.ds(d, n_pages, stride=PAGE_SIZE), :] + pev_ref[pl.ds(d, 1), :]
        return ak + _dot(xk.astype(BF16), wk_ref[d]), av + _dot(xv.astype(BF16), wv_ref[d])

    ak, av = lax.fori_loop(0, HEAD_DIM, body,
                           (jnp.zeros((n_pages, 2 * LANES), F32), jnp.zeros((n_pages, 2 * BRANCH_WIDTH), F32)),
                           unroll=8)
    ak = jnp.concatenate([ak[:, 0:LANES], ak[:, LANES:]], axis=0)
    av = jnp.concatenate([av[:, 0:BRANCH_WIDTH], av[:, BRANCH_WIDTH:]], axis=0)
    ms = jnp.sum(ak * ak, axis=1, keepdims=True) * (1.0 / HEAD_DIM)
    kc = ak * lax.rsqrt(ms + NORM_EPS) * kn_ref[...]
    return kc.astype(BF16), av.astype(BF16)


def _nsa_compress_kernel(k_ref, v_ref, pek_ref, pev_ref, wk_ref, wv_ref, kn_ref, kc_out, vc_out):
    kc, vc = _nsa_compress(k_ref, v_ref, pek_ref, pev_ref, wk_ref, wv_ref, kn_ref, kc_out.shape[0])
    kc_out[...] = kc
    vc_out[...] = vc


def _nsa_compress_prompt(newk, newv, p):
    b, t, _ = newk.shape
    nb = t // NSA_BLOCK
    per_b = lambda rows, w: pl.BlockSpec((None, rows, w), lambda bi: (bi, 0, 0))
    consts = [p["pe_k"], p["pe_v"], p["w_k"], p["w_v"], p["k_gain_pad"]]
    return pl.pallas_call(
        _nsa_compress_kernel,
        grid=(b,),
        in_specs=[per_b(t, LANES), per_b(t, LANES)] + [_const_spec(a.shape) for a in consts],
        out_specs=[per_b(nb, LANES), per_b(nb, BRANCH_WIDTH)],
        out_shape=[jax.ShapeDtypeStruct((b, nb, LANES), BF16), jax.ShapeDtypeStruct((b, nb, BRANCH_WIDTH), BF16)],
        compiler_params=_cparams("parallel"),
        name="nsa_compress",
    )(newk, newv, *consts)


def _nsa_cmp_branch(ql, kc, vc, qpos, q0, slopes, blk=None):
    nb = kc.shape[0]
    blk = _iota((1, nb), 1) if blk is None else blk
    bend = (blk + 1) * NSA_BLOCK - 1
    valid = bend <= qpos
    s = jnp.where(valid, _dot_nt(ql, kc) + slopes * (bend - q0).astype(F32), NEG)
    e = jnp.exp(s - jnp.max(s, axis=1, keepdims=True))
    p = jnp.where(valid, e / jnp.sum(e, axis=1, keepdims=True), 0.0)
    return _dot(p.astype(BF16), vc), p


def _nsa_select(imp, qpos_q, nbl, blk=None):
    blk = _iota((1, nbl), 1) if blk is None else blk
    blk_f = blk.astype(F32)
    cur = qpos_q // NSA_BLOCK
    forced = (blk == 0) | (blk == cur) | (blk == cur - 1)
    work = jnp.where(blk > cur, NEG, jnp.where(forced, NSA_FORCED, imp))
    sel = jnp.zeros(work.shape, F32)
    for _ in range(min(NSA_TOPK, nbl)):
        mx = jnp.max(work, axis=1, keepdims=True)
        idx = jnp.min(jnp.where(work == mx, blk_f, float(nbl)), axis=1, keepdims=True)
        hit = blk_f == idx
        sel = jnp.where(hit & (mx > 0.5 * NEG), 1.0, sel)
        work = jnp.where(hit, -3e38, work)
    return sel


def _block_expand_matrix(nbl, start, kt, blk_col=None):
    blk_col = _iota((nbl, kt), 0) if blk_col is None else blk_col
    return jnp.where(blk_col == (start + _iota((nbl, kt), 1)) // NSA_BLOCK, 1.0, 0.0).astype(BF16)


def _gate_expand(gl, j):
    r = _iota((LANES, BRANCH_WIDTH), 0)
    c = _iota((LANES, BRANCH_WIDTH), 1)
    return _sigmoid(_dot_hi(gl, jnp.where(r == (c // HEAD_DIM) * 3 + j, 1.0, 0.0)))


def _softmax_rows(s):
    e = jnp.exp(s - jnp.max(s, axis=1, keepdims=True))
    return e / jnp.sum(e, axis=1, keepdims=True)


PLACED_WIDTH = 2 * N_HEADS * LANES


def _query_place_matrix():
    r = _iota((BRANCH_WIDTH, PLACED_WIDTH), 0)
    j = _iota((BRANCH_WIDTH, PLACED_WIDTH), 1)
    side = j // (N_HEADS * LANES)
    head = (j // LANES) % N_HEADS
    c = j % LANES
    left = (side == 0) & (c < HEAD_DIM) & (r == head * HEAD_DIM + c)
    right = (side == 1) & (c >= HEAD_DIM) & (r == head * HEAD_DIM + c - HEAD_DIM)
    return jnp.where(left | right, 1.0, 0.0).astype(BF16)


def _placed_rows(qs):
    ql = jnp.concatenate([qs[:, h * LANES:(h + 1) * LANES] for h in range(N_HEADS)], axis=0)
    qr = jnp.concatenate([qs[:, (N_HEADS + h) * LANES:(N_HEADS + h + 1) * LANES] for h in range(N_HEADS)], axis=0)
    return ql, qr


NSA_SELECT_TQ = 1024
CHUNK_BLOCKS = NSA_KT // NSA_BLOCK


def _nsa_select_kernel(qs_ref, kc_ref, vc_ref, oc_ref, sel_ref, hit_ref):
    tq = qs_ref.shape[0]
    rows = N_HEADS * tq
    q0 = pl.program_id(1) * tq
    nbl = kc_ref.shape[0]
    ql, _ = _placed_rows(qs_ref[...])
    qpos = q0 + _iota((rows, 1), 0) % tq
    o_c, p_c = _nsa_cmp_branch(ql, kc_ref[...], vc_ref[...], qpos, q0, _row_slopes(tq))
    oc_ref[...] = _fold_heads(o_c, tq)
    imp = p_c[0:tq]
    for h in range(1, N_HEADS):
        imp = imp + p_c[h * tq:(h + 1) * tq]
    sel = _nsa_select(imp, q0 + _iota((tq, 1), 0), nbl)
    sel_ref[...] = sel.astype(BF16)
    union = jnp.max(sel.reshape(tq // Q_BLOCK, Q_BLOCK, nbl), axis=1)
    chunk_of = jnp.where(_iota((nbl, LANES), 0) // CHUNK_BLOCKS == _iota((nbl, LANES), 1), 1.0, 0.0)
    hit_ref[...] = _dot(union.astype(BF16), chunk_of.astype(BF16))


def _nsa_select_prompt(qs, kc, vc):
    b, t, _ = qs.shape
    nb = kc.shape[1]
    tq = min(NSA_SELECT_TQ, t)
    per_b = lambda r, w: pl.BlockSpec((None, r, w), lambda bi, i: (bi, 0, 0))
    tile = lambda r, w: pl.BlockSpec((None, r, w), lambda bi, i: (bi, i, 0))
    return pl.pallas_call(
        _nsa_select_kernel,
        grid=(b, t // tq),
        in_specs=[tile(tq, PLACED_WIDTH), per_b(nb, LANES), per_b(nb, BRANCH_WIDTH)],
        out_specs=[tile(tq, BRANCH_WIDTH), tile(tq, nb), tile(tq // Q_BLOCK, LANES)],
        out_shape=[jax.ShapeDtypeStruct((b, t, BRANCH_WIDTH), F32), jax.ShapeDtypeStruct((b, t, nb), BF16),
                   jax.ShapeDtypeStruct((b, t // Q_BLOCK, LANES), F32)],
        compiler_params=_cparams("parallel", "parallel"),
        name="nsa_select",
    )(qs, kc, vc)


def _nsa_prompt_kernel(hit_ref, qs_ref, gate_ref, sel_ref, oc_ref, kb_ref, vs_ref, wb_ref, vw_ref, o_ref,
                       m_ref, l_ref, acc_ref, *, n_chunks_all):
    i = pl.program_id(1)
    rq = Q_BLOCK
    rows = N_HEADS * rq
    q0 = i * rq
    nbl = sel_ref.shape[1]
    ql, qr = _placed_rows(qs_ref[...])
    slopes = _row_slopes(rq)
    qpos = q0 + _iota((rows, 1), 0) % rq
    qpos_q = q0 + _iota((rq, 1), 0)
    sel = sel_ref[...]
    hit_base = (pl.program_id(0) * pl.num_programs(1) + i) * n_chunks_all

    span = NSA_WINDOW + rq
    wstart = pl.multiple_of(q0, rq)
    kpos_w = q0 - NSA_WINDOW + _iota((1, span), 1)
    dist = qpos - kpos_w
    valid_w = (dist >= 0) & (dist < NSA_WINDOW) & (kpos_w >= 0)
    s_w = _dot_nt(ql, wb_ref[pl.ds(wstart, span), :]) + slopes * (kpos_w - q0).astype(F32)
    o_w = _dot(_softmax_rows(jnp.where(valid_w, s_w, NEG)).astype(BF16), vw_ref[pl.ds(wstart, span), :])
    gl = gate_ref[...]
    o_ref[...] = _gate_expand(gl, 0) * oc_ref[...] + _gate_expand(gl, 2) * _fold_heads(o_w, rq)

    m_ref[...] = jnp.full(m_ref.shape, NEG, F32)
    l_ref[...] = jnp.zeros_like(l_ref)
    acc_ref[...] = jnp.zeros_like(acc_ref)
    n_chunks = (q0 + rq + NSA_KT - 1) // NSA_KT

    def body(j, _):
        @pl.when(hit_ref[hit_base + j] > 0)
        def _():
            start = pl.multiple_of(j * NSA_KT, NSA_KT)
            chosen = _dot(sel, _block_expand_matrix(nbl, start, NSA_KT))
            s = _dot_nt(qr, kb_ref[pl.ds(start, NSA_KT), :])
            v = vs_ref[pl.ds(start, NSA_KT), :]
            kpos = start + _iota((1, NSA_KT), 1)
            ok = (chosen > 0.5) & (kpos <= qpos_q)
            rel = (kpos - q0).astype(F32)
            heads = range(N_HEADS)
            rs = [slice(h * rq, (h + 1) * rq) for h in heads]
            s_h = [jnp.where(ok, s[rs[h]] + ALIBI_SLOPES[h] * rel, NEG) for h in heads]
            m_old = [m_ref[rs[h]] for h in heads]
            m_new = [jnp.maximum(m_old[h], jnp.max(s_h[h], axis=1, keepdims=True)) for h in heads]
            alpha = [jnp.exp(m_old[h] - m_new[h]) for h in heads]
            p = [jnp.where(ok, jnp.exp(s_h[h] - m_new[h]), 0.0) for h in heads]
            l_new = [alpha[h] * l_ref[rs[h]] + jnp.sum(p[h], axis=1, keepdims=True) for h in heads]
            pv = [_dot(p[h].astype(BF16), v) for h in heads]
            for h in heads:
                l_ref[rs[h]] = l_new[h]
                acc_ref[rs[h]] = alpha[h] * acc_ref[rs[h]] + pv[h]
                m_ref[rs[h]] = m_new[h]

        return 0

    lax.fori_loop(0, n_chunks, body, 0)
    o_ref[...] += _gate_expand(gl, 1) * _fold_heads(acc_ref[...] / l_ref[...], rq)


def _nsa_prompt(qs, u3, kc, vc, newk_b, vslc, win_b_pad, vwin_pad):
    b, t, _ = qs.shape
    nb = kc.shape[1]
    rows = N_HEADS * Q_BLOCK
    n_chunks_all = -(-t // NSA_KT)
    o_c, sel, hits = _nsa_select_prompt(qs, kc, vc)
    hit_flags = (hits[..., :n_chunks_all] > 0.5).astype(jnp.int32).reshape(-1)
    per_b = lambda r, w: pl.BlockSpec((None, r, w), lambda bi, i, hf: (bi, 0, 0))
    tile = lambda w, col=0: pl.BlockSpec((None, Q_BLOCK, w), lambda bi, i, hf: (bi, i, col))
    grid_spec = pltpu.PrefetchScalarGridSpec(
        num_scalar_prefetch=1,
        grid=(b, t // Q_BLOCK),
        in_specs=[tile(PLACED_WIDTH), tile(LANES, SEG_NSA_GATE // LANES), tile(nb), tile(BRANCH_WIDTH),
                  per_b(t, LANES), per_b(t, BRANCH_WIDTH),
                  per_b(t + NSA_WINDOW, LANES), per_b(t + NSA_WINDOW, BRANCH_WIDTH)],
        out_specs=tile(BRANCH_WIDTH),
        scratch_shapes=[pltpu.VMEM((rows, 1), F32), pltpu.VMEM((rows, 1), F32), pltpu.VMEM((rows, BRANCH_WIDTH), F32)],
    )
    return pl.pallas_call(
        functools.partial(_nsa_prompt_kernel, n_chunks_all=n_chunks_all),
        grid_spec=grid_spec,
        out_shape=jax.ShapeDtypeStruct((b, t, BRANCH_WIDTH), F32),
        compiler_params=_cparams("parallel", "parallel"),
        name="nsa_prompt",
    )(hit_flags, qs, u3, sel, o_c, newk_b, vslc, win_b_pad, vwin_pad)


NSA_PAGES = 8


def _nsa_sample_kernel(pt_ref, ql_ref, qr_ref, gate_ref, kn_ref, vn_ref, wbuf_ref, wn_ref, vwn_ref,
                       pek_ref, pev_ref, wk_ref, wv_ref, kg_ref, *refs, n_steps, n_new):
    k_pages = refs[:NSA_PAGES]
    v_pages = refs[NSA_PAGES:2 * NSA_PAGES]
    o_ref, kbuf_ref, vbuf_ref, kst_ref, vst_ref = refs[2 * NSA_PAGES:]
    s = pl.program_id(1)
    for r in range(NSA_PAGES):
        row0 = pl.multiple_of((s * NSA_PAGES + r) * PAGE_SIZE, PAGE_SIZE)
        kbuf_ref[pl.ds(row0, PAGE_SIZE), :] = k_pages[r][...]
        vbuf_ref[pl.ds(row0, PAGE_SIZE), :] = v_pages[r][...]

    @pl.when(s == n_steps - 1)
    def _():
        rq = SAMPLE_ROWS
        rows = N_HEADS * rq
        n_pages = kbuf_ref.shape[0] // PAGE_SIZE
        past = n_pages * PAGE_SIZE
        nb = past // NSA_BLOCK
        n_buf = wbuf_ref.shape[0]
        ql = ql_ref[...]
        qr = qr_ref[...]
        slopes = _row_slopes(rq)
        step = _iota((rows, 1), 0) % rq
        qpos = past + step
        step_q = _iota((rq, 1), 0)
        rep = _rep_right_matrix()
        tile = lambda a: jnp.concatenate([a] * N_HEADS, axis=0)

        kc, vc = _nsa_compress_pages(kbuf_ref, vbuf_ref, pek_ref, pev_ref, wk_ref, wv_ref, kg_ref, n_pages)
        order = lambda idx: 2 * (idx % n_pages) + idx // n_pages
        o_c, p_c = _nsa_cmp_branch(ql, kc, vc, qpos, past, slopes, blk=order(_iota((1, nb), 1)))
        imp = p_c[0:rq]
        for h in range(1, N_HEADS):
            imp = imp + p_c[h * rq:(h + 1) * rq]
        width = -(-(nb + 1) // LANES) * LANES
        imp = jnp.concatenate([imp, jnp.zeros((rq, width - nb), F32)], axis=1)
        lane = _iota((1, width), 1)
        sel = _nsa_select(imp, past + step_q, width, blk=jnp.where(lane < nb, order(lane), lane))

        for pg in range(n_pages):
            cols = slice(pg * PAGE_SIZE, (pg + 1) * PAGE_SIZE)
            slc_rows = slice(pg * PAGE_SIZE + HEAD_DIM, (pg + 1) * PAGE_SIZE)
            kst_ref[:, cols] = kbuf_ref[slc_rows, :].astype(BF16)
            vst_ref[:, cols] = vbuf_ref[slc_rows, :].astype(BF16)
        col = _iota((1, PAGE_SIZE), 1)
        expand = _block_expand_matrix(nb, 0, past, blk_col=order(_iota((nb, past), 0)))
        ok_p = tile(_dot(sel[:, 0:nb].astype(BF16), expand) > 0.5)
        s_p = _dot(ql[:, 0:HEAD_DIM], kst_ref[...]) + slopes * (_iota((1, past), 1) - past).astype(F32)
        s_p = jnp.where(ok_p, s_p, NEG)
        ok_n = tile(sel[:, nb:nb + 1] > 0.5) & (col <= step) & (col < n_new)
        s_n = jnp.where(ok_n, _dot_nt(qr, kn_ref[...]) + slopes * col.astype(F32), NEG)
        m = jnp.maximum(jnp.max(s_p, axis=1, keepdims=True), jnp.max(s_n, axis=1, keepdims=True))
        p_p = jnp.where(ok_p, jnp.exp(s_p - m), 0.0)
        p_n = jnp.where(ok_n, jnp.exp(s_n - m), 0.0)
        den = jnp.sum(p_p, axis=1, keepdims=True) + jnp.sum(p_n, axis=1, keepdims=True)
        spread = jnp.where(_iota((HEAD_DIM, BRANCH_WIDTH), 0) == _iota((HEAD_DIM, BRANCH_WIDTH), 1) % HEAD_DIM, 1.0, 0.0)
        o_s = (_dot_hi(_dot_nt(p_p.astype(BF16), vst_ref[...]), spread)
               + _dot(p_n.astype(BF16), vn_ref[...])) / den

        wb = wbuf_ref[...].astype(BF16)
        cw = _iota((1, n_buf), 1)
        kpos_w = past - n_buf + cw
        dist = qpos - kpos_w
        ok_w = (dist >= 0) & (dist < NSA_WINDOW) & (kpos_w >= 0)
        s_w = jnp.where(ok_w, _dot_nt(ql, wb) + slopes * (cw - n_buf).astype(F32), NEG)
        ok_wn = (col <= step) & (col < n_new)
        s_wn = jnp.where(ok_wn, _dot_nt(ql, wn_ref[...]) + slopes * col.astype(F32), NEG)
        m = jnp.maximum(jnp.max(s_w, axis=1, keepdims=True), jnp.max(s_wn, axis=1, keepdims=True))
        p_w = jnp.where(ok_w, jnp.exp(s_w - m), 0.0)
        p_wn = jnp.where(ok_wn, jnp.exp(s_wn - m), 0.0)
        den = jnp.sum(p_w, axis=1, keepdims=True) + jnp.sum(p_wn, axis=1, keepdims=True)
        o_w = (_dot_hi(_dot(p_w.astype(BF16), wb), rep) + _dot(p_wn.astype(BF16), vwn_ref[...])) / den

        gl = gate_ref[...]
        o_ref[...] = (_gate_expand(gl, 0) * _fold_heads(o_c, rq) + _gate_expand(gl, 1) * _fold_heads(o_s, rq)
                      + _gate_expand(gl, 2) * _fold_heads(o_w, rq))


def _nsa_sample(page_table, ql, qr, gate, kn, vn, wbuf, wn, vwn, p, cache_k, cache_v, layer, n_pool, n_new):
    b, n_pages = page_table.shape
    n_steps = n_pages // NSA_PAGES
    past = n_pages * PAGE_SIZE
    base = layer * n_pool

    def page_spec(r):
        return pl.BlockSpec((None, PAGE_SIZE, LANES), lambda bi, s, pt: (base + pt[bi, s * NSA_PAGES + r], 0, 0))

    per_b = lambda a: pl.BlockSpec((None,) + a.shape[1:], lambda bi, s, pt: (bi, 0, 0))
    const = lambda a: pl.BlockSpec(a.shape, lambda bi, s, pt: (0,) * a.ndim)
    consts = [p["pe_k_t"], p["pe_v_t"], p["w_k_t"], p["w_v_t"], p["k_gain_pad"]]
    seq_ops = [ql, qr, gate, kn, vn, wbuf, wn, vwn]
    grid_spec = pltpu.PrefetchScalarGridSpec(
        num_scalar_prefetch=1,
        grid=(b, n_steps),
        in_specs=[per_b(a) for a in seq_ops] + [const(a) for a in consts]
                 + [page_spec(r) for r in range(NSA_PAGES)] * 2,
        out_specs=pl.BlockSpec((None, SAMPLE_ROWS, BRANCH_WIDTH), lambda bi, s, pt: (bi, 0, 0)),
        scratch_shapes=[pltpu.VMEM((past, LANES), F32), pltpu.VMEM((past, LANES), F32),
                        pltpu.VMEM((HEAD_DIM, past), BF16), pltpu.VMEM((HEAD_DIM, past), BF16)],
    )
    return pl.pallas_call(
        functools.partial(_nsa_sample_kernel, n_steps=n_steps, n_new=n_new),
        grid_spec=grid_spec,
        out_shape=jax.ShapeDtypeStruct((b, SAMPLE_ROWS, BRANCH_WIDTH), F32),
        compiler_params=_cparams("parallel", "arbitrary"),
        name="nsa_sample",
    )(page_table, *seq_ops, *consts, *([cache_k] * NSA_PAGES), *([cache_v] * NSA_PAGES))


def _nsa_sample_mix(u3, page_table, wbuf, p, cache_k, cache_v, layer, n_pool):
    b, t, _ = u3.shape
    qs, newk, newv, win, newk_b, win_b, vslc, vwin = _nsa_prep(u3.reshape(b * t, -1), p["q_gain"], p["k_gain"], b * t)
    q4 = qs.reshape(b, t, 2, N_HEADS, LANES).transpose(2, 0, 3, 1, 4)
    q4 = jnp.pad(q4, ((0, 0), (0, 0), (0, 0), (0, SAMPLE_ROWS - t), (0, 0)))
    ql, qr = q4.reshape(2, b, N_HEADS * SAMPLE_ROWS, LANES)
    gate = jnp.pad(u3[..., SEG_NSA_GATE:SEG_NSA_GATE + LANES], ((0, 0), (0, SAMPLE_ROWS - t), (0, 0)))
    page_rows = lambda a: jnp.pad(a.reshape(b, t, -1), ((0, 0), (0, PAGE_SIZE - t), (0, 0)))
    o = _nsa_sample(page_table, ql, qr, gate, page_rows(newk_b), page_rows(vslc), wbuf, page_rows(win_b),
                    page_rows(vwin), p, cache_k, cache_v, layer, n_pool, t)
    return o[:, :t], newk, newv, win


def _nsa_params(q_norm, k_norm, cmp_pe, cmp_w):
    lane_pad = lambda a: jnp.pad(a, ((0, 0),) * (a.ndim - 1) + ((0, LANES - HEAD_DIM),))
    wk = cmp_w[0].reshape(NSA_BLOCK, HEAD_DIM, HEAD_DIM)
    wv = cmp_w[1].reshape(NSA_BLOCK, HEAD_DIM, HEAD_DIM)
    pad_rows = lambda a: jnp.pad(a, ((0, 0), (0, LANES - HEAD_DIM), (0, 0)))

    def per_dim(w):
        base = w.transpose(1, 0, 2)
        width = base.shape[2]
        out = jnp.zeros((HEAD_DIM, PAGE_SIZE, 2 * width), F32)
        for c in range(PAGE_SIZE // NSA_BLOCK):
            out = out.at[:, c * NSA_BLOCK:(c + 1) * NSA_BLOCK, c * width:(c + 1) * width].set(base)
        return out.astype(BF16)

    return dict(q_gain=jnp.tile(q_norm, N_HEADS)[None, :], k_gain=jnp.tile(k_norm, 6)[None, :],
                k_gain_pad=lane_pad(k_norm[None, :]), pe_k=lane_pad(cmp_pe[0]), pe_v=lane_pad(cmp_pe[1]),
                w_k=pad_rows(lane_pad(wk)).astype(BF16), w_v=pad_rows(jnp.tile(wv, (1, 1, N_HEADS))).astype(BF16),
                pe_k_t=jnp.tile(cmp_pe[0].T, (1, 2)), pe_v_t=jnp.tile(cmp_pe[1].T, (1, 2)),
                w_k_t=per_dim(lane_pad(wk)), w_v_t=per_dim(jnp.tile(wv, (1, 1, N_HEADS))))


SSM_HIST = 8
GROUP_LANES = BRANCH_WIDTH // SSM_GROUPS


def _ssd_kernel(xbc_ref, z_ref, dt_ref, dtt_ref, hist0_ref, h0_ref, cw_ref, cb_ref, bias_ref, biast_ref,
                a_ref, at_ref, dskip_ref, ng_ref, y_ref, hout_ref, hist_ref, state_ref, *, n_valid):
    c = pl.program_id(1)
    q = xbc_ref.shape[0]

    @pl.when(c == 0)
    def _():
        hist_ref[0:SSM_HIST, :] = hist0_ref[...]
        state_ref[...] = h0_ref[...]

    hist_ref[SSM_HIST:SSM_HIST + q, :] = xbc_ref[...]
    full = hist_ref[...]
    conv = cw_ref[3:4, :] * full + cb_ref[...]
    for k in range(1, SSM_CONV):
        conv = conv + cw_ref[3 - k:4 - k, :] * pltpu.roll(full, k, axis=0)
    hist_ref[0:SSM_HIST, :] = full[q:q + SSM_HIST, :]
    act = _silu(conv[SSM_HIST:, :])
    xs = act[:, 0:BRANCH_WIDTH]
    xs_b = xs.astype(BF16)
    bm = act[:, BRANCH_WIDTH:BRANCH_WIDTH + SSM_GROUPS * SSM_STATE].astype(BF16)
    cm = act[:, BRANCH_WIDTH + SSM_GROUPS * SSM_STATE:].astype(BF16)

    dt = jnp.where(_iota((q, LANES), 0) < n_valid, _softplus(dt_ref[...] + bias_ref[...]), 0.0)
    dta = dt * -jnp.exp(a_ref[...])
    dtt = jnp.where(_iota((SSM_HIST, q), 1) < n_valid, _softplus(dtt_ref[...] + biast_ref[...]), 0.0)
    dtat = dtt * -jnp.exp(at_ref[...])
    tril = jnp.where(_iota((q, q), 0) >= _iota((q, q), 1), 1.0, 0.0)
    triu = jnp.where(_iota((q, q), 0) <= _iota((q, q), 1), 1.0, 0.0)
    expand = jnp.where(_iota((LANES, BRANCH_WIDTH), 0) == _iota((LANES, BRANCH_WIDTH), 1) // HEAD_DIM, 1.0, 0.0)
    dt_e = _dot_hi(dt, expand)
    cum = _dot_hi(tril, dta)
    cum_e = _dot_hi(cum, expand)
    cum_t = _dot_hi(dtat, triu)
    causal = _iota((q, q), 0) >= _iota((q, q), 1)
    lane_head = _iota((1, BRANCH_WIDTH), 1) // HEAD_DIM

    heads = range(N_HEADS)
    grams = [_dot_nt(cm[:, g * SSM_STATE:(g + 1) * SSM_STATE], bm[:, g * SSM_STATE:(g + 1) * SSM_STATE])
             for g in range(SSM_GROUPS)]
    cum_l = [_dot_hi(cum, jnp.where(_iota((LANES, q), 0) == h, 1.0, 0.0)) for h in heads]
    decay = [jnp.where(causal, jnp.exp(jnp.minimum(cum_l[h] - cum_t[h:h + 1, :], 0.0)), 0.0) for h in heads]
    scores = [(grams[h // (N_HEADS // SSM_GROUPS)] * decay[h] * dtt[h:h + 1, :]).astype(BF16) for h in heads]
    intra = [_dot(scores[h], xs_b) for h in heads]
    y = dskip_ref[...] * xs
    for h in heads:
        y = y + jnp.where(lane_head == h, intra[h], 0.0)
    state = state_ref[...]
    inter = jnp.concatenate(
        [_dot(cm[:, g * SSM_STATE:(g + 1) * SSM_STATE],
              state[:, g * GROUP_LANES:(g + 1) * GROUP_LANES].astype(BF16)) for g in range(SSM_GROUPS)], axis=1)
    y = y + jnp.exp(cum_e) * inter

    cum_last = cum_e[q - 1:q, :]
    xw = (xs * jnp.exp(cum_last - cum_e) * dt_e).astype(BF16)
    contrib = jnp.concatenate(
        [lax.dot_general(bm[:, g * SSM_STATE:(g + 1) * SSM_STATE], xw[:, g * GROUP_LANES:(g + 1) * GROUP_LANES],
                         (((0,), (0,)), ((), ())), preferred_element_type=F32) for g in range(SSM_GROUPS)], axis=1)
    state_ref[...] = state * jnp.exp(cum_last) + contrib

    y = y * _silu(z_ref[...])
    parts = []
    for g in range(SSM_GROUPS):
        yg = y[:, g * GROUP_LANES:(g + 1) * GROUP_LANES]
        parts.append(yg * lax.rsqrt(jnp.mean(yg * yg, axis=-1, keepdims=True) + NORM_EPS))
    y_ref[...] = jnp.concatenate(parts, axis=1) * ng_ref[...]

    @pl.when(c == pl.num_programs(1) - 1)
    def _():
        hout_ref[...] = state_ref[...]


def _ssd(u3, dtt, hist0, h0, p, *, q, n_valid):
    b, t, _ = u3.shape
    per_b = lambda shape: pl.BlockSpec((None,) + shape, lambda bi, c: (bi, 0, 0))
    col = lambda width, seg: pl.BlockSpec((None, q, width), lambda bi, c: (bi, c, seg // width))
    consts = [p["conv_w"], p["conv_b"], p["dt_bias"], p["dt_bias_t"], p["a"], p["a_t"], p["d_skip"], p["norm_g"]]
    return pl.pallas_call(
        functools.partial(_ssd_kernel, n_valid=n_valid),
        grid=(b, t // q),
        in_specs=[col(SSM_CONV_DIM, SEG_SSM_XBC), col(BRANCH_WIDTH, SEG_SSM_Z), col(LANES, SEG_SSM_DT),
                  pl.BlockSpec((None, SSM_HIST, q), lambda bi, c: (bi, 0, c)),
                  per_b((SSM_HIST, SSM_CONV_DIM)), per_b((SSM_STATE, BRANCH_WIDTH))]
                 + [_const_spec(a.shape) for a in consts],
        out_specs=[pl.BlockSpec((None, q, BRANCH_WIDTH), lambda bi, c: (bi, c, 0)),
                   per_b((SSM_STATE, BRANCH_WIDTH))],
        out_shape=[jax.ShapeDtypeStruct((b, t, BRANCH_WIDTH), F32),
                   jax.ShapeDtypeStruct((b, SSM_STATE, BRANCH_WIDTH), F32)],
        scratch_shapes=[pltpu.VMEM((SSM_HIST + q, SSM_CONV_DIM), F32), pltpu.VMEM((SSM_STATE, BRANCH_WIDTH), F32)],
        compiler_params=_cparams("parallel", "arbitrary"),
        name="ssd",
    )(u3, u3, u3, dtt, hist0, h0, *consts)


def _head_sum_matrix(scale):
    same = _iota((BRANCH_WIDTH, BRANCH_WIDTH), 0) // HEAD_DIM == _iota((BRANCH_WIDTH, BRANCH_WIDTH), 1) // HEAD_DIM
    return jnp.where(same, scale, 0.0)


def _rwkv_prep_kernel(u_ref, up_ref, p1_ref, mu_ref, w0_ref, w2_ref, a0_ref, a2_ref, g2_ref, kk_ref, ka_ref, rk_ref,
                      r_out, k_out, v_out, d_out, nkk_out, kka_out, g_out, bonus_out, *, seq_tiles, period):
    tm = u_ref.shape[0]
    u = u_ref[...]
    rolled = pltpu.roll(u, 1, axis=0)
    if seq_tiles > 0:
        first = (pl.program_id(0) % seq_tiles) == 0
        carry_in = up_ref[FFN_HALO - 1:FFN_HALO, :] * jnp.where(first, 0.0, 1.0)
        prev = jnp.where(_iota((tm, 1), 0) == 0, carry_in, rolled)
    else:
        prev = jnp.where(_iota((tm, 1), 0) % period >= 1, rolled, p1_ref[...])
    us = u + (prev - u) * mu_ref[...]
    r = us[:, 0:BRANCH_WIDTH]
    k = us[:, BRANCH_WIDTH:2 * BRANCH_WIDTH]
    v = us[:, 2 * BRANCH_WIDTH:3 * BRANCH_WIDTH]
    wa = us[:, 3 * BRANCH_WIDTH:3 * BRANCH_WIDTH + LANES]
    gd = us[:, 3 * BRANCH_WIDTH + LANES:]
    is_w = _iota((1, LANES), 1) < RWKV_W_LORA
    w_lora = _dot(jnp.where(is_w, jnp.tanh(wa), 0.0).astype(BF16), w2_ref[...])
    a_lora = _dot(jnp.where(is_w, 0.0, wa).astype(BF16), a2_ref[...])
    w_raw = -_softplus(-(w0_ref[...] + w_lora)) - 0.5
    log_decay = -jnp.exp(w_raw)
    a = _sigmoid(a0_ref[...] + a_lora)
    g = _dot(_sigmoid(gd).astype(BF16), g2_ref[...])
    head_sum = _head_sum_matrix(1.0)
    kk = k * kk_ref[...]
    kk = kk / jnp.maximum(jnp.sqrt(_dot_hi(kk * kk, head_sum)), 1e-12)
    k_mod = k * (1.0 + (a - 1.0) * ka_ref[...])
    r_out[...] = r
    k_out[...] = k_mod
    v_out[...] = v
    d_out[...] = log_decay
    nkk_out[...] = -kk
    kka_out[...] = kk * a
    g_out[...] = g
    bonus_out[...] = _dot_hi(r * k_mod * rk_ref[...], head_sum) * v


def _rwkv_prep(u, p1, p, *, tm, seq_tiles, period):
    n = u.shape[0]
    hb = tm // FFN_HALO
    seg = SEG_RWKV // RWKV_IN
    if seq_tiles > 0:
        up_spec = pl.BlockSpec((FFN_HALO, RWKV_IN), lambda i: (jnp.maximum(i * hb - 1, 0), seg))
        p_spec = _const_spec(p1.shape)
    else:
        up_spec = pl.BlockSpec((FFN_HALO, RWKV_IN), lambda i: (0, seg))
        p_spec = pl.BlockSpec((tm, RWKV_IN), lambda i: (i, 0))
    consts = [p["mu"], p["w0"], p["w2"], p["a0"], p["a2"], p["g2"], p["k_k"], p["k_a"], p["r_k"]]
    out = pl.BlockSpec((tm, BRANCH_WIDTH), lambda i: (i, 0))
    return pl.pallas_call(
        functools.partial(_rwkv_prep_kernel, seq_tiles=seq_tiles, period=period),
        grid=(n // tm,),
        in_specs=[pl.BlockSpec((tm, RWKV_IN), lambda i: (i, seg)), up_spec, p_spec]
                 + [_const_spec(a.shape) for a in consts],
        out_specs=[out] * 8,
        out_shape=[jax.ShapeDtypeStruct((n, BRANCH_WIDTH), F32)] * 8,
        compiler_params=_cparams("parallel"),
        name="rwkv_prep",
    )(u, u, p1, *consts)


RWKV_CHAINS = 8
RWKV_TB = 128


def _rwkv_scan_kernel(r_ref, k_ref, d_ref, nkk_ref, kka_ref, vt_ref, s0_ref, yt_ref, sout_ref, s_ref, *, n_steps):
    @pl.when(pl.program_id(1) == 0)
    def _():
        s_ref[...] = s0_ref[...]

    yt_ref[...] = jnp.zeros_like(yt_ref)
    lane = _iota((HEAD_DIM, RWKV_TB), 1)

    def step(t, _):
        for c in range(RWKV_CHAINS):
            row = lambda ref: ref[c, pl.ds(t, 1), :]
            s = s_ref[c]
            sa = jnp.sum(s * row(nkk_ref), axis=1, keepdims=True)
            v_col = jnp.sum(jnp.where(lane == t, vt_ref[c], 0.0), axis=1, keepdims=True)
            s = s * jnp.exp(row(d_ref)) + sa * row(kka_ref) + v_col * row(k_ref)
            s_ref[c] = s
            y_col = jnp.sum(s * row(r_ref), axis=1, keepdims=True)
            yt_ref[c] = jnp.where(lane == t, y_col, yt_ref[c])
        return 0

    lax.fori_loop(0, n_steps, step, 0)

    @pl.when(pl.program_id(1) == pl.num_programs(1) - 1)
    def _():
        sout_ref[...] = s_ref[...]


def _rwkv_scan(r, k, d, nkk, kka, vt, s0, *, rows, n_steps):
    chains, t_rows, _ = r.shape
    t_lanes = vt.shape[2]
    row_spec = pl.BlockSpec((RWKV_CHAINS, rows, HEAD_DIM), lambda ci, tb: (ci, tb, 0))
    lane_spec = pl.BlockSpec((RWKV_CHAINS, HEAD_DIM, RWKV_TB), lambda ci, tb: (ci, 0, tb))
    state_spec = pl.BlockSpec((RWKV_CHAINS, HEAD_DIM, HEAD_DIM), lambda ci, tb: (ci, 0, 0))
    return pl.pallas_call(
        functools.partial(_rwkv_scan_kernel, n_steps=n_steps),
        grid=(chains // RWKV_CHAINS, t_lanes // RWKV_TB),
        in_specs=[row_spec] * 5 + [lane_spec, state_spec],
        out_specs=[lane_spec, state_spec],
        out_shape=[jax.ShapeDtypeStruct((chains, HEAD_DIM, t_lanes), F32),
                   jax.ShapeDtypeStruct((chains, HEAD_DIM, HEAD_DIM), F32)],
        scratch_shapes=[pltpu.VMEM((RWKV_CHAINS, HEAD_DIM, HEAD_DIM), F32)],
        compiler_params=_cparams("parallel", "arbitrary"),
        name="rwkv_scan",
    )(r, k, d, nkk, kka, vt, s0)


RWKV_CHUNK = 64


def _split2(x):
    hi = x.astype(BF16)
    return hi, (x - hi.astype(F32)).astype(BF16)


def _dot_split(a2, b2, dims=(((1,), (0,)), ((), ()))):
    (ah, al), (bh, bl) = a2, b2
    dg = lambda x, y: lax.dot_general(x, y, dims, preferred_element_type=F32)
    return dg(ah, bh) + dg(ah, bl) + dg(al, bh)


def _rwkv_chunk_step(r, k, v, ld, a, b, s):
    c = RWKV_CHUNK
    rows = N_HEADS * c
    cum = _dot_hi(jnp.where(_iota((c, c), 0) >= _iota((c, c), 1), 1.0, 0.0), ld)
    cum_last = cum[c - 1:c, :]
    g_inv = jnp.exp(-cum)
    g_end = jnp.exp(cum_last - cum)
    own = (_iota((rows, BRANCH_WIDTH), 0) // c) == (_iota((rows, BRANCH_WIDTH), 1) // HEAD_DIM)
    stack = lambda x: jnp.where(own, jnp.concatenate([x] * N_HEADS, axis=0), 0.0)
    tile = lambda x: jnp.concatenate([x] * N_HEADS, axis=0)
    ar = jnp.concatenate([stack(a * jnp.exp(cum - ld)), stack(r * jnp.exp(cum))], axis=0)
    bt = b * g_inv
    kt = k * g_inv
    nt = (((1,), (1,)), ((), ()))
    ar2 = _split2(ar)
    g_b = _dot_split(ar2, _split2(tile(bt)), nt)
    g_k = _dot_split(ar2, _split2(tile(kt)), nt)
    yield
    step_r = _iota((rows, rows), 0) % c
    step_c = _iota((rows, rows), 1) % c
    same = (_iota((rows, rows), 0) // c) == (_iota((rows, rows), 1) // c)
    strict = same & (step_c < step_r)
    incl = same & (step_c <= step_r)
    a_ab = jnp.where(strict, g_b[0:rows], 0.0)
    a_ak = jnp.where(strict, g_k[0:rows], 0.0)
    a_rb = jnp.where(incl, g_b[rows:], 0.0)
    a_rk = jnp.where(incl, g_k[rows:], 0.0)

    vbd = stack(v)
    vbd2 = _split2(vbd)
    w = _dot_split(ar2, _split2(s), nt)
    av = _dot_split(_split2(jnp.concatenate([a_ak, a_rk], axis=0)), vbd2)
    yield

    power2 = _split2(a_ab)
    u = w[0:rows] + av[0:rows]
    u = u + _dot_split(power2, _split2(u))
    for _ in range(int(math.log2(c)) - 1):
        yield
        power2 = _split2(_dot_split(power2, power2))
        u = u + _dot_split(power2, _split2(u))

    yield
    u2 = _split2(u)
    y = w[rows:] + av[rows:] + _dot_split(_split2(a_rb), u2)
    out = y[0:c]
    for h in range(1, N_HEADS):
        out = out + y[h * c:(h + 1) * c]
    cat2 = lambda p, q: (jnp.concatenate([p[0], q[0]], axis=0), jnp.concatenate([p[1], q[1]], axis=0))
    upd = _dot_split(cat2(u2, vbd2), cat2(_split2(stack(b * g_end)), _split2(stack(k * g_end))),
                     (((0,), (0,)), ((), ())))
    return out, s * jnp.exp(cum_last) + upd


def _rwkv_chunk_kernel(r_ref, k_ref, v_ref, ld_ref, a_ref, b_ref, y_ref, sout_ref, s_ref):
    @pl.when(pl.program_id(0) == 0)
    def _():
        s_ref[...] = jnp.zeros_like(s_ref)

    bsz = s_ref.shape[0]
    steps = [_rwkv_chunk_step(r_ref[bi], k_ref[bi], v_ref[bi], ld_ref[bi], a_ref[bi], b_ref[bi], s_ref[bi])
             for bi in range(bsz)]
    done = {}
    while len(done) < bsz:
        for bi in range(bsz):
            if bi not in done:
                try:
                    next(steps[bi])
                except StopIteration as fin:
                    done[bi] = fin.value
    for bi in range(bsz):
        y_ref[bi], s_ref[bi] = done[bi]

    @pl.when(pl.program_id(0) == pl.num_programs(0) - 1)
    def _():
        sout_ref[...] = s_ref[...]


def _rwkv_chunked(r, k, v, ld, a, b):
    bsz, t, _ = r.shape
    blk = pl.BlockSpec((bsz, RWKV_CHUNK, BRANCH_WIDTH), lambda ci: (0, ci, 0))
    st = pl.BlockSpec((bsz, BRANCH_WIDTH, BRANCH_WIDTH), lambda ci: (0, 0, 0))
    return pl.pallas_call(
        _rwkv_chunk_kernel,
        grid=(t // RWKV_CHUNK,),
        in_specs=[blk] * 6,
        out_specs=[blk, st],
        out_shape=[jax.ShapeDtypeStruct((bsz, t, BRANCH_WIDTH), F32),
                   jax.ShapeDtypeStruct((bsz, BRANCH_WIDTH, BRANCH_WIDTH), F32)],
        scratch_shapes=[pltpu.VMEM((bsz, BRANCH_WIDTH, BRANCH_WIDTH), F32)],
        compiler_params=_cparams("arbitrary"),
        name="rwkv_chunk",
    )(r, k, v, ld, a, b)


def _rwkv_post_kernel(y_ref, bonus_ref, g_ref, lnw_ref, lnb_ref, o_ref):
    y = y_ref[...]
    head_mean = _head_sum_matrix(1.0 / HEAD_DIM)
    cen = y - _dot_hi(y, head_mean)
    var = _dot_hi(cen * cen, head_mean)
    yn = cen * lax.rsqrt(var + RWKV_LN_EPS) * lnw_ref[...] + lnb_ref[...]
    o_ref[...] = (yn + bonus_ref[...]) * g_ref[...]


def _rwkv_post(y, bonus, g, ln_w, ln_b, tm):
    n = y.shape[0]
    blk = pl.BlockSpec((tm, BRANCH_WIDTH), lambda i: (i, 0))
    return pl.pallas_call(
        _rwkv_post_kernel,
        grid=(n // tm,),
        in_specs=[blk, blk, blk, _const_spec(ln_w.shape), _const_spec(ln_b.shape)],
        out_specs=blk,
        out_shape=jax.ShapeDtypeStruct((n, BRANCH_WIDTH), F32),
        compiler_params=_cparams("parallel"),
        name="rwkv_post",
    )(y, bonus, g, ln_w, ln_b)


def _rwkv_mix(u2, p1, s0, p, ln_w, ln_b, *, b, t, tm, seq_tiles, period):
    r, k, v, d, nkk, kka, g, bonus = _rwkv_prep(u2, p1, p, tm=tm, seq_tiles=seq_tiles, period=period)
    if s0 is None:
        r3 = lambda x: x.reshape(b, t, BRANCH_WIDTH)
        y, s_wide = _rwkv_chunked(r3(r), r3(k), r3(v), r3(d), r3(nkk), r3(kka))
        s_wide = s_wide.reshape(b, N_HEADS, HEAD_DIM, N_HEADS, HEAD_DIM)
        s_fin = jnp.stack([s_wide[:, h, :, h, :] for h in range(N_HEADS)], axis=1)
        o = _rwkv_post(y.reshape(b * t, BRANCH_WIDTH), bonus, g, ln_w, ln_b, tm)
        return o, s_fin.reshape(b * N_HEADS, HEAD_DIM, HEAD_DIM)
    t_rows = max(t, SAMPLE_ROWS)
    t_lanes = -(-t // RWKV_TB) * RWKV_TB

    def chains(a):
        a = a.reshape(b, t, N_HEADS, HEAD_DIM).transpose(0, 2, 1, 3).reshape(b * N_HEADS, t, HEAD_DIM)
        return a

    rows = [jnp.pad(chains(a), ((0, 0), (0, t_rows - t), (0, 0))) for a in (r, k, d, nkk, kka)]
    vt = jnp.pad(chains(v).transpose(0, 2, 1), ((0, 0), (0, 0), (0, t_lanes - t)))
    yt, s_fin = _rwkv_scan(*rows, vt, s0, rows=min(t_rows, RWKV_TB), n_steps=min(t, RWKV_TB))
    y = yt[:, :, :t].transpose(0, 2, 1).reshape(b, N_HEADS, t, HEAD_DIM).transpose(0, 2, 1, 3)
    o = _rwkv_post(y.reshape(b * t, BRANCH_WIDTH), bonus, g, ln_w, ln_b, tm)
    return o, s_fin


def _rwkv_params(mu, w0, w2, a0, a2, g2, k_k, k_a, r_k):
    row = lambda v: v.reshape(1, -1)
    zeros = jnp.zeros((RWKV_W_LORA, BRANCH_WIDTH), F32)
    return dict(mu=row(mu), w0=row(w0), w2=jnp.concatenate([w2, zeros]).astype(BF16), a0=row(a0),
                a2=jnp.concatenate([zeros, a2]).astype(BF16), g2=g2.astype(BF16), k_k=row(k_k), k_a=row(k_a),
                r_k=row(r_k))


def _ssd_params(conv_w, conv_b, dt_bias, a_log, d_skip, norm_g):
    a = a_log.astype(F32)
    pad_row = lambda v: jnp.pad(v, (0, LANES - N_HEADS))[None, :]
    pad_col = lambda v: jnp.pad(v, (0, SSM_HIST - N_HEADS))[:, None]
    return dict(conv_w=conv_w, conv_b=conv_b[None, :], dt_bias=pad_row(dt_bias), dt_bias_t=pad_col(dt_bias),
                a=pad_row(a), a_t=pad_col(a), d_skip=jnp.repeat(d_skip, HEAD_DIM)[None, :], norm_g=norm_g[None, :])


PROMPT_TM = 512
INPROJ_TM = 1024


def _sb_sample_mix(u3, page_table, cache_k, cache_v, layer, n_pool):
    b, t, _ = u3.shape
    q = u3[..., SEG_SB:SEG_SB + BRANCH_WIDTH]
    k = u3[..., SEG_SB + BRANCH_WIDTH:SEG_SB + 2 * BRANCH_WIDTH]
    v = u3[..., SEG_SB + 2 * BRANCH_WIDTH:SEG_SB + 3 * BRANCH_WIDTH]
    qt = jnp.tile(jnp.pad(q, ((0, 0), (0, SAMPLE_ROWS - t), (0, 0))), (1, N_HEADS, 1))
    rows = N_HEADS * SAMPLE_ROWS
    own = (np.arange(rows)[:, None] // SAMPLE_ROWS) == (np.arange(BRANCH_WIDTH)[None, :] // HEAD_DIM)
    qbd = jnp.where(own[None], qt, 0.0)
    page_rows = lambda a: jnp.pad(a, ((0, 0), (0, PAGE_SIZE - t), (0, 0)))
    o = _sb_sample(page_table, qbd, page_rows(k), page_rows(v), cache_k, cache_v, layer, n_pool)
    return o[:, :t]


def kernel(x_prompt, x_sample, cache_nsa_k, cache_nsa_v, cache_sb_k, cache_sb_v, state_win_kv, state_ssm_conv, state_ssm, state_rwkv_shift, state_rwkv, state_ffn_conv, page_table, norm1, norm2, w_in, nsa_q_norm, nsa_k_norm, nsa_cmp_pe, nsa_cmp_w, ssm_conv_w, ssm_conv_b, ssm_dt_bias, ssm_a_log, ssm_d, ssm_norm, rwkv_mu, rwkv_w0, rwkv_w2, rwkv_a0, rwkv_a2, rwkv_g2, rwkv_k_k, rwkv_k_a, rwkv_r_k, rwkv_ln_w, rwkv_ln_b, w_branch, w_out, ffn_up, ffn_conv_w, ffn_conv_b, ffn_down):
    bp, tp, _ = x_prompt.shape
    bs, ts, _ = x_sample.shape
    depth, n_pool = cache_nsa_k.shape[:2]
    n_pages = page_table.shape[1]
    past = n_pages * PAGE_SIZE
    n_buf = state_win_kv.shape[2]
    n_p, n_s = bp * tp, bs * ts
    assert tp % INPROJ_TM == 0 and tp >= NSA_WINDOW and past % NSA_BLOCK == 0 and ts < SAMPLE_ROWS
    assert n_pages % SB_PAGES == 0 and n_pages % NSA_PAGES == 0 and n_s % FFN_HALO == 0

    perm = _in_perm()
    w_in_p = jnp.where(perm >= 0, jnp.take(w_in, np.maximum(perm, 0), axis=2), 0.0).astype(BF16)
    token_minor = lambda c: jnp.transpose(c, (0, 1, 3, 4, 2)).reshape(depth * n_pool, -1, PAGE_SIZE)
    nsa_ck = token_minor(cache_nsa_k)
    nsa_cv = token_minor(cache_nsa_v)
    sb_ck = token_minor(cache_sb_k)
    sb_cv = token_minor(cache_sb_v)
    dummy = jnp.zeros((FFN_HALO, LANES), F32)
    seg = lambda u, start, width: u[..., start:start + width]

    xp = x_prompt.reshape(n_p, D_MODEL)
    xs = x_sample.reshape(n_s, D_MODEL)
    outs = [[] for _ in range(20)]
    for l in range(depth):
        nsa_p = _nsa_params(nsa_q_norm[l], nsa_k_norm[l], nsa_cmp_pe[l], nsa_cmp_w[l])
        ssd_p = _ssd_params(ssm_conv_w[l], ssm_conv_b[l], ssm_dt_bias[l], ssm_a_log[l], ssm_d[l], ssm_norm[l])
        rwkv_p = _rwkv_params(rwkv_mu[l], rwkv_w0[l], rwkv_w2[l], rwkv_a0[l], rwkv_a2[l], rwkv_g2[l],
                              rwkv_k_k[l], rwkv_k_a[l], rwkv_r_k[l].reshape(-1))
        ln_w, ln_b = rwkv_ln_w[l][None, :], rwkv_ln_b[l][None, :]
        wb, wo = w_branch[l].astype(BF16), w_out[l].astype(BF16)
        wup, wdn = ffn_up[l].astype(BF16), ffn_down[l].astype(BF16)

        up = _inproj(xp, norm1[l][None, :], w_in_p[l], INPROJ_TM)
        u3 = up.reshape(bp, tp, D_IN_PAD)
        r3 = lambda a: a.reshape(bp, tp, -1)
        qs, newk, newv, win, newk_b, win_b, vslc, vwin = _nsa_prep(up, nsa_p["q_gain"], nsa_p["k_gain"], PROMPT_TM)
        kc, vc = _nsa_compress_prompt(r3(newk), r3(newv), nsa_p)
        front = lambda a: jnp.pad(r3(a), ((0, 0), (NSA_WINDOW, 0), (0, 0)))
        o_a = _nsa_prompt(r3(qs), u3, kc, vc, r3(newk_b), r3(vslc), front(win_b), front(vwin))
        dtt = jnp.pad(jnp.swapaxes(seg(u3, SEG_SSM_DT, N_HEADS), 1, 2), ((0, 0), (0, SSM_HIST - N_HEADS), (0, 0)))
        o_b, ssm_fin = _ssd(u3, dtt, jnp.zeros((bp, SSM_HIST, SSM_CONV_DIM), F32),
                            jnp.zeros((bp, SSM_STATE, BRANCH_WIDTH), F32), ssd_p, q=SSM_CHUNK, n_valid=SSM_CHUNK)
        o_c, rwkv_fin = _rwkv_mix(up, dummy, None, rwkv_p, ln_w, ln_b,
                                  b=bp, t=tp, tm=PROMPT_TM, seq_tiles=tp // PROMPT_TM, period=0)
        sb_k = seg(u3, SEG_SB + BRANCH_WIDTH, BRANCH_WIDTH)
        sb_v = seg(u3, SEG_SB + 2 * BRANCH_WIDTH, BRANCH_WIDTH)
        o_d = _sb_prompt(u3, sb_k.astype(BF16), sb_v.astype(BF16))
        flat = lambda a: a.reshape(n_p, BRANCH_WIDTH)
        xp = _merge(xp, [flat(o_a), flat(o_b), o_c, flat(o_d)], up, wb, wo, PROMPT_TM)
        xp, hs = _ffn(xp, dummy, dummy, norm2[l][None, :], wup, ffn_conv_w[l], ffn_conv_b[l][None, :], wdn,
                      tm=PROMPT_TM, seq_tiles=tp // PROMPT_TM, period=0, keep=FFN_HALO)
        prompt_states = (
            r3(newk).reshape(bp, tp, 2, HEAD_DIM), r3(newv).reshape(bp, tp, 2, HEAD_DIM),
            sb_k.reshape(bp, tp, N_HEADS, HEAD_DIM), sb_v.reshape(bp, tp, N_HEADS, HEAD_DIM),
            r3(win)[:, tp - min(NSA_WINDOW, tp):].reshape(bp, -1, 2, HEAD_DIM),
            seg(u3, SEG_SSM_XBC, SSM_CONV_DIM)[:, tp - (SSM_CONV - 1):],
            ssm_fin.reshape(bp, SSM_STATE, N_HEADS, HEAD_DIM).transpose(0, 2, 3, 1),
            seg(u3, SEG_RWKV, RWKV_IN)[:, tp - 1:],
            rwkv_fin.reshape(bp, N_HEADS, HEAD_DIM, HEAD_DIM),
            hs.reshape(bp, tp // PROMPT_TM, FFN_HALO, 2 * D_FF)[:, -1, FFN_HALO - (FFN_CONV - 1):])

        us = _inproj(xs, norm1[l][None, :], w_in_p[l], n_s)
        u3 = us.reshape(bs, ts, D_IN_PAD)
        o_a, newk, newv, win = _nsa_sample_mix(u3, page_table, state_win_kv[l].reshape(bs, n_buf, LANES), nsa_p,
                                               nsa_ck, nsa_cv, l, n_pool)
        t_pad = 2 * SAMPLE_ROWS
        u3_pad = jnp.pad(u3, ((0, 0), (0, t_pad - ts), (0, 0)))
        dtt = jnp.pad(jnp.swapaxes(seg(u3_pad, SEG_SSM_DT, N_HEADS), 1, 2), ((0, 0), (0, SSM_HIST - N_HEADS), (0, 0)))
        hist0 = jnp.pad(state_ssm_conv[l], ((0, 0), (SSM_HIST - (SSM_CONV - 1), 0), (0, 0)))
        h0 = state_ssm[l].transpose(0, 3, 1, 2).reshape(bs, SSM_STATE, BRANCH_WIDTH)
        o_b, ssm_fin = _ssd(u3_pad, dtt, hist0, h0, ssd_p, q=t_pad, n_valid=ts)
        shift_rows = jnp.pad(state_rwkv_shift[l], ((0, 0), (0, ts - 1), (0, 0))).reshape(n_s, RWKV_IN)
        o_c, rwkv_fin = _rwkv_mix(us, shift_rows, state_rwkv[l].reshape(bs * N_HEADS, HEAD_DIM, HEAD_DIM), rwkv_p,
                                  ln_w, ln_b, b=bs, t=ts, tm=n_s, seq_tiles=0, period=ts)
        o_d = _sb_sample_mix(u3, page_table, sb_ck, sb_cv, l, n_pool)
        flat = lambda a: a.reshape(n_s, BRANCH_WIDTH)
        xs = _merge(xs, [flat(o_a), flat(o_b[:, :ts]), o_c, flat(o_d)], us, wb, wo, n_s)
        conv_state = state_ffn_conv[l]
        prev1 = jnp.pad(conv_state[:, 1:2], ((0, 0), (0, ts - 1), (0, 0))).reshape(n_s, 2 * D_FF)
        prev2 = jnp.pad(conv_state, ((0, 0), (0, ts - 2), (0, 0))).reshape(n_s, 2 * D_FF)
        xs, hs = _ffn(xs, prev1, prev2, norm2[l][None, :], wup, ffn_conv_w[l], ffn_conv_b[l][None, :], wdn,
                      tm=n_s, seq_tiles=0, period=ts, keep=n_s)
        tail = lambda old, new, n: jnp.concatenate([old, new], axis=1)[:, -n:]
        sample_states = (
            newk.reshape(bs, ts, 2, HEAD_DIM), newv.reshape(bs, ts, 2, HEAD_DIM),
            seg(u3, SEG_SB + BRANCH_WIDTH, BRANCH_WIDTH).reshape(bs, ts, N_HEADS, HEAD_DIM),
            seg(u3, SEG_SB + 2 * BRANCH_WIDTH, BRANCH_WIDTH).reshape(bs, ts, N_HEADS, HEAD_DIM),
            tail(state_win_kv[l], win.reshape(bs, ts, 2, HEAD_DIM), n_buf),
            tail(state_ssm_conv[l], seg(u3, SEG_SSM_XBC, SSM_CONV_DIM), SSM_CONV - 1),
            ssm_fin.reshape(bs, SSM_STATE, N_HEADS, HEAD_DIM).transpose(0, 2, 3, 1),
            seg(u3, SEG_RWKV, RWKV_IN)[:, ts - 1:],
            rwkv_fin.reshape(bs, N_HEADS, HEAD_DIM, HEAD_DIM),
            tail(conv_state, hs.reshape(bs, ts, 2 * D_FF), FFN_CONV - 1))
        for j in range(10):
            outs[2 * j].append(prompt_states[j])
            outs[2 * j + 1].append(sample_states[j])

    return (xp.reshape(bp, tp, D_MODEL), xs.reshape(bs, ts, D_MODEL)) + tuple(jnp.stack(o) for o in outs)
```

```python
import functools
import math

import numpy as np
import jax
import jax.numpy as jnp
from jax import lax
from jax.experimental import pallas as pl
from jax.experimental.pallas import tpu as pltpu

F32 = jnp.float32
BF16 = jnp.bfloat16
HIGHEST = lax.Precision.HIGHEST

D_MODEL = 1024
N_BRANCH = 4
BRANCH_WIDTH = D_MODEL // N_BRANCH
HEAD_DIM = 64
N_HEADS = BRANCH_WIDTH // HEAD_DIM
Q_BLOCK = 128
PAGE_SIZE = 128
NORM_EPS = 1e-6
NEG = -1e30
NSA_BLOCK = 64
NSA_TOPK = 16
NSA_WINDOW = 512
NSA_FORCED = 2.0 * N_HEADS
SSM_GROUPS = 2
SSM_STATE = 128
SSM_CONV = 4
SSM_CHUNK = 128
SSM_CONV_DIM = BRANCH_WIDTH + 2 * SSM_GROUPS * SSM_STATE
RWKV_W_LORA = 64
RWKV_A_LORA = 64
RWKV_G_LORA = 128
RWKV_IN = 3 * BRANCH_WIDTH + RWKV_W_LORA + RWKV_A_LORA + RWKV_G_LORA
RWKV_LN_EPS = 64e-5
D_FF = 2816
FFN_CONV = 3

LANES = 128
VMEM_LIMIT = 56 * 1024 * 1024

SEG_MERGE = 0
SEG_RWKV = 4096
SEG_NSA_Q = 5120
SEG_SSM_XBC = 5376
SEG_SB = 6144
SEG_NSA_KV = 6912
SEG_NSA_GATE = 7296
SEG_SSM_Z = 7424
SEG_SSM_DT = 7680
D_IN_PAD = 8192


def _in_perm():
    sizes = (BRANCH_WIDTH, 6 * HEAD_DIM, 3 * N_HEADS, BRANCH_WIDTH, SSM_CONV_DIM, N_HEADS, RWKV_IN,
             3 * BRANCH_WIDTH, N_BRANCH * D_MODEL)
    off = np.concatenate([[0], np.cumsum(sizes)])
    o_q, o_kv, o_gate, o_z, o_xbc, o_dt, o_rwkv, o_sb, o_merge = off[:-1]
    perm = -np.ones((D_IN_PAD,), np.int64)
    perm[SEG_MERGE:SEG_MERGE + 4096] = o_merge + np.arange(4096)
    perm[SEG_RWKV:SEG_RWKV + RWKV_IN] = o_rwkv + np.arange(RWKV_IN)
    perm[SEG_NSA_Q:SEG_NSA_Q + 256] = o_q + np.arange(256)
    perm[SEG_SSM_XBC:SEG_SSM_XBC + 768] = o_xbc + np.arange(768)
    perm[SEG_SB:SEG_SB + 768] = o_sb + np.arange(768)
    kv_order = (0, 2, 1, 3, 4, 5)
    for j, src in enumerate(kv_order):
        perm[SEG_NSA_KV + 64 * j:SEG_NSA_KV + 64 * (j + 1)] = o_kv + 64 * src + np.arange(64)
    perm[SEG_NSA_GATE:SEG_NSA_GATE + 12] = o_gate + np.arange(12)
    perm[SEG_SSM_Z:SEG_SSM_Z + 256] = o_z + np.arange(256)
    perm[SEG_SSM_DT:SEG_SSM_DT + 4] = o_dt + np.arange(4)
    return perm


def _cparams(*sem):
    return pltpu.CompilerParams(dimension_semantics=tuple(sem), vmem_limit_bytes=VMEM_LIMIT)


def _const_spec(shape):
    nd = len(shape)
    return pl.BlockSpec(shape, lambda *_: (0,) * nd)


def _iota(shape, dim):
    return lax.broadcasted_iota(jnp.int32, shape, dim)


def _dot(a, b):
    return jnp.dot(a, b, preferred_element_type=F32)


def _dot_hi(a, b):
    return jnp.dot(a, b, preferred_element_type=F32, precision=HIGHEST)


def _dot_nt(a, b):
    return lax.dot_general(a, b, (((1,), (1,)), ((), ())), preferred_element_type=F32)


def _sigmoid(x):
    return 1.0 / (1.0 + jnp.exp(-x))


def _silu(x):
    return x * _sigmoid(x)


def _softplus(x):
    return jnp.maximum(x, 0.0) + jnp.log(1.0 + jnp.exp(-jnp.abs(x)))


def _inproj_kernel(x_ref, g_ref, w_ref, o_ref, xn_ref):
    @pl.when(pl.program_id(1) == 0)
    def _():
        x = x_ref[...]
        ms = jnp.mean(x * x, axis=-1, keepdims=True)
        xn_ref[...] = (x * lax.rsqrt(ms + NORM_EPS) * g_ref[...]).astype(BF16)

    o_ref[...] = _dot(xn_ref[...], w_ref[...])


def _inproj(x, g, w, tm, tn=1024):
    n = x.shape[0]
    return pl.pallas_call(
        _inproj_kernel,
        grid=(n // tm, D_IN_PAD // tn),
        in_specs=[pl.BlockSpec((tm, D_MODEL), lambda i, j: (i, 0)),
                  pl.BlockSpec((1, D_MODEL), lambda i, j: (0, 0)),
                  pl.BlockSpec((D_MODEL, tn), lambda i, j: (0, j))],
        out_specs=pl.BlockSpec((tm, tn), lambda i, j: (i, j)),
        out_shape=jax.ShapeDtypeStruct((n, D_IN_PAD), F32),
        scratch_shapes=[pltpu.VMEM((tm, D_MODEL), BF16)],
        compiler_params=_cparams("parallel", "arbitrary"),
        name="inproj",
    )(x, g, w)


def _merge_kernel(x_ref, oa_ref, ob_ref, oc_ref, od_ref, gate_ref, wb_ref, wo_ref, out_ref):
    acc = None
    for n, o_ref in enumerate((oa_ref, ob_ref, oc_ref, od_ref)):
        proj = _dot(o_ref[...].astype(BF16), wb_ref[n])
        term = _sigmoid(gate_ref[:, n * D_MODEL:(n + 1) * D_MODEL]) * proj
        acc = term if acc is None else acc + term
    out_ref[...] = x_ref[...] + _dot(acc.astype(BF16), wo_ref[...])


def _merge(x, branches, u, wb, wo, tm):
    n = x.shape[0]
    row = lambda i: (i, 0)
    bspec = pl.BlockSpec((tm, BRANCH_WIDTH), row)
    return pl.pallas_call(
        _merge_kernel,
        grid=(n // tm,),
        in_specs=[pl.BlockSpec((tm, D_MODEL), row), bspec, bspec, bspec, bspec,
                  pl.BlockSpec((tm, N_BRANCH * D_MODEL), lambda i: (i, SEG_MERGE // (N_BRANCH * D_MODEL))),
                  _const_spec((N_BRANCH, BRANCH_WIDTH, D_MODEL)),
                  _const_spec((D_MODEL, D_MODEL))],
        out_specs=pl.BlockSpec((tm, D_MODEL), row),
        out_shape=jax.ShapeDtypeStruct((n, D_MODEL), F32),
        compiler_params=_cparams("parallel"),
        name="merge",
    )(x, *branches, u, wb, wo)


FFN_COLS = 256
FFN_HALO = 8


def _ffn_kernel(x_ref, xp_ref, p1_ref, p2_ref, g_ref, wup_ref, cw_ref, cb_ref, wd_ref,
                out_ref, hs_ref, xn_ref, acc_ref, *, seq_tiles, period, keep):
    tm = x_ref.shape[0]
    prompt = seq_tiles > 0
    halo = FFN_HALO if prompt else 0

    def norm(x):
        ms = jnp.mean(x * x, axis=-1, keepdims=True)
        return (x * lax.rsqrt(ms + NORM_EPS) * g_ref[...]).astype(BF16)

    x = x_ref[...]
    xn_ref[halo:halo + tm, :] = norm(x)
    if prompt:
        first = (pl.program_id(0) % seq_tiles) == 0
        xn_ref[0:halo, :] = norm(xp_ref[...])
        hist_ok = jnp.where(first, 0.0, 1.0)
    else:
        step = _iota((tm, 1), 0) % period
    acc_ref[...] = x

    for c in range(D_FF // FFN_COLS):
        gs = slice(c * FFN_COLS, (c + 1) * FFN_COLS)
        us = slice(D_FF + c * FFN_COLS, D_FF + (c + 1) * FFN_COLS)
        xn = xn_ref[...]
        conv = []
        for part, cs in enumerate((gs, us)):
            h = _dot(xn, wup_ref[:, cs])
            if prompt:
                rows = _iota((tm + halo, 1), 0)
                h = jnp.where(rows < halo, h * hist_ok, h)
            h1 = pltpu.roll(h, 1, axis=0)
            h2 = pltpu.roll(h, 2, axis=0)
            if not prompt:
                h1 = jnp.where(step >= 1, h1, p1_ref[:, cs])
                h2 = jnp.where(step >= 2, h2, p2_ref[:, cs])
            y = cw_ref[2:3, cs] * h + cw_ref[1:2, cs] * h1 + cw_ref[0:1, cs] * h2 + cb_ref[:, cs]
            conv.append(y[halo:, :])
            hs_ref[0, :, cs] = h[halo + tm - keep:, :]
        act = (_silu(conv[0]) * conv[1]).astype(BF16)
        acc_ref[...] += _dot(act, wd_ref[gs, :])
    out_ref[...] = acc_ref[...]


def _ffn(x, prev1, prev2, g, wup, cw, cb, wd, *, tm, seq_tiles, period, keep):
    n = x.shape[0]
    nt = n // tm
    prompt = seq_tiles > 0
    halo = FFN_HALO if prompt else 0
    hb = tm // FFN_HALO
    if prompt:
        xp_spec = pl.BlockSpec((FFN_HALO, D_MODEL), lambda i: (jnp.maximum(i * hb - 1, 0), 0))
        xp = x
        p_spec = _const_spec(prev1.shape)
    else:
        xp_spec = _const_spec((FFN_HALO, D_MODEL))
        xp = x
        p_spec = pl.BlockSpec((tm, 2 * D_FF), lambda i: (i, 0))
    single = dict(pipeline_mode=pl.Buffered(1))
    kern = functools.partial(_ffn_kernel, seq_tiles=seq_tiles, period=period, keep=keep)
    return pl.pallas_call(
        kern,
        grid=(nt,),
        in_specs=[pl.BlockSpec((tm, D_MODEL), lambda i: (i, 0)), xp_spec, p_spec, p_spec,
                  _const_spec((1, D_MODEL)),
                  pl.BlockSpec((D_MODEL, 2 * D_FF), lambda i: (0, 0), **single),
                  _const_spec((FFN_CONV, 2 * D_FF)), _const_spec((1, 2 * D_FF)),
                  pl.BlockSpec((D_FF, D_MODEL), lambda i: (0, 0), **single)],
        out_specs=[pl.BlockSpec((tm, D_MODEL), lambda i: (i, 0)),
                   pl.BlockSpec((1, keep, 2 * D_FF), lambda i: (i, 0, 0))],
        out_shape=[jax.ShapeDtypeStruct((n, D_MODEL), F32),
                   jax.ShapeDtypeStruct((nt, keep, 2 * D_FF), F32)],
        scratch_shapes=[pltpu.VMEM((tm + halo, D_MODEL), BF16), pltpu.VMEM((tm, D_MODEL), F32)],
        compiler_params=_cparams("parallel"),
        name="ffn",
    )(x, xp, prev1, prev2, g, wup, cw, cb, wd)


SB_KT = 256
SB_PAGES = 32


def _head_block_mask(rows_per_head, n_rows):
    r = _iota((n_rows, BRANCH_WIDTH), 0) // rows_per_head
    c = _iota((n_rows, BRANCH_WIDTH), 1) // HEAD_DIM
    return r == c


def _suffix_matrix(n):
    return jnp.where(_iota((n, n), 0) > _iota((n, n), 1), 1.0, 0.0).astype(BF16)


def _sb_block(qbd, k, v, mask, carry, tri, token_minor=False):
    z = _dot(qbd, k) if token_minor else _dot_nt(qbd, k)
    sp = _softplus(z)
    l1 = -sp if mask is None else jnp.where(mask, -sp, 0.0)
    hi = l1.astype(BF16)
    lo = (l1 - hi.astype(F32)).astype(BF16)
    after = _dot(hi, tri) + _dot(lo, tri) + carry
    a = jnp.exp(z + l1 + after)
    if mask is not None:
        a = jnp.where(mask, a, 0.0)
    pv = _dot_nt(a.astype(BF16), v) if token_minor else _dot(a.astype(BF16), v)
    return pv, carry + jnp.sum(l1, axis=1, keepdims=True)


SB_DEAD = -104.0


def _sb_alive(carry):
    return (jnp.max(carry) > SB_DEAD).astype(jnp.int32)


def _fold_heads(acc, rows_per_head):
    masked = jnp.where(_head_block_mask(rows_per_head, acc.shape[0]), acc, 0.0)
    out = masked[0:rows_per_head]
    for h in range(1, N_HEADS):
        out = out + masked[h * rows_per_head:(h + 1) * rows_per_head]
    return out


def _sb_prompt_kernel(q_ref, k_ref, v_ref, o_ref, acc_ref, carry_ref):
    i = pl.program_id(1)
    rows = N_HEADS * Q_BLOCK
    q = q_ref[...] * (HEAD_DIM ** -0.5)
    qbd = jnp.where(_head_block_mask(Q_BLOCK, rows), jnp.concatenate([q] * N_HEADS, axis=0), 0.0).astype(BF16)
    tri = _suffix_matrix(SB_KT)
    acc_ref[...] = jnp.zeros_like(acc_ref)
    carry_ref[...] = jnp.zeros_like(carry_ref)
    qpos = i * Q_BLOCK + _iota((rows, 1), 0) % Q_BLOCK
    n_chunks = (i * Q_BLOCK) // SB_KT + 1

    def body(state):
        jj, _ = state
        j = n_chunks - 1 - jj
        start = pl.multiple_of(j * SB_KT, SB_KT)
        kpos = start + _iota((1, SB_KT), 1)
        pv, carry = _sb_block(qbd, k_ref[pl.ds(start, SB_KT), :], v_ref[pl.ds(start, SB_KT), :],
                              kpos < qpos, carry_ref[...], tri)
        acc_ref[...] += pv
        carry_ref[...] = carry
        return jj + 1, _sb_alive(carry)

    lax.while_loop(lambda st: (st[0] < n_chunks) & (st[1] > 0), body, (jnp.int32(0), jnp.int32(1)))
    o_ref[...] = _fold_heads(acc_ref[...], Q_BLOCK)


def _sb_prompt(u3, kb, vb):
    b, t, _ = u3.shape
    rows = N_HEADS * Q_BLOCK
    return pl.pallas_call(
        _sb_prompt_kernel,
        grid=(b, t // Q_BLOCK),
        in_specs=[pl.BlockSpec((None, Q_BLOCK, BRANCH_WIDTH), lambda bi, i: (bi, i, SEG_SB // BRANCH_WIDTH)),
                  pl.BlockSpec((None, t, BRANCH_WIDTH), lambda bi, i: (bi, 0, 0)),
                  pl.BlockSpec((None, t, BRANCH_WIDTH), lambda bi, i: (bi, 0, 0))],
        out_specs=pl.BlockSpec((None, Q_BLOCK, BRANCH_WIDTH), lambda bi, i: (bi, i, 0)),
        out_shape=jax.ShapeDtypeStruct((b, t, BRANCH_WIDTH), F32),
        scratch_shapes=[pltpu.VMEM((rows, BRANCH_WIDTH), F32), pltpu.VMEM((rows, 1), F32)],
        compiler_params=_cparams("parallel", "parallel"),
        name="sb_prompt",
    )(u3, kb, vb)


SAMPLE_ROWS = 8


def _sb_sample_kernel(pt_ref, q_ref, kn_ref, vn_ref, *refs, n_steps):
    k_refs = refs[:SB_PAGES]
    v_refs = refs[SB_PAGES:2 * SB_PAGES]
    o_ref, acc_ref, carry_ref, alive_ref = refs[2 * SB_PAGES:]
    s = pl.program_id(1)
    rows = N_HEADS * SAMPLE_ROWS

    def queries():
        return (q_ref[...] * HEAD_DIM ** -0.5).astype(BF16), _suffix_matrix(PAGE_SIZE)

    @pl.when(s == 0)
    def _():
        qbd, tri = queries()
        step = _iota((rows, 1), 0) % SAMPLE_ROWS
        col = _iota((1, PAGE_SIZE), 1)
        pv, carry = _sb_block(qbd, kn_ref[...].astype(BF16), vn_ref[...].astype(BF16), col < step,
                              jnp.zeros((rows, 1), F32), tri)
        acc_ref[...] = pv
        carry_ref[...] = carry
        alive_ref[0] = _sb_alive(carry)

    for r in range(SB_PAGES):
        @pl.when(alive_ref[0] > 0)
        def _():
            qbd, tri = queries()
            pv, carry = _sb_block(qbd, k_refs[r][...].astype(BF16), v_refs[r][...].astype(BF16), None,
                                  carry_ref[...], tri, token_minor=True)
            acc_ref[...] += pv
            carry_ref[...] = carry
            alive_ref[0] = _sb_alive(carry)

    @pl.when(s == n_steps - 1)
    def _():
        o_ref[...] = _fold_heads(acc_ref[...], SAMPLE_ROWS)


def _sb_sample(page_table, qbd, kn, vn, cache_k, cache_v, layer, n_pool):
    b, n_pages = page_table.shape
    n_steps = n_pages // SB_PAGES
    rows = N_HEADS * SAMPLE_ROWS
    base = layer * n_pool

    def page_spec(r):
        return pl.BlockSpec((None, BRANCH_WIDTH, PAGE_SIZE),
                            lambda bi, s, pt: (base + pt[bi, n_pages - 1 - (s * SB_PAGES + r)], 0, 0))

    per_b = lambda shape: pl.BlockSpec((None,) + shape, lambda bi, s, pt: (bi, 0, 0))
    grid_spec = pltpu.PrefetchScalarGridSpec(
        num_scalar_prefetch=1,
        grid=(b, n_steps),
        in_specs=[per_b((rows, BRANCH_WIDTH)), per_b((PAGE_SIZE, BRANCH_WIDTH)), per_b((PAGE_SIZE, BRANCH_WIDTH))]
                 + [page_spec(r) for r in range(SB_PAGES)] * 2,
        out_specs=per_b((SAMPLE_ROWS, BRANCH_WIDTH)),
        scratch_shapes=[pltpu.VMEM((rows, BRANCH_WIDTH), F32), pltpu.VMEM((rows, 1), F32),
                        pltpu.SMEM((1,), jnp.int32)],
    )
    return pl.pallas_call(
        functools.partial(_sb_sample_kernel, n_steps=n_steps),
        grid_spec=grid_spec,
        out_shape=jax.ShapeDtypeStruct((b, SAMPLE_ROWS, BRANCH_WIDTH), F32),
        compiler_params=_cparams("parallel", "arbitrary"),
        name="sb_sample",
    )(page_table, qbd, kn, vn, *([cache_k] * SB_PAGES), *([cache_v] * SB_PAGES))


NSA_KT = 512
ALIBI_SLOPES = tuple(2.0 ** (-8.0 * (h + 1.0) / N_HEADS) for h in range(N_HEADS))


def _row_slopes(rq):
    h = _iota((N_HEADS * rq, 1), 0) // rq
    out = jnp.full((N_HEADS * rq, 1), ALIBI_SLOPES[0], F32)
    for i in range(1, N_HEADS):
        out = jnp.where(h == i, ALIBI_SLOPES[i], out)
    return out


def _rep_right_matrix():
    r = _iota((LANES, BRANCH_WIDTH), 0)
    c = _iota((LANES, BRANCH_WIDTH), 1)
    return jnp.where(r == HEAD_DIM + c % HEAD_DIM, 1.0, 0.0)


def _nsa_prep_kernel(q_ref, kv_ref, qg_ref, kg_ref, qs_out, newk_out, newv_out, win_out, newkb_out, winb_out,
                     vslc_out, vwin_out):
    q = q_ref[...]
    qn = q * lax.rsqrt(_dot_hi(q * q, _head_sum_matrix(1.0 / HEAD_DIM)) + NORM_EPS) * qg_ref[...]
    qs_out[...] = _dot((qn * HEAD_DIM ** -0.5).astype(BF16), _query_place_matrix()).astype(BF16)
    kv = kv_ref[...]
    w = kv.shape[1]
    same = _iota((w, w), 0) // HEAD_DIM == _iota((w, w), 1) // HEAD_DIM
    ms = _dot_hi(kv * kv, jnp.where(same, 1.0 / HEAD_DIM, 0.0))
    grp = _iota((1, w), 1) // HEAD_DIM
    normed = jnp.where((grp == 1) | (grp == 4), kv * lax.rsqrt(ms + NORM_EPS) * kg_ref[...], kv)
    newk = normed[:, 0:LANES]
    newv = normed[:, LANES:2 * LANES]
    win = normed[:, 2 * LANES:3 * LANES]
    newk_out[...] = newk
    newv_out[...] = newv
    win_out[...] = win
    newkb_out[...] = newk.astype(BF16)
    winb_out[...] = win.astype(BF16)
    rep = _rep_right_matrix().astype(BF16)
    vslc_out[...] = _dot(newv.astype(BF16), rep).astype(BF16)
    vwin_out[...] = _dot(win.astype(BF16), rep).astype(BF16)


def _nsa_prep(u, q_gain, k_gain, tm):
    n = u.shape[0]
    out = lambda w: pl.BlockSpec((tm, w), lambda i: (i, 0))
    shp = lambda w, dt: jax.ShapeDtypeStruct((n, w), dt)
    return pl.pallas_call(
        _nsa_prep_kernel,
        grid=(n // tm,),
        in_specs=[pl.BlockSpec((tm, BRANCH_WIDTH), lambda i: (i, SEG_NSA_Q // BRANCH_WIDTH)),
                  pl.BlockSpec((tm, 3 * LANES), lambda i: (i, SEG_NSA_KV // (3 * LANES))),
                  _const_spec(q_gain.shape), _const_spec(k_gain.shape)],
        out_specs=[out(PLACED_WIDTH), out(LANES), out(LANES), out(LANES), out(LANES), out(LANES),
                   out(BRANCH_WIDTH), out(BRANCH_WIDTH)],
        out_shape=[shp(PLACED_WIDTH, BF16), shp(LANES, F32), shp(LANES, F32), shp(LANES, F32), shp(LANES, BF16),
                   shp(LANES, BF16), shp(BRANCH_WIDTH, BF16), shp(BRANCH_WIDTH, BF16)],
        compiler_params=_cparams("parallel"),
        name="nsa_prep",
    )(u, u, q_gain, k_gain)


def _nsa_compress(k_ref, v_ref, pek_ref, pev_ref, wk_ref, wv_ref, kn_ref, nb):
    def body(m, carry):
        ak, av = carry
        xk = k_ref[pl.ds(m, nb, stride=NSA_BLOCK), :] + pek_ref[pl.ds(m, 1), :]
        xv = v_ref[pl.ds(m, nb, stride=NSA_BLOCK), :] + pev_ref[pl.ds(m, 1), :]
        return ak + _dot(xk.astype(BF16), wk_ref[m]), av + _dot(xv.astype(BF16), wv_ref[m])

    ak, av = lax.fori_loop(0, NSA_BLOCK, body,
                           (jnp.zeros((nb, LANES), F32), jnp.zeros((nb, BRANCH_WIDTH), F32)), unroll=8)
    ms = jnp.sum(ak * ak, axis=1, keepdims=True) * (1.0 / HEAD_DIM)
    kc = ak * lax.rsqrt(ms + NORM_EPS) * kn_ref[...]
    return kc.astype(BF16), av.astype(BF16)


def _nsa_compress_pages(k_ref, v_ref, pek_ref, pev_ref, wk_ref, wv_ref, kn_ref, n_pages):
    def body(d, carry):
        ak, av = carry
        xk = k_ref[pl.ds(d, n_pages, stride=PAGE_SIZE), :] + pek_ref[pl.ds(d, 1), :]
        xv = v_ref[pl.ds(d, n_pages, stride=PAGE_SIZE), :] + pev_ref[pl.ds(d, 1), :]
        return ak + _dot(xk.astype(BF16), wk_ref[d]), av + _dot(xv.astype(BF16), wv_ref[d])

    ak, av = lax.fori_loop(0, HEAD_DIM, body,
                           (jnp.zeros((n_pages, 2 * LANES), F32), jnp.zeros((n_pages, 2 * BRANCH_WIDTH), F32)),
                           unroll=8)
    ak = jnp.concatenate([ak[:, 0:LANES], ak[:, LANES:]], axis=0)
    av = jnp.concatenate([av[:, 0:BRANCH_WIDTH], av[:, BRANCH_WIDTH:]], axis=0)
    ms = jnp.sum(ak * ak, axis=1, keepdims=True) * (1.0 / HEAD_DIM)
    kc = ak * lax.rsqrt(ms + NORM_EPS) * kn_ref[...]
    return kc.astype(BF16), av.astype(BF16)


def _nsa_compress_kernel(k_ref, v_ref, pek_ref, pev_ref, wk_ref, wv_ref, kn_ref, kc_out, vc_out):
    kc, vc = _nsa_compress(k_ref, v_ref, pek_ref, pev_ref, wk_ref, wv_ref, kn_ref, kc_out.shape[0])
    kc_out[...] = kc
    vc_out[...] = vc


def _nsa_compress_prompt(newk, newv, p):
    b, t, _ = newk.shape
    nb = t // NSA_BLOCK
    per_b = lambda rows, w: pl.BlockSpec((None, rows, w), lambda bi: (bi, 0, 0))
    consts = [p["pe_k"], p["pe_v"], p["w_k"], p["w_v"], p["k_gain_pad"]]
    return pl.pallas_call(
        _nsa_compress_kernel,
        grid=(b,),
        in_specs=[per_b(t, LANES), per_b(t, LANES)] + [_const_spec(a.shape) for a in consts],
        out_specs=[per_b(nb, LANES), per_b(nb, BRANCH_WIDTH)],
        out_shape=[jax.ShapeDtypeStruct((b, nb, LANES), BF16), jax.ShapeDtypeStruct((b, nb, BRANCH_WIDTH), BF16)],
        compiler_params=_cparams("parallel"),
        name="nsa_compress",
    )(newk, newv, *consts)


def _nsa_cmp_branch(ql, kc, vc, qpos, q0, slopes, blk=None):
    nb = kc.shape[0]
    blk = _iota((1, nb), 1) if blk is None else blk
    bend = (blk + 1) * NSA_BLOCK - 1
    valid = bend <= qpos
    s = jnp.where(valid, _dot_nt(ql, kc) + slopes * (bend - q0).astype(F32), NEG)
    e = jnp.exp(s - jnp.max(s, axis=1, keepdims=True))
    p = jnp.where(valid, e / jnp.sum(e, axis=1, keepdims=True), 0.0)
    return _dot(p.astype(BF16), vc), p


def _nsa_select(imp, qpos_q, nbl, blk=None):
    blk = _iota((1, nbl), 1) if blk is None else blk
    blk_f = blk.astype(F32)
    cur = qpos_q // NSA_BLOCK
    forced = (blk == 0) | (blk == cur) | (blk == cur - 1)
    work = jnp.where(blk > cur, NEG, jnp.where(forced, NSA_FORCED, imp))
    sel = jnp.zeros(work.shape, F32)
    for _ in range(min(NSA_TOPK, nbl)):
        mx = jnp.max(work, axis=1, keepdims=True)
        idx = jnp.min(jnp.where(work == mx, blk_f, float(nbl)), axis=1, keepdims=True)
        hit = blk_f == idx
        sel = jnp.where(hit & (mx > 0.5 * NEG), 1.0, sel)
        work = jnp.where(hit, -3e38, work)
    return sel


def _block_expand_matrix(nbl, start, kt, blk_col=None):
    blk_col = _iota((nbl, kt), 0) if blk_col is None else blk_col
    return jnp.where(blk_col == (start + _iota((nbl, kt), 1)) // NSA_BLOCK, 1.0, 0.0).astype(BF16)


def _gate_expand(gl, j):
    r = _iota((LANES, BRANCH_WIDTH), 0)
    c = _iota((LANES, BRANCH_WIDTH), 1)
    return _sigmoid(_dot_hi(gl, jnp.where(r == (c // HEAD_DIM) * 3 + j, 1.0, 0.0)))


def _softmax_rows(s):
    e = jnp.exp(s - jnp.max(s, axis=1, keepdims=True))
    return e / jnp.sum(e, axis=1, keepdims=True)


PLACED_WIDTH = 2 * N_HEADS * LANES


def _query_place_matrix():
    r = _iota((BRANCH_WIDTH, PLACED_WIDTH), 0)
    j = _iota((BRANCH_WIDTH, PLACED_WIDTH), 1)
    side = j // (N_HEADS * LANES)
    head = (j // LANES) % N_HEADS
    c = j % LANES
    left = (side == 0) & (c < HEAD_DIM) & (r == head * HEAD_DIM + c)
    right = (side == 1) & (c >= HEAD_DIM) & (r == head * HEAD_DIM + c - HEAD_DIM)
    return jnp.where(left | right, 1.0, 0.0).astype(BF16)


def _placed_rows(qs):
    ql = jnp.concatenate([qs[:, h * LANES:(h + 1) * LANES] for h in range(N_HEADS)], axis=0)
    qr = jnp.concatenate([qs[:, (N_HEADS + h) * LANES:(N_HEADS + h + 1) * LANES] for h in range(N_HEADS)], axis=0)
    return ql, qr


NSA_SELECT_TQ = 1024
CHUNK_BLOCKS = NSA_KT // NSA_BLOCK


def _nsa_select_kernel(qs_ref, kc_ref, vc_ref, oc_ref, sel_ref, hit_ref):
    tq = qs_ref.shape[0]
    rows = N_HEADS * tq
    q0 = pl.program_id(1) * tq
    nbl = kc_ref.shape[0]
    ql, _ = _placed_rows(qs_ref[...])
    qpos = q0 + _iota((rows, 1), 0) % tq
    o_c, p_c = _nsa_cmp_branch(ql, kc_ref[...], vc_ref[...], qpos, q0, _row_slopes(tq))
    oc_ref[...] = _fold_heads(o_c, tq)
    imp = p_c[0:tq]
    for h in range(1, N_HEADS):
        imp = imp + p_c[h * tq:(h + 1) * tq]
    sel = _nsa_select(imp, q0 + _iota((tq, 1), 0), nbl)
    sel_ref[...] = sel.astype(BF16)
    union = jnp.max(sel.reshape(tq // Q_BLOCK, Q_BLOCK, nbl), axis=1)
    chunk_of = jnp.where(_iota((nbl, LANES), 0) // CHUNK_BLOCKS == _iota((nbl, LANES), 1), 1.0, 0.0)
    hit_ref[...] = _dot(union.astype(BF16), chunk_of.astype(BF16))


def _nsa_select_prompt(qs, kc, vc):
    b, t, _ = qs.shape
    nb = kc.shape[1]
    tq = min(NSA_SELECT_TQ, t)
    per_b = lambda r, w: pl.BlockSpec((None, r, w), lambda bi, i: (bi, 0, 0))
    tile = lambda r, w: pl.BlockSpec((None, r, w), lambda bi, i: (bi, i, 0))
    return pl.pallas_call(
        _nsa_select_kernel,
        grid=(b, t // tq),
        in_specs=[tile(tq, PLACED_WIDTH), per_b(nb, LANES), per_b(nb, BRANCH_WIDTH)],
        out_specs=[tile(tq, BRANCH_WIDTH), tile(tq, nb), tile(tq // Q_BLOCK, LANES)],
        out_shape=[jax.ShapeDtypeStruct((b, t, BRANCH_WIDTH), F32), jax.ShapeDtypeStruct((b, t, nb), BF16),
                   jax.ShapeDtypeStruct((b, t // Q_BLOCK, LANES), F32)],
        compiler_params=_cparams("parallel", "parallel"),
        name="nsa_select",
    )(qs, kc, vc)


def _nsa_prompt_kernel(hit_ref, qs_ref, gate_ref, sel_ref, oc_ref, kb_ref, vs_ref, wb_ref, vw_ref, o_ref,
                       m_ref, l_ref, acc_ref, *, n_chunks_all):
    i = pl.program_id(1)
    rq = Q_BLOCK
    rows = N_HEADS * rq
    q0 = i * rq
    nbl = sel_ref.shape[1]
    ql, qr = _placed_rows(qs_ref[...])
    slopes = _row_slopes(rq)
    qpos = q0 + _iota((rows, 1), 0) % rq
    qpos_q = q0 + _iota((rq, 1), 0)
    sel = sel_ref[...]
    hit_base = (pl.program_id(0) * pl.num_programs(1) + i) * n_chunks_all

    span = NSA_WINDOW + rq
    wstart = pl.multiple_of(q0, rq)
    kpos_w = q0 - NSA_WINDOW + _iota((1, span), 1)
    dist = qpos - kpos_w
    valid_w = (dist >= 0) & (dist < NSA_WINDOW) & (kpos_w >= 0)
    s_w = _dot_nt(ql, wb_ref[pl.ds(wstart, span), :]) + slopes * (kpos_w - q0).astype(F32)
    o_w = _dot(_softmax_rows(jnp.where(valid_w, s_w, NEG)).astype(BF16), vw_ref[pl.ds(wstart, span), :])
    gl = gate_ref[...]
    o_ref[...] = _gate_expand(gl, 0) * oc_ref[...] + _gate_expand(gl, 2) * _fold_heads(o_w, rq)

    m_ref[...] = jnp.full(m_ref.shape, NEG, F32)
    l_ref[...] = jnp.zeros_like(l_ref)
    acc_ref[...] = jnp.zeros_like(acc_ref)
    n_chunks = (q0 + rq + NSA_KT - 1) // NSA_KT

    def body(j, _):
        @pl.when(hit_ref[hit_base + j] > 0)
        def _():
            start = pl.multiple_of(j * NSA_KT, NSA_KT)
            chosen = _dot(sel, _block_expand_matrix(nbl, start, NSA_KT))
            s = _dot_nt(qr, kb_ref[pl.ds(start, NSA_KT), :])
            v = vs_ref[pl.ds(start, NSA_KT), :]
            kpos = start + _iota((1, NSA_KT), 1)
            ok = (chosen > 0.5) & (kpos <= qpos_q)
            rel = (kpos - q0).astype(F32)
            heads = range(N_HEADS)
            rs = [slice(h * rq, (h + 1) * rq) for h in heads]
            s_h = [jnp.where(ok, s[rs[h]] + ALIBI_SLOPES[h] * rel, NEG) for h in heads]
            m_old = [m_ref[rs[h]] for h in heads]
            m_new = [jnp.maximum(m_old[h], jnp.max(s_h[h], axis=1, keepdims=True)) for h in heads]
            alpha = [jnp.exp(m_old[h] - m_new[h]) for h in heads]
            p = [jnp.where(ok, jnp.exp(s_h[h] - m_new[h]), 0.0) for h in heads]
            l_new = [alpha[h] * l_ref[rs[h]] + jnp.sum(p[h], axis=1, keepdims=True) for h in heads]
            pv = [_dot(p[h].astype(BF16), v) for h in heads]
            for h in heads:
                l_ref[rs[h]] = l_new[h]
                acc_ref[rs[h]] = alpha[h] * acc_ref[rs[h]] + pv[h]
                m_ref[rs[h]] = m_new[h]

        return 0

    lax.fori_loop(0, n_chunks, body, 0)
    o_ref[...] += _gate_expand(gl, 1) * _fold_heads(acc_ref[...] / l_ref[...], rq)


def _nsa_prompt(qs, u3, kc, vc, newk_b, vslc, win_b_pad, vwin_pad):
    b, t, _ = qs.shape
    nb = kc.shape[1]
    rows = N_HEADS * Q_BLOCK
    n_chunks_all = -(-t // NSA_KT)
    o_c, sel, hits = _nsa_select_prompt(qs, kc, vc)
    hit_flags = (hits[..., :n_chunks_all] > 0.5).astype(jnp.int32).reshape(-1)
    per_b = lambda r, w: pl.BlockSpec((None, r, w), lambda bi, i, hf: (bi, 0, 0))
    tile = lambda w, col=0: pl.BlockSpec((None, Q_BLOCK, w), lambda bi, i, hf: (bi, i, col))
    grid_spec = pltpu.PrefetchScalarGridSpec(
        num_scalar_prefetch=1,
        grid=(b, t // Q_BLOCK),
        in_specs=[tile(PLACED_WIDTH), tile(LANES, SEG_NSA_GATE // LANES), tile(nb), tile(BRANCH_WIDTH),
                  per_b(t, LANES), per_b(t, BRANCH_WIDTH),
                  per_b(t + NSA_WINDOW, LANES), per_b(t + NSA_WINDOW, BRANCH_WIDTH)],
        out_specs=tile(BRANCH_WIDTH),
        scratch_shapes=[pltpu.VMEM((rows, 1), F32), pltpu.VMEM((rows, 1), F32), pltpu.VMEM((rows, BRANCH_WIDTH), F32)],
    )
    return pl.pallas_call(
        functools.partial(_nsa_prompt_kernel, n_chunks_all=n_chunks_all),
        grid_spec=grid_spec,
        out_shape=jax.ShapeDtypeStruct((b, t, BRANCH_WIDTH), F32),
        compiler_params=_cparams("parallel", "parallel"),
        name="nsa_prompt",
    )(hit_flags, qs, u3, sel, o_c, newk_b, vslc, win_b_pad, vwin_pad)


NSA_PAGES = 16


def _nsa_sample_kernel(pt_ref, ql_ref, qr_ref, gate_ref, kn_ref, vn_ref, wbuf_ref, wn_ref, vwn_ref,
                       pek_ref, pev_ref, wk_ref, wv_ref, kg_ref, *refs, n_steps, n_new):
    k_pages = refs[:NSA_PAGES]
    v_pages = refs[NSA_PAGES:2 * NSA_PAGES]
    o_ref, kbuf_ref, vbuf_ref, kst_ref, vst_ref = refs[2 * NSA_PAGES:]
    s = pl.program_id(1)
    for r in range(NSA_PAGES):
        row0 = pl.multiple_of((s * NSA_PAGES + r) * PAGE_SIZE, PAGE_SIZE)
        kbuf_ref[pl.ds(row0, PAGE_SIZE), :] = k_pages[r][...]
        vbuf_ref[pl.ds(row0, PAGE_SIZE), :] = v_pages[r][...]

    @pl.when(s == n_steps - 1)
    def _():
        rq = SAMPLE_ROWS
        rows = N_HEADS * rq
        n_pages = kbuf_ref.shape[0] // PAGE_SIZE
        past = n_pages * PAGE_SIZE
        nb = past // NSA_BLOCK
        n_buf = wbuf_ref.shape[0]
        ql = ql_ref[...]
        qr = qr_ref[...]
        slopes = _row_slopes(rq)
        step = _iota((rows, 1), 0) % rq
        qpos = past + step
        step_q = _iota((rq, 1), 0)
        rep = _rep_right_matrix()
        tile = lambda a: jnp.concatenate([a] * N_HEADS, axis=0)

        kc, vc = _nsa_compress_pages(kbuf_ref, vbuf_ref, pek_ref, pev_ref, wk_ref, wv_ref, kg_ref, n_pages)
        order = lambda idx: 2 * (idx % n_pages) + idx // n_pages
        o_c, p_c = _nsa_cmp_branch(ql, kc, vc, qpos, past, slopes, blk=order(_iota((1, nb), 1)))
        imp = p_c[0:rq]
        for h in range(1, N_HEADS):
            imp = imp + p_c[h * rq:(h + 1) * rq]
        width = -(-(nb + 1) // LANES) * LANES
        imp = jnp.concatenate([imp, jnp.zeros((rq, width - nb), F32)], axis=1)
        lane = _iota((1, width), 1)
        sel = _nsa_select(imp, past + step_q, width, blk=jnp.where(lane < nb, order(lane), lane))

        for pg in range(n_pages):
            cols = slice(pg * PAGE_SIZE, (pg + 1) * PAGE_SIZE)
            slc_rows = slice(pg * PAGE_SIZE + HEAD_DIM, (pg + 1) * PAGE_SIZE)
            kst_ref[:, cols] = kbuf_ref[slc_rows, :].astype(BF16)
            vst_ref[:, cols] = vbuf_ref[slc_rows, :].astype(BF16)
        col = _iota((1, PAGE_SIZE), 1)
        expand = _block_expand_matrix(nb, 0, past, blk_col=order(_iota((nb, past), 0)))
        ok_p = tile(_dot(sel[:, 0:nb].astype(BF16), expand) > 0.5)
        s_p = _dot(ql[:, 0:HEAD_DIM], kst_ref[...]) + slopes * (_iota((1, past), 1) - past).astype(F32)
        s_p = jnp.where(ok_p, s_p, NEG)
        ok_n = tile(sel[:, nb:nb + 1] > 0.5) & (col <= step) & (col < n_new)
        s_n = jnp.where(ok_n, _dot_nt(qr, kn_ref[...]) + slopes * col.astype(F32), NEG)
        m = jnp.maximum(jnp.max(s_p, axis=1, keepdims=True), jnp.max(s_n, axis=1, keepdims=True))
        p_p = jnp.where(ok_p, jnp.exp(s_p - m), 0.0)
        p_n = jnp.where(ok_n, jnp.exp(s_n - m), 0.0)
        den = jnp.sum(p_p, axis=1, keepdims=True) + jnp.sum(p_n, axis=1, keepdims=True)
        spread = jnp.where(_iota((HEAD_DIM, BRANCH_WIDTH), 0) == _iota((HEAD_DIM, BRANCH_WIDTH), 1) % HEAD_DIM, 1.0, 0.0)
        o_s = (_dot_hi(_dot_nt(p_p.astype(BF16), vst_ref[...]), spread)
               + _dot(p_n.astype(BF16), vn_ref[...])) / den

        wb = wbuf_ref[...].astype(BF16)
        cw = _iota((1, n_buf), 1)
        kpos_w = past - n_buf + cw
        dist = qpos - kpos_w
        ok_w = (dist >= 0) & (dist < NSA_WINDOW) & (kpos_w >= 0)
        s_w = jnp.where(ok_w, _dot_nt(ql, wb) + slopes * (cw - n_buf).astype(F32), NEG)
        ok_wn = (col <= step) & (col < n_new)
        s_wn = jnp.where(ok_wn, _dot_nt(ql, wn_ref[...]) + slopes * col.astype(F32), NEG)
        m = jnp.maximum(jnp.max(s_w, axis=1, keepdims=True), jnp.max(s_wn, axis=1, keepdims=True))
        p_w = jnp.where(ok_w, jnp.exp(s_w - m), 0.0)
        p_wn = jnp.where(ok_wn, jnp.exp(s_wn - m), 0.0)
        den = jnp.sum(p_w, axis=1, keepdims=True) + jnp.sum(p_wn, axis=1, keepdims=True)
        o_w = (_dot_hi(_dot(p_w.astype(BF16), wb), rep) + _dot(p_wn.astype(BF16), vwn_ref[...])) / den

        gl = gate_ref[...]
        o_ref[...] = (_gate_expand(gl, 0) * _fold_heads(o_c, rq) + _gate_expand(gl, 1) * _fold_heads(o_s, rq)
                      + _gate_expand(gl, 2) * _fold_heads(o_w, rq))


def _nsa_sample(page_table, ql, qr, gate, kn, vn, wbuf, wn, vwn, p, cache_k, cache_v, layer, n_pool, n_new):
    b, n_pages = page_table.shape
    n_steps = n_pages // NSA_PAGES
    past = n_pages * PAGE_SIZE
    base = layer * n_pool

    def page_spec(r):
        return pl.BlockSpec((None, PAGE_SIZE, LANES), lambda bi, s, pt: (base + pt[bi, s * NSA_PAGES + r], 0, 0))

    per_b = lambda a: pl.BlockSpec((None,) + a.shape[1:], lambda bi, s, pt: (bi, 0, 0))
    const = lambda a: pl.BlockSpec(a.shape, lambda bi, s, pt: (0,) * a.ndim)
    consts = [p["pe_k_t"], p["pe_v_t"], p["w_k_t"], p["w_v_t"], p["k_gain_pad"]]
    seq_ops = [ql, qr, gate, kn, vn, wbuf, wn, vwn]
    grid_spec = pltpu.PrefetchScalarGridSpec(
        num_scalar_prefetch=1,
        grid=(b, n_steps),
        in_specs=[per_b(a) for a in seq_ops] + [const(a) for a in consts]
                 + [page_spec(r) for r in range(NSA_PAGES)] * 2,
        out_specs=pl.BlockSpec((None, SAMPLE_ROWS, BRANCH_WIDTH), lambda bi, s, pt: (bi, 0, 0)),
        scratch_shapes=[pltpu.VMEM((past, LANES), F32), pltpu.VMEM((past, LANES), F32),
                        pltpu.VMEM((HEAD_DIM, past), BF16), pltpu.VMEM((HEAD_DIM, past), BF16)],
    )
    return pl.pallas_call(
        functools.partial(_nsa_sample_kernel, n_steps=n_steps, n_new=n_new),
        grid_spec=grid_spec,
        out_shape=jax.ShapeDtypeStruct((b, SAMPLE_ROWS, BRANCH_WIDTH), F32),
        compiler_params=_cparams("parallel", "arbitrary"),
        name="nsa_sample",
    )(page_table, *seq_ops, *consts, *([cache_k] * NSA_PAGES), *([cache_v] * NSA_PAGES))


def _nsa_sample_mix(u3, page_table, wbuf, p, cache_k, cache_v, layer, n_pool):
    b, t, _ = u3.shape
    qs, newk, newv, win, newk_b, win_b, vslc, vwin = _nsa_prep(u3.reshape(b * t, -1), p["q_gain"], p["k_gain"], b * t)
    q4 = qs.reshape(b, t, 2, N_HEADS, LANES).transpose(2, 0, 3, 1, 4)
    q4 = jnp.pad(q4, ((0, 0), (0, 0), (0, 0), (0, SAMPLE_ROWS - t), (0, 0)))
    ql, qr = q4.reshape(2, b, N_HEADS * SAMPLE_ROWS, LANES)
    gate = jnp.pad(u3[..., SEG_NSA_GATE:SEG_NSA_GATE + LANES], ((0, 0), (0, SAMPLE_ROWS - t), (0, 0)))
    page_rows = lambda a: jnp.pad(a.reshape(b, t, -1), ((0, 0), (0, PAGE_SIZE - t), (0, 0)))
    o = _nsa_sample(page_table, ql, qr, gate, page_rows(newk_b), page_rows(vslc), wbuf, page_rows(win_b),
                    page_rows(vwin), p, cache_k, cache_v, layer, n_pool, t)
    return o[:, :t], newk, newv, win


def _nsa_params(q_norm, k_norm, cmp_pe, cmp_w):
    lane_pad = lambda a: jnp.pad(a, ((0, 0),) * (a.ndim - 1) + ((0, LANES - HEAD_DIM),))
    wk = cmp_w[0].reshape(NSA_BLOCK, HEAD_DIM, HEAD_DIM)
    wv = cmp_w[1].reshape(NSA_BLOCK, HEAD_DIM, HEAD_DIM)
    pad_rows = lambda a: jnp.pad(a, ((0, 0), (0, LANES - HEAD_DIM), (0, 0)))

    def per_dim(w):
        base = w.transpose(1, 0, 2)
        width = base.shape[2]
        out = jnp.zeros((HEAD_DIM, PAGE_SIZE, 2 * width), F32)
        for c in range(PAGE_SIZE // NSA_BLOCK):
            out = out.at[:, c * NSA_BLOCK:(c + 1) * NSA_BLOCK, c * width:(c + 1) * width].set(base)
        return out.astype(BF16)

    return dict(q_gain=jnp.tile(q_norm, N_HEADS)[None, :], k_gain=jnp.tile(k_norm, 6)[None, :],
                k_gain_pad=lane_pad(k_norm[None, :]), pe_k=lane_pad(cmp_pe[0]), pe_v=lane_pad(cmp_pe[1]),
                w_k=pad_rows(lane_pad(wk)).astype(BF16), w_v=pad_rows(jnp.tile(wv, (1, 1, N_HEADS))).astype(BF16),
                pe_k_t=jnp.tile(cmp_pe[0].T, (1, 2)), pe_v_t=jnp.tile(cmp_pe[1].T, (1, 2)),
                w_k_t=per_dim(lane_pad(wk)), w_v_t=per_dim(jnp.tile(wv, (1, 1, N_HEADS))))


SSM_HIST = 8
GROUP_LANES = BRANCH_WIDTH // SSM_GROUPS


def _ssd_kernel(xbc_ref, z_ref, dt_ref, dtt_ref, hist0_ref, h0_ref, cw_ref, cb_ref, bias_ref, biast_ref,
                a_ref, at_ref, dskip_ref, ng_ref, y_ref, hout_ref, hist_ref, state_ref, *, n_valid):
    c = pl.program_id(1)
    q = xbc_ref.shape[0]

    @pl.when(c == 0)
    def _():
        hist_ref[0:SSM_HIST, :] = hist0_ref[...]
        state_ref[...] = h0_ref[...]

    hist_ref[SSM_HIST:SSM_HIST + q, :] = xbc_ref[...]
    full = hist_ref[...]
    conv = cw_ref[3:4, :] * full + cb_ref[...]
    for k in range(1, SSM_CONV):
        conv = conv + cw_ref[3 - k:4 - k, :] * pltpu.roll(full, k, axis=0)
    hist_ref[0:SSM_HIST, :] = full[q:q + SSM_HIST, :]
    act = _silu(conv[SSM_HIST:, :])
    xs = act[:, 0:BRANCH_WIDTH]
    xs_b = xs.astype(BF16)
    bm = act[:, BRANCH_WIDTH:BRANCH_WIDTH + SSM_GROUPS * SSM_STATE].astype(BF16)
    cm = act[:, BRANCH_WIDTH + SSM_GROUPS * SSM_STATE:].astype(BF16)

    dt = jnp.where(_iota((q, LANES), 0) < n_valid, _softplus(dt_ref[...] + bias_ref[...]), 0.0)
    dta = dt * -jnp.exp(a_ref[...])
    dtt = jnp.where(_iota((SSM_HIST, q), 1) < n_valid, _softplus(dtt_ref[...] + biast_ref[...]), 0.0)
    dtat = dtt * -jnp.exp(at_ref[...])
    tril = jnp.where(_iota((q, q), 0) >= _iota((q, q), 1), 1.0, 0.0)
    triu = jnp.where(_iota((q, q), 0) <= _iota((q, q), 1), 1.0, 0.0)
    expand = jnp.where(_iota((LANES, BRANCH_WIDTH), 0) == _iota((LANES, BRANCH_WIDTH), 1) // HEAD_DIM, 1.0, 0.0)
    dt_e = _dot_hi(dt, expand)
    cum = _dot_hi(tril, dta)
    cum_e = _dot_hi(cum, expand)
    cum_t = _dot_hi(dtat, triu)
    causal = _iota((q, q), 0) >= _iota((q, q), 1)
    lane_head = _iota((1, BRANCH_WIDTH), 1) // HEAD_DIM

    heads = range(N_HEADS)
    grams = [_dot_nt(cm[:, g * SSM_STATE:(g + 1) * SSM_STATE], bm[:, g * SSM_STATE:(g + 1) * SSM_STATE])
             for g in range(SSM_GROUPS)]
    cum_l = [_dot_hi(cum, jnp.where(_iota((LANES, q), 0) == h, 1.0, 0.0)) for h in heads]
    decay = [jnp.where(causal, jnp.exp(jnp.minimum(cum_l[h] - cum_t[h:h + 1, :], 0.0)), 0.0) for h in heads]
    scores = [(grams[h // (N_HEADS // SSM_GROUPS)] * decay[h] * dtt[h:h + 1, :]).astype(BF16) for h in heads]
    intra = [_dot(scores[h], xs_b) for h in heads]
    y = dskip_ref[...] * xs
    for h in heads:
        y = y + jnp.where(lane_head == h, intra[h], 0.0)
    state = state_ref[...]
    inter = jnp.concatenate(
        [_dot(cm[:, g * SSM_STATE:(g + 1) * SSM_STATE],
              state[:, g * GROUP_LANES:(g + 1) * GROUP_LANES].astype(BF16)) for g in range(SSM_GROUPS)], axis=1)
    y = y + jnp.exp(cum_e) * inter

    cum_last = cum_e[q - 1:q, :]
    xw = (xs * jnp.exp(cum_last - cum_e) * dt_e).astype(BF16)
    contrib = jnp.concatenate(
        [lax.dot_general(bm[:, g * SSM_STATE:(g + 1) * SSM_STATE], xw[:, g * GROUP_LANES:(g + 1) * GROUP_LANES],
                         (((0,), (0,)), ((), ())), preferred_element_type=F32) for g in range(SSM_GROUPS)], axis=1)
    state_ref[...] = state * jnp.exp(cum_last) + contrib

    y = y * _silu(z_ref[...])
    parts = []
    for g in range(SSM_GROUPS):
        yg = y[:, g * GROUP_LANES:(g + 1) * GROUP_LANES]
        parts.append(yg * lax.rsqrt(jnp.mean(yg * yg, axis=-1, keepdims=True) + NORM_EPS))
    y_ref[...] = jnp.concatenate(parts, axis=1) * ng_ref[...]

    @pl.when(c == pl.num_programs(1) - 1)
    def _():
        hout_ref[...] = state_ref[...]


def _ssd(u3, dtt, hist0, h0, p, *, q, n_valid):
    b, t, _ = u3.shape
    per_b = lambda shape: pl.BlockSpec((None,) + shape, lambda bi, c: (bi, 0, 0))
    col = lambda width, seg: pl.BlockSpec((None, q, width), lambda bi, c: (bi, c, seg // width))
    consts = [p["conv_w"], p["conv_b"], p["dt_bias"], p["dt_bias_t"], p["a"], p["a_t"], p["d_skip"], p["norm_g"]]
    return pl.pallas_call(
        functools.partial(_ssd_kernel, n_valid=n_valid),
        grid=(b, t // q),
        in_specs=[col(SSM_CONV_DIM, SEG_SSM_XBC), col(BRANCH_WIDTH, SEG_SSM_Z), col(LANES, SEG_SSM_DT),
                  pl.BlockSpec((None, SSM_HIST, q), lambda bi, c: (bi, 0, c)),
                  per_b((SSM_HIST, SSM_CONV_DIM)), per_b((SSM_STATE, BRANCH_WIDTH))]
                 + [_const_spec(a.shape) for a in consts],
        out_specs=[pl.BlockSpec((None, q, BRANCH_WIDTH), lambda bi, c: (bi, c, 0)),
                   per_b((SSM_STATE, BRANCH_WIDTH))],
        out_shape=[jax.ShapeDtypeStruct((b, t, BRANCH_WIDTH), F32),
                   jax.ShapeDtypeStruct((b, SSM_STATE, BRANCH_WIDTH), F32)],
        scratch_shapes=[pltpu.VMEM((SSM_HIST + q, SSM_CONV_DIM), F32), pltpu.VMEM((SSM_STATE, BRANCH_WIDTH), F32)],
        compiler_params=_cparams("parallel", "arbitrary"),
        name="ssd",
    )(u3, u3, u3, dtt, hist0, h0, *consts)


def _head_sum_matrix(scale):
    same = _iota((BRANCH_WIDTH, BRANCH_WIDTH), 0) // HEAD_DIM == _iota((BRANCH_WIDTH, BRANCH_WIDTH), 1) // HEAD_DIM
    return jnp.where(same, scale, 0.0)


def _rwkv_prep_kernel(u_ref, up_ref, p1_ref, mu_ref, w0_ref, w2_ref, a0_ref, a2_ref, g2_ref, kk_ref, ka_ref, rk_ref,
                      r_out, k_out, v_out, d_out, nkk_out, kka_out, g_out, bonus_out, *, seq_tiles, period):
    tm = u_ref.shape[0]
    u = u_ref[...]
    rolled = pltpu.roll(u, 1, axis=0)
    if seq_tiles > 0:
        first = (pl.program_id(0) % seq_tiles) == 0
        carry_in = up_ref[FFN_HALO - 1:FFN_HALO, :] * jnp.where(first, 0.0, 1.0)
        prev = jnp.where(_iota((tm, 1), 0) == 0, carry_in, rolled)
    else:
        prev = jnp.where(_iota((tm, 1), 0) % period >= 1, rolled, p1_ref[...])
    us = u + (prev - u) * mu_ref[...]
    r = us[:, 0:BRANCH_WIDTH]
    k = us[:, BRANCH_WIDTH:2 * BRANCH_WIDTH]
    v = us[:, 2 * BRANCH_WIDTH:3 * BRANCH_WIDTH]
    wa = us[:, 3 * BRANCH_WIDTH:3 * BRANCH_WIDTH + LANES]
    gd = us[:, 3 * BRANCH_WIDTH + LANES:]
    is_w = _iota((1, LANES), 1) < RWKV_W_LORA
    w_lora = _dot(jnp.where(is_w, jnp.tanh(wa), 0.0).astype(BF16), w2_ref[...])
    a_lora = _dot(jnp.where(is_w, 0.0, wa).astype(BF16), a2_ref[...])
    w_raw = -_softplus(-(w0_ref[...] + w_lora)) - 0.5
    log_decay = -jnp.exp(w_raw)
    a = _sigmoid(a0_ref[...] + a_lora)
    g = _dot(_sigmoid(gd).astype(BF16), g2_ref[...])
    head_sum = _head_sum_matrix(1.0)
    kk = k * kk_ref[...]
    kk = kk / jnp.maximum(jnp.sqrt(_dot_hi(kk * kk, head_sum)), 1e-12)
    k_mod = k * (1.0 + (a - 1.0) * ka_ref[...])
    r_out[...] = r
    k_out[...] = k_mod
    v_out[...] = v
    d_out[...] = log_decay
    nkk_out[...] = -kk
    kka_out[...] = kk * a
    g_out[...] = g
    bonus_out[...] = _dot_hi(r * k_mod * rk_ref[...], head_sum) * v


def _rwkv_prep(u, p1, p, *, tm, seq_tiles, period):
    n = u.shape[0]
    hb = tm // FFN_HALO
    seg = SEG_RWKV // RWKV_IN
    if seq_tiles > 0:
        up_spec = pl.BlockSpec((FFN_HALO, RWKV_IN), lambda i: (jnp.maximum(i * hb - 1, 0), seg))
        p_spec = _const_spec(p1.shape)
    else:
        up_spec = pl.BlockSpec((FFN_HALO, RWKV_IN), lambda i: (0, seg))
        p_spec = pl.BlockSpec((tm, RWKV_IN), lambda i: (i, 0))
    consts = [p["mu"], p["w0"], p["w2"], p["a0"], p["a2"], p["g2"], p["k_k"], p["k_a"], p["r_k"]]
    out = pl.BlockSpec((tm, BRANCH_WIDTH), lambda i: (i, 0))
    return pl.pallas_call(
        functools.partial(_rwkv_prep_kernel, seq_tiles=seq_tiles, period=period),
        grid=(n // tm,),
        in_specs=[pl.BlockSpec((tm, RWKV_IN), lambda i: (i, seg)), up_spec, p_spec]
                 + [_const_spec(a.shape) for a in consts],
        out_specs=[out] * 8,
        out_shape=[jax.ShapeDtypeStruct((n, BRANCH_WIDTH), F32)] * 8,
        compiler_params=_cparams("parallel"),
        name="rwkv_prep",
    )(u, u, p1, *consts)


RWKV_CHAINS = 8
RWKV_TB = 128


def _rwkv_scan_kernel(r_ref, k_ref, d_ref, nkk_ref, kka_ref, vt_ref, s0_ref, yt_ref, sout_ref, s_ref, *, n_steps):
    @pl.when(pl.program_id(1) == 0)
    def _():
        s_ref[...] = s0_ref[...]

    yt_ref[...] = jnp.zeros_like(yt_ref)
    lane = _iota((HEAD_DIM, RWKV_TB), 1)

    def step(t, _):
        for c in range(RWKV_CHAINS):
            row = lambda ref: ref[c, pl.ds(t, 1), :]
            s = s_ref[c]
            sa = jnp.sum(s * row(nkk_ref), axis=1, keepdims=True)
            v_col = jnp.sum(jnp.where(lane == t, vt_ref[c], 0.0), axis=1, keepdims=True)
            s = s * jnp.exp(row(d_ref)) + sa * row(kka_ref) + v_col * row(k_ref)
            s_ref[c] = s
            y_col = jnp.sum(s * row(r_ref), axis=1, keepdims=True)
            yt_ref[c] = jnp.where(lane == t, y_col, yt_ref[c])
        return 0

    lax.fori_loop(0, n_steps, step, 0)

    @pl.when(pl.program_id(1) == pl.num_programs(1) - 1)
    def _():
        sout_ref[...] = s_ref[...]


def _rwkv_scan(r, k, d, nkk, kka, vt, s0, *, rows, n_steps):
    chains, t_rows, _ = r.shape
    t_lanes = vt.shape[2]
    row_spec = pl.BlockSpec((RWKV_CHAINS, rows, HEAD_DIM), lambda ci, tb: (ci, tb, 0))
    lane_spec = pl.BlockSpec((RWKV_CHAINS, HEAD_DIM, RWKV_TB), lambda ci, tb: (ci, 0, tb))
    state_spec = pl.BlockSpec((RWKV_CHAINS, HEAD_DIM, HEAD_DIM), lambda ci, tb: (ci, 0, 0))
    return pl.pallas_call(
        functools.partial(_rwkv_scan_kernel, n_steps=n_steps),
        grid=(chains // RWKV_CHAINS, t_lanes // RWKV_TB),
        in_specs=[row_spec] * 5 + [lane_spec, state_spec],
        out_specs=[lane_spec, state_spec],
        out_shape=[jax.ShapeDtypeStruct((chains, HEAD_DIM, t_lanes), F32),
                   jax.ShapeDtypeStruct((chains, HEAD_DIM, HEAD_DIM), F32)],
        scratch_shapes=[pltpu.VMEM((RWKV_CHAINS, HEAD_DIM, HEAD_DIM), F32)],
        compiler_params=_cparams("parallel", "arbitrary"),
        name="rwkv_scan",
    )(r, k, d, nkk, kka, vt, s0)


RWKV_CHUNK = 64


def _split2(x):
    hi = x.astype(BF16)
    return hi, (x - hi.astype(F32)).astype(BF16)


def _dot_split(a2, b2, dims=(((1,), (0,)), ((), ()))):
    (ah, al), (bh, bl) = a2, b2
    dg = lambda x, y: lax.dot_general(x, y, dims, preferred_element_type=F32)
    return dg(ah, bh) + dg(ah, bl) + dg(al, bh)


def _rwkv_chunk_step(r, k, v, ld, a, b, s):
    c = RWKV_CHUNK
    rows = N_HEADS * c
    cum = _dot_hi(jnp.where(_iota((c, c), 0) >= _iota((c, c), 1), 1.0, 0.0), ld)
    cum_last = cum[c - 1:c, :]
    g_inv = jnp.exp(-cum)
    g_end = jnp.exp(cum_last - cum)
    own = (_iota((rows, BRANCH_WIDTH), 0) // c) == (_iota((rows, BRANCH_WIDTH), 1) // HEAD_DIM)
    stack = lambda x: jnp.where(own, jnp.concatenate([x] * N_HEADS, axis=0), 0.0)
    tile = lambda x: jnp.concatenate([x] * N_HEADS, axis=0)
    ar = jnp.concatenate([stack(a * jnp.exp(cum - ld)), stack(r * jnp.exp(cum))], axis=0)
    bt = b * g_inv
    kt = k * g_inv
    nt = (((1,), (1,)), ((), ()))
    ar2 = _split2(ar)
    g_b = _dot_split(ar2, _split2(tile(bt)), nt)
    g_k = _dot_split(ar2, _split2(tile(kt)), nt)
    yield
    step_r = _iota((rows, rows), 0) % c
    step_c = _iota((rows, rows), 1) % c
    same = (_iota((rows, rows), 0) // c) == (_iota((rows, rows), 1) // c)
    strict = same & (step_c < step_r)
    incl = same & (step_c <= step_r)
    a_ab = jnp.where(strict, g_b[0:rows], 0.0)
    a_ak = jnp.where(strict, g_k[0:rows], 0.0)
    a_rb = jnp.where(incl, g_b[rows:], 0.0)
    a_rk = jnp.where(incl, g_k[rows:], 0.0)

    vbd = stack(v)
    vbd2 = _split2(vbd)
    w = _dot_split(ar2, _split2(s), nt)
    av = _dot_split(_split2(jnp.concatenate([a_ak, a_rk], axis=0)), vbd2)
    yield

    power2 = _split2(a_ab)
    u = w[0:rows] + av[0:rows]
    u = u + _dot_split(power2, _split2(u))
    for _ in range(int(math.log2(c)) - 1):
        yield
        power2 = _split2(_dot_split(power2, power2))
        u = u + _dot_split(power2, _split2(u))

    yield
    u2 = _split2(u)
    y = w[rows:] + av[rows:] + _dot_split(_split2(a_rb), u2)
    out = y[0:c]
    for h in range(1, N_HEADS):
        out = out + y[h * c:(h + 1) * c]
    cat2 = lambda p, q: (jnp.concatenate([p[0], q[0]], axis=0), jnp.concatenate([p[1], q[1]], axis=0))
    upd = _dot_split(cat2(u2, vbd2), cat2(_split2(stack(b * g_end)), _split2(stack(k * g_end))),
                     (((0,), (0,)), ((), ())))
    return out, s * jnp.exp(cum_last) + upd


def _rwkv_chunk_kernel(r_ref, k_ref, v_ref, ld_ref, a_ref, b_ref, y_ref, sout_ref, s_ref):
    @pl.when(pl.program_id(0) == 0)
    def _():
        s_ref[...] = jnp.zeros_like(s_ref)

    bsz = s_ref.shape[0]
    steps = [_rwkv_chunk_step(r_ref[bi], k_ref[bi], v_ref[bi], ld_ref[bi], a_ref[bi], b_ref[bi], s_ref[bi])
             for bi in range(bsz)]
    done = {}
    while len(done) < bsz:
        for bi in range(bsz):
            if bi not in done:
                try:
                    next(steps[bi])
                except StopIteration as fin:
                    done[bi] = fin.value
    for bi in range(bsz):
        y_ref[bi], s_ref[bi] = done[bi]

    @pl.when(pl.program_id(0) == pl.num_programs(0) - 1)
    def _():
        sout_ref[...] = s_ref[...]


def _rwkv_chunked(r, k, v, ld, a, b):
    bsz, t, _ = r.shape
    blk = pl.BlockSpec((bsz, RWKV_CHUNK, BRANCH_WIDTH), lambda ci: (0, ci, 0))
    st = pl.BlockSpec((bsz, BRANCH_WIDTH, BRANCH_WIDTH), lambda ci: (0, 0, 0))
    return pl.pallas_call(
        _rwkv_chunk_kernel,
        grid=(t // RWKV_CHUNK,),
        in_specs=[blk] * 6,
        out_specs=[blk, st],
        out_shape=[jax.ShapeDtypeStruct((bsz, t, BRANCH_WIDTH), F32),
                   jax.ShapeDtypeStruct((bsz, BRANCH_WIDTH, BRANCH_WIDTH), F32)],
        scratch_shapes=[pltpu.VMEM((bsz, BRANCH_WIDTH, BRANCH_WIDTH), F32)],
        compiler_params=_cparams("arbitrary"),
        name="rwkv_chunk",
    )(r, k, v, ld, a, b)


def _rwkv_post_kernel(y_ref, bonus_ref, g_ref, lnw_ref, lnb_ref, o_ref):
    y = y_ref[...]
    head_mean = _head_sum_matrix(1.0 / HEAD_DIM)
    cen = y - _dot_hi(y, head_mean)
    var = _dot_hi(cen * cen, head_mean)
    yn = cen * lax.rsqrt(var + RWKV_LN_EPS) * lnw_ref[...] + lnb_ref[...]
    o_ref[...] = (yn + bonus_ref[...]) * g_ref[...]


def _rwkv_post(y, bonus, g, ln_w, ln_b, tm):
    n = y.shape[0]
    blk = pl.BlockSpec((tm, BRANCH_WIDTH), lambda i: (i, 0))
    return pl.pallas_call(
        _rwkv_post_kernel,
        grid=(n // tm,),
        in_specs=[blk, blk, blk, _const_spec(ln_w.shape), _const_spec(ln_b.shape)],
        out_specs=blk,
        out_shape=jax.ShapeDtypeStruct((n, BRANCH_WIDTH), F32),
        compiler_params=_cparams("parallel"),
        name="rwkv_post",
    )(y, bonus, g, ln_w, ln_b)


def _rwkv_mix(u2, p1, s0, p, ln_w, ln_b, *, b, t, tm, seq_tiles, period):
    r, k, v, d, nkk, kka, g, bonus = _rwkv_prep(u2, p1, p, tm=tm, seq_tiles=seq_tiles, period=period)
    if s0 is None:
        r3 = lambda x: x.reshape(b, t, BRANCH_WIDTH)
        y, s_wide = _rwkv_chunked(r3(r), r3(k), r3(v), r3(d), r3(nkk), r3(kka))
        s_wide = s_wide.reshape(b, N_HEADS, HEAD_DIM, N_HEADS, HEAD_DIM)
        s_fin = jnp.stack([s_wide[:, h, :, h, :] for h in range(N_HEADS)], axis=1)
        o = _rwkv_post(y.reshape(b * t, BRANCH_WIDTH), bonus, g, ln_w, ln_b, tm)
        return o, s_fin.reshape(b * N_HEADS, HEAD_DIM, HEAD_DIM)
    t_rows = max(t, SAMPLE_ROWS)
    t_lanes = -(-t // RWKV_TB) * RWKV_TB

    def chains(a):
        a = a.reshape(b, t, N_HEADS, HEAD_DIM).transpose(0, 2, 1, 3).reshape(b * N_HEADS, t, HEAD_DIM)
        return a

    rows = [jnp.pad(chains(a), ((0, 0), (0, t_rows - t), (0, 0))) for a in (r, k, d, nkk, kka)]
    vt = jnp.pad(chains(v).transpose(0, 2, 1), ((0, 0), (0, 0), (0, t_lanes - t)))
    yt, s_fin = _rwkv_scan(*rows, vt, s0, rows=min(t_rows, RWKV_TB), n_steps=min(t, RWKV_TB))
    y = yt[:, :, :t].transpose(0, 2, 1).reshape(b, N_HEADS, t, HEAD_DIM).transpose(0, 2, 1, 3)
    o = _rwkv_post(y.reshape(b * t, BRANCH_WIDTH), bonus, g, ln_w, ln_b, tm)
    return o, s_fin


def _rwkv_params(mu, w0, w2, a0, a2, g2, k_k, k_a, r_k):
    row = lambda v: v.reshape(1, -1)
    zeros = jnp.zeros((RWKV_W_LORA, BRANCH_WIDTH), F32)
    return dict(mu=row(mu), w0=row(w0), w2=jnp.concatenate([w2, zeros]).astype(BF16), a0=row(a0),
                a2=jnp.concatenate([zeros, a2]).astype(BF16), g2=g2.astype(BF16), k_k=row(k_k), k_a=row(k_a),
                r_k=row(r_k))


def _ssd_params(conv_w, conv_b, dt_bias, a_log, d_skip, norm_g):
    a = a_log.astype(F32)
    pad_row = lambda v: jnp.pad(v, (0, LANES - N_HEADS))[None, :]
    pad_col = lambda v: jnp.pad(v, (0, SSM_HIST - N_HEADS))[:, None]
    return dict(conv_w=conv_w, conv_b=conv_b[None, :], dt_bias=pad_row(dt_bias), dt_bias_t=pad_col(dt_bias),
                a=pad_row(a), a_t=pad_col(a), d_skip=jnp.repeat(d_skip, HEAD_DIM)[None, :], norm_g=norm_g[None, :])


PROMPT_TM = 512
INPROJ_TM = 1024


def _sb_sample_mix(u3, page_table, cache_k, cache_v, layer, n_pool):
    b, t, _ = u3.shape
    q = u3[..., SEG_SB:SEG_SB + BRANCH_WIDTH]
    k = u3[..., SEG_SB + BRANCH_WIDTH:SEG_SB + 2 * BRANCH_WIDTH]
    v = u3[..., SEG_SB + 2 * BRANCH_WIDTH:SEG_SB + 3 * BRANCH_WIDTH]
    qt = jnp.tile(jnp.pad(q, ((0, 0), (0, SAMPLE_ROWS - t), (0, 0))), (1, N_HEADS, 1))
    rows = N_HEADS * SAMPLE_ROWS
    own = (np.arange(rows)[:, None] // SAMPLE_ROWS) == (np.arange(BRANCH_WIDTH)[None, :] // HEAD_DIM)
    qbd = jnp.where(own[None], qt, 0.0)
    page_rows = lambda a: jnp.pad(a, ((0, 0), (0, PAGE_SIZE - t), (0, 0)))
    o = _sb_sample(page_table, qbd, page_rows(k), page_rows(v), cache_k, cache_v, layer, n_pool)
    return o[:, :t]


def kernel(x_prompt, x_sample, cache_nsa_k, cache_nsa_v, cache_sb_k, cache_sb_v, state_win_kv, state_ssm_conv, state_ssm, state_rwkv_shift, state_rwkv, state_ffn_conv, page_table, norm1, norm2, w_in, nsa_q_norm, nsa_k_norm, nsa_cmp_pe, nsa_cmp_w, ssm_conv_w, ssm_conv_b, ssm_dt_bias, ssm_a_log, ssm_d, ssm_norm, rwkv_mu, rwkv_w0, rwkv_w2, rwkv_a0, rwkv_a2, rwkv_g2, rwkv_k_k, rwkv_k_a, rwkv_r_k, rwkv_ln_w, rwkv_ln_b, w_branch, w_out, ffn_up, ffn_conv_w, ffn_conv_b, ffn_down):
    bp, tp, _ = x_prompt.shape
    bs, ts, _ = x_sample.shape
    depth, n_pool = cache_nsa_k.shape[:2]
    n_pages = page_table.shape[1]
    past = n_pages * PAGE_SIZE
    n_buf = state_win_kv.shape[2]
    n_p, n_s = bp * tp, bs * ts
    assert tp % INPROJ_TM == 0 and tp >= NSA_WINDOW and past % NSA_BLOCK == 0 and ts < SAMPLE_ROWS
    assert n_pages % SB_PAGES == 0 and n_pages % NSA_PAGES == 0 and n_s % FFN_HALO == 0

    perm = _in_perm()
    w_in_p = jnp.where(perm >= 0, jnp.take(w_in, np.maximum(perm, 0), axis=2), 0.0).astype(BF16)
    token_minor = lambda c: jnp.transpose(c, (0, 1, 3, 4, 2)).reshape(depth * n_pool, -1, PAGE_SIZE)
    nsa_ck = token_minor(cache_nsa_k)
    nsa_cv = token_minor(cache_nsa_v)
    sb_ck = token_minor(cache_sb_k)
    sb_cv = token_minor(cache_sb_v)
    dummy = jnp.zeros((FFN_HALO, LANES), F32)
    seg = lambda u, start, width: u[..., start:start + width]

    xp = x_prompt.reshape(n_p, D_MODEL)
    xs = x_sample.reshape(n_s, D_MODEL)
    outs = [[] for _ in range(20)]
    for l in range(depth):
        nsa_p = _nsa_params(nsa_q_norm[l], nsa_k_norm[l], nsa_cmp_pe[l], nsa_cmp_w[l])
        ssd_p = _ssd_params(ssm_conv_w[l], ssm_conv_b[l], ssm_dt_bias[l], ssm_a_log[l], ssm_d[l], ssm_norm[l])
        rwkv_p = _rwkv_params(rwkv_mu[l], rwkv_w0[l], rwkv_w2[l], rwkv_a0[l], rwkv_a2[l], rwkv_g2[l],
                              rwkv_k_k[l], rwkv_k_a[l], rwkv_r_k[l].reshape(-1))
        ln_w, ln_b = rwkv_ln_w[l][None, :], rwkv_ln_b[l][None, :]
        wb, wo = w_branch[l].astype(BF16), w_out[l].astype(BF16)
        wup, wdn = ffn_up[l].astype(BF16), ffn_down[l].astype(BF16)

        up = _inproj(xp, norm1[l][None, :], w_in_p[l], INPROJ_TM)
        u3 = up.reshape(bp, tp, D_IN_PAD)
        r3 = lambda a: a.reshape(bp, tp, -1)
        qs, newk, newv, win, newk_b, win_b, vslc, vwin = _nsa_prep(up, nsa_p["q_gain"], nsa_p["k_gain"], PROMPT_TM)
        kc, vc = _nsa_compress_prompt(r3(newk), r3(newv), nsa_p)
        front = lambda a: jnp.pad(r3(a), ((0, 0), (NSA_WINDOW, 0), (0, 0)))
        o_a = _nsa_prompt(r3(qs), u3, kc, vc, r3(newk_b), r3(vslc), front(win_b), front(vwin))
        dtt = jnp.pad(jnp.swapaxes(seg(u3, SEG_SSM_DT, N_HEADS), 1, 2), ((0, 0), (0, SSM_HIST - N_HEADS), (0, 0)))
        o_b, ssm_fin = _ssd(u3, dtt, jnp.zeros((bp, SSM_HIST, SSM_CONV_DIM), F32),
                            jnp.zeros((bp, SSM_STATE, BRANCH_WIDTH), F32), ssd_p, q=SSM_CHUNK, n_valid=SSM_CHUNK)
        o_c, rwkv_fin = _rwkv_mix(up, dummy, None, rwkv_p, ln_w, ln_b,
                                  b=bp, t=tp, tm=PROMPT_TM, seq_tiles=tp // PROMPT_TM, period=0)
        sb_k = seg(u3, SEG_SB + BRANCH_WIDTH, BRANCH_WIDTH)
        sb_v = seg(u3, SEG_SB + 2 * BRANCH_WIDTH, BRANCH_WIDTH)
        o_d = _sb_prompt(u3, sb_k.astype(BF16), sb_v.astype(BF16))
        flat = lambda a: a.reshape(n_p, BRANCH_WIDTH)
        xp = _merge(xp, [flat(o_a), flat(o_b), o_c, flat(o_d)], up, wb, wo, PROMPT_TM)
        xp, hs = _ffn(xp, dummy, dummy, norm2[l][None, :], wup, ffn_conv_w[l], ffn_conv_b[l][None, :], wdn,
                      tm=PROMPT_TM, seq_tiles=tp // PROMPT_TM, period=0, keep=FFN_HALO)
        prompt_states = (
            r3(newk).reshape(bp, tp, 2, HEAD_DIM), r3(newv).reshape(bp, tp, 2, HEAD_DIM),
            sb_k.reshape(bp, tp, N_HEADS, HEAD_DIM), sb_v.reshape(bp, tp, N_HEADS, HEAD_DIM),
            r3(win)[:, tp - min(NSA_WINDOW, tp):].reshape(bp, -1, 2, HEAD_DIM),
            seg(u3, SEG_SSM_XBC, SSM_CONV_DIM)[:, tp - (SSM_CONV - 1):],
            ssm_fin.reshape(bp, SSM_STATE, N_HEADS, HEAD_DIM).transpose(0, 2, 3, 1),
            seg(u3, SEG_RWKV, RWKV_IN)[:, tp - 1:],
            rwkv_fin.reshape(bp, N_HEADS, HEAD_DIM, HEAD_DIM),
            hs.reshape(bp, tp // PROMPT_TM, FFN_HALO, 2 * D_FF)[:, -1, FFN_HALO - (FFN_CONV - 1):])

        us = _inproj(xs, norm1[l][None, :], w_in_p[l], n_s)
        u3 = us.reshape(bs, ts, D_IN_PAD)
        o_a, newk, newv, win = _nsa_sample_mix(u3, page_table, state_win_kv[l].reshape(bs, n_buf, LANES), nsa_p,
                                               nsa_ck, nsa_cv, l, n_pool)
        t_pad = 2 * SAMPLE_ROWS
        u3_pad = jnp.pad(u3, ((0, 0), (0, t_pad - ts), (0, 0)))
        dtt = jnp.pad(jnp.swapaxes(seg(u3_pad, SEG_SSM_DT, N_HEADS), 1, 2), ((0, 0), (0, SSM_HIST - N_HEADS), (0, 0)))
        hist0 = jnp.pad(state_ssm_conv[l], ((0, 0), (SSM_HIST - (SSM_CONV - 1), 0), (0, 0)))
        h0 = state_ssm[l].transpose(0, 3, 1, 2).reshape(bs, SSM_STATE, BRANCH_WIDTH)
        o_b, ssm_fin = _ssd(u3_pad, dtt, hist0, h0, ssd_p, q=t_pad, n_valid=ts)
        shift_rows = jnp.pad(state_rwkv_shift[l], ((0, 0), (0, ts - 1), (0, 0))).reshape(n_s, RWKV_IN)
        o_c, rwkv_fin = _rwkv_mix(us, shift_rows, state_rwkv[l].reshape(bs * N_HEADS, HEAD_DIM, HEAD_DIM), rwkv_p,
                                  ln_w, ln_b, b=bs, t=ts, tm=n_s, seq_tiles=0, period=ts)
        o_d = _sb_sample_mix(u3, page_table, sb_ck, sb_cv, l, n_pool)
        flat = lambda a: a.reshape(n_s, BRANCH_WIDTH)
        xs = _merge(xs, [flat(o_a), flat(o_b[:, :ts]), o_c, flat(o_d)], us, wb, wo, n_s)
        conv_state = state_ffn_conv[l]
        prev1 = jnp.pad(conv_state[:, 1:2], ((0, 0), (0, ts - 1), (0, 0))).reshape(n_s, 2 * D_FF)
        prev2 = jnp.pad(conv_state, ((0, 0), (0, ts - 2), (0, 0))).reshape(n_s, 2 * D_FF)
        xs, hs = _ffn(xs, prev1, prev2, norm2[l][None, :], wup, ffn_conv_w[l], ffn_conv_b[l][None, :], wdn,
                      tm=n_s, seq_tiles=0, period=ts, keep=n_s)
        tail = lambda old, new, n: jnp.concatenate([old, new], axis=1)[:, -n:]
        sample_states = (
            newk.reshape(bs, ts, 2, HEAD_DIM), newv.reshape(bs, ts, 2, HEAD_DIM),
            seg(u3, SEG_SB + BRANCH_WIDTH, BRANCH_WIDTH).reshape(bs, ts, N_HEADS, HEAD_DIM),
            seg(u3, SEG_SB + 2 * BRANCH_WIDTH, BRANCH_WIDTH).reshape(bs, ts, N_HEADS, HEAD_DIM),
            tail(state_win_kv[l], win.reshape(bs, ts, 2, HEAD_DIM), n_buf),
            tail(state_ssm_conv[l], seg(u3, SEG_SSM_XBC, SSM_CONV_DIM), SSM_CONV - 1),
            ssm_fin.reshape(bs, SSM_STATE, N_HEADS, HEAD_DIM).transpose(0, 2, 3, 1),
            seg(u3, SEG_RWKV, RWKV_IN)[:, ts - 1:],
            rwkv_fin.reshape(bs, N_HEADS, HEAD_DIM, HEAD_DIM),
            tail(conv_state, hs.reshape(bs, ts, 2 * D_FF), FFN_CONV - 1))
        for j in range(10):
            outs[2 * j].append(prompt_states[j])
            outs[2 * j + 1].append(sample_states[j])

    return (xp.reshape(bp, tp, D_MODEL), xs.reshape(bs, ts, D_MODEL)) + tuple(jnp.stack(o) for o in outs)
```
